```python
import jax
import jax.numpy as jnp
from jax import lax
import numpy as np

D_MODEL = 2048
BATCH = 2
SEQ = 4096
DEPTH = 2

GRID_W = 64
CTX_LEN = 256

D_RWKV = 3 * D_MODEL // 8
D_POOL = D_MODEL // 4
D_ATT = 3 * D_MODEL // 8
D_MIX = D_RWKV + D_POOL + D_ATT

RWKV_HEAD_DIM = 64
RWKV_HEADS = D_RWKV // RWKV_HEAD_DIM
DECAY_RANK = 64
AAA_RANK = 64
GATE_RANK = 128
RWKV_GN_EPS = 64e-5

POOL_WINDOWS = (2, 4, 8, 16)
POOL_GROUPS = 4
POOL_GROUP_DIM = D_POOL // POOL_GROUPS

ATT_HEAD_DIM = 64
ATT_Q_HEADS = D_ATT // ATT_HEAD_DIM
ATT_KV_HEADS = ATT_Q_HEADS // 3
ATT_KV_DIM = ATT_KV_HEADS * ATT_HEAD_DIM
ATT_WINDOW = 128
ATT_BLOCK = 128
ROPE_BASE = 10000.0

FFN_HIDDEN = ((8 * D_MODEL + 3 * 256 - 1) // (3 * 256)) * 256
NORM_EPS = 1e-6

OFF_R = 0
OFF_K = D_RWKV
OFF_V = 2 * D_RWKV
OFF_WD = 3 * D_RWKV
OFF_AD = OFF_WD + 2 * DECAY_RANK
OFF_GD = OFF_AD + 2 * AAA_RANK
RWKV_COLS = OFF_GD + GATE_RANK
OFF_POOL = RWKV_COLS
OFF_AQ = OFF_POOL + D_POOL
OFF_AK = OFF_AQ + D_ATT
OFF_AV = OFF_AK + ATT_KV_DIM
D_IN = OFF_AV + ATT_KV_DIM

kernel_name = "hybrid_rwkv7_pool_swa_prefix_dit"


def rms_norm(x, g):
    xf = x.astype(jnp.float32)
    y = xf * lax.rsqrt(jnp.mean(xf * xf, axis=-1, keepdims=True) + NORM_EPS)
    return (y * g.astype(jnp.float32)).astype(x.dtype)


def swiglu(h, w_gu, w_down):
    gate, up = jnp.split(h @ w_gu, 2, axis=-1)
    return (jax.nn.silu(gate) * up) @ w_down


def centred_token_shift(z, mu):
    zp = jnp.pad(z, ((0, 0), (1, 1), (0, 0)))
    nb = 0.5 * (zp[:, :-2] + zp[:, 2:])
    return z + mu * (nb - z)


def rwkv_prepare(z, mu, w0, w_up, a0, a_up, g_up, k_k, k_a, r_k):
    B, T, _ = z.shape
    H, N = RWKV_HEADS, RWKV_HEAD_DIM
    z = centred_token_shift(z, mu)
    r = z[..., OFF_R:OFF_K].reshape(B, T, H, N)
    k = z[..., OFF_K:OFF_V]
    v = z[..., OFF_V:OFF_WD].reshape(B, T, H, N)
    wd = z[..., OFF_WD:OFF_AD].reshape(B, T, 2, DECAY_RANK)
    ad = z[..., OFF_AD:OFF_GD].reshape(B, T, 2, AAA_RANK)
    gd = z[..., OFF_GD:RWKV_COLS]
    w_log = -jax.nn.softplus(-(w0 + jnp.einsum('btdr,drc->btdc', jnp.tanh(wd), w_up))) - 0.5
    decay = jnp.exp(-jnp.exp(w_log.astype(jnp.float32))).reshape(B, T, 2, H, N)
    a = jax.nn.sigmoid(a0 + jnp.einsum('btdr,drc->btdc', ad, a_up))
    g = jnp.einsum('btr,rc->btc', jax.nn.sigmoid(gd), g_up)
    kk = (k * k_k).reshape(B, T, H, N).astype(jnp.float32)
    kk = kk / jnp.maximum(jnp.sqrt(jnp.sum(kk * kk, axis=-1, keepdims=True)), 1e-12)
    k_dir = (k[:, :, None, :] * (1.0 + (a - 1.0) * k_a)).reshape(B, T, 2, H, N)
    a = a.reshape(B, T, 2, H, N)
    a_vec = -kk
    b_dir = kk[:, :, None] * a
    bonus = jnp.sum(jnp.sum(r[:, :, None] * k_dir * r_k, axis=-1, keepdims=True) * v[:, :, None], axis=2)
    return r, k_dir, v, decay, a_vec, b_dir, g, bonus


def rwkv_scan(S0, r, decay, k, v, a_vec, b_vec, reverse):
    tm = lambda t: None if t is None else jnp.moveaxis(t.astype(jnp.float32), 1, 0)

    def step(S, inp):
        r_t, w_t, k_t, v_t, a_t, b_t = inp
        sa = jnp.einsum('bhij,bhj->bhi', S, a_t)
        S = S * w_t[:, :, None, :] + sa[..., None] * b_t[:, :, None, :] + v_t[..., None] * k_t[:, :, None, :]
        y = None if r_t is None else jnp.einsum('bhij,bhj->bhi', S, r_t)
        return S, y

    S, ys = lax.scan(step, S0, (tm(r), tm(decay), tm(k), tm(v), tm(a_vec), tm(b_vec)), reverse=reverse)
    return S, (None if r is None else jnp.moveaxis(ys, 0, 1))


def rwkv_readout(y, bonus, g, ln_g, ln_b):
    B, T, H, N = y.shape
    mu = jnp.mean(y, axis=-1, keepdims=True)
    var = jnp.mean(jnp.square(y - mu), axis=-1, keepdims=True)
    yn = ((y - mu) * lax.rsqrt(var + RWKV_GN_EPS)).reshape(B, T, H * N)
    yn = yn * ln_g + ln_b
    return ((yn + bonus.reshape(B, T, H * N)) * g).astype(g.dtype)


def rwkv_mix(prep, prep_c, ln_g, ln_b, emit_ctx):
    r, kd, v, decay, a_vec, b_dir, g, bonus = prep
    rc, kdc, vc, decayc, a_vec_c, b_dirc, gc, bonusc = prep_c
    B = r.shape[0]
    S0 = jnp.zeros((B, RWKV_HEADS, RWKV_HEAD_DIM, RWKV_HEAD_DIM), jnp.float32)
    y_lat = 0.0
    y_ctx = 0.0
    for d in range(2):
        rev = d == 1
        S_c, yc = rwkv_scan(S0, rc if emit_ctx else None, decayc[:, :, d], kdc[:, :, d], vc,
                            a_vec_c, b_dirc[:, :, d], rev)
        _, yl = rwkv_scan(S_c, r, decay[:, :, d], kd[:, :, d], v, a_vec, b_dir[:, :, d], rev)
        y_lat = y_lat + yl
        if emit_ctx:
            y_ctx = y_ctx + yc
    out = rwkv_readout(y_lat, bonus, g, ln_g, ln_b)
    out_c = rwkv_readout(y_ctx, bonusc, gc, ln_g, ln_b) if emit_ctx else None
    return out, out_c


def multiscale_pool(p, pool_w, pool_scale):
    B, T, _ = p.shape
    pf = p.astype(jnp.float32)
    cs = jnp.concatenate([jnp.zeros((B, 1, D_POOL), jnp.float32), jnp.cumsum(pf, axis=1)], axis=1)
    t = jnp.arange(T)
    means = []
    for gi, w in enumerate(POOL_WINDOWS):
        lo = jnp.clip(t - w // 2, 0, T)
        hi = jnp.clip(t - w // 2 + w, 0, T)
        csg = cs[..., gi * POOL_GROUP_DIM:(gi + 1) * POOL_GROUP_DIM]
        cnt = (hi - lo).astype(jnp.float32)[None, :, None]
        means.append((csg[:, hi] - csg[:, lo]) / cnt)
    mean = jnp.stack(means, axis=2)
    y = mean - pf.reshape(B, T, POOL_GROUPS, POOL_GROUP_DIM)
    y = jnp.einsum('btgc,gcd->btgd', y, pool_w.astype(jnp.float32)).reshape(B, T, D_POOL)
    return (y * pool_scale).astype(p.dtype)


def axial_rope_tables(T):
    rows = T // GRID_W
    row = jnp.repeat(jnp.arange(rows), GRID_W).astype(jnp.float32)
    col = jnp.tile(jnp.arange(GRID_W), rows).astype(jnp.float32)
    n_freq = ATT_HEAD_DIM // 4
    inv = ROPE_BASE ** (-jnp.arange(n_freq, dtype=jnp.float32) / n_freq)
    ang_r = row[:, None] * inv
    ang_c = col[:, None] * inv
    ang = jnp.concatenate([ang_r, ang_r, ang_c, ang_c], axis=-1)
    return jnp.cos(ang), jnp.sin(ang)


def apply_axial_rope(x, cos, sin):
    x1, x2, x3, x4 = jnp.split(x, 4, axis=-1)
    rot = jnp.concatenate([-x2, x1, -x4, x3], axis=-1)
    return (x * cos[:, None, :] + rot * sin[:, None, :]).astype(x.dtype)


def window_attention(q, k, v, kc, vc, sink):
    B, T, Hq, Dh = q.shape
    Hkv = k.shape[2]
    G = Hq // Hkv
    nb = T // ATT_BLOCK
    scale = Dh ** -0.5
    qb = q.reshape(B, nb, ATT_BLOCK, Hkv, G, Dh)
    pad = lambda t: jnp.pad(t, ((0, 0), (ATT_BLOCK, ATT_BLOCK), (0, 0), (0, 0))).reshape(B, nb + 2, ATT_BLOCK, Hkv, Dh)
    band = lambda t: jnp.concatenate([t[:, :-2], t[:, 1:-1], t[:, 2:]], axis=2)
    kw, vw = band(pad(k)), band(pad(v))
    s_loc = jnp.einsum('bnqhgd,bnkhd->bnhgqk', qb, kw).astype(jnp.float32) * scale
    qi = jnp.arange(ATT_BLOCK)[:, None]
    kj = jnp.arange(3 * ATT_BLOCK)[None, :]
    rel = kj - ATT_BLOCK - qi
    kpos = jnp.arange(nb)[:, None, None] * ATT_BLOCK - ATT_BLOCK + kj[None]
    valid = (jnp.abs(rel) <= ATT_WINDOW)[None] & (kpos >= 0) & (kpos < T)
    s_loc = jnp.where(valid[None, :, None, None], s_loc, -1e30)
    s_ctx = jnp.einsum('bnqhgd,bchd->bnhgqc', qb, kc).astype(jnp.float32) * scale
    s_sink = jnp.broadcast_to(sink.astype(jnp.float32).reshape(Hkv, G)[None, None, :, :, None, None],
                              s_loc.shape[:-1] + (1,))
    p = jax.nn.softmax(jnp.concatenate([s_loc, s_ctx, s_sink], axis=-1), axis=-1)
    n_loc = 3 * ATT_BLOCK
    n_ctx = kc.shape[1]
    p_loc = p[..., :n_loc].astype(v.dtype)
    p_ctx = p[..., n_loc:n_loc + n_ctx].astype(v.dtype)
    o = jnp.einsum('bnhgqk,bnkhd->bnqhgd', p_loc, vw) + jnp.einsum('bnhgqc,bchd->bnqhgd', p_ctx, vc)
    return o.reshape(B, T, Hq * Dh)


def context_attention(qc, kc, vc, sink):
    B, C, Hq, Dh = qc.shape
    Hkv = kc.shape[2]
    G = Hq // Hkv
    qg = qc.reshape(B, C, Hkv, G, Dh)
    s = jnp.einsum('bqhgd,bkhd->bhgqk', qg, kc).astype(jnp.float32) * (Dh ** -0.5)
    s_sink = jnp.broadcast_to(sink.astype(jnp.float32).reshape(Hkv, G)[None, :, :, None, None], s.shape[:-1] + (1,))
    p = jax.nn.softmax(jnp.concatenate([s, s_sink], axis=-1), axis=-1)[..., :C].astype(vc.dtype)
    o = jnp.einsum('bhgqk,bkhd->bqhgd', p, vc)
    return o.reshape(B, C, Hq * Dh)


def setup_inputs(seed: int = 0) -> dict:
    key = jax.random.key(seed)
    ks = jax.random.split(key, 32)
    L, D = DEPTH, D_MODEL
    nrm = lambda k, shape, s: jax.random.normal(k, shape, jnp.float32) * s
    return {
        "x": nrm(ks[0], (BATCH, SEQ, D), 1.0),
        "c": nrm(ks[1], (BATCH, D), 1.0),
        "ctx": nrm(ks[2], (BATCH, CTX_LEN, D), 1.0),
        "c_ctx": nrm(ks[3], (D,), 1.0),
        "ada_w": nrm(ks[4], (L, D, 6 * D), 0.5 * D ** -0.5),
        "ada_b": nrm(ks[5], (L, 6 * D), 0.02),
        "norm_mix_g": 1.0 + nrm(ks[6], (L, D), 0.05),
        "norm_ffn_g": 1.0 + nrm(ks[7], (L, D), 0.05),
        "w_in": nrm(ks[8], (L, D, D_IN), D ** -0.5),
        "rwkv_mu": jax.random.uniform(ks[9], (L, RWKV_COLS), jnp.float32, 0.2, 0.8),
        "rwkv_w0": jax.random.uniform(ks[10], (L, 2, D_RWKV), jnp.float32, -6.0, -1.0),
        "rwkv_w_up": nrm(ks[11], (L, 2, DECAY_RANK, D_RWKV), 0.1 * DECAY_RANK ** -0.5),
        "rwkv_a0": nrm(ks[12], (L, 2, D_RWKV), 0.3),
        "rwkv_a_up": nrm(ks[13], (L, 2, AAA_RANK, D_RWKV), 0.3 * AAA_RANK ** -0.5),
        "rwkv_g_up": nrm(ks[14], (L, GATE_RANK, D_RWKV), GATE_RANK ** -0.5),
        "rwkv_k_k": 0.85 + nrm(ks[15], (L, D_RWKV), 0.05),
        "rwkv_k_a": 1.0 + nrm(ks[16], (L, D_RWKV), 0.05),
        "rwkv_r_k": nrm(ks[17], (L, RWKV_HEADS, RWKV_HEAD_DIM), 0.1),
        "rwkv_ln_g": 1.0 + nrm(ks[18], (L, D_RWKV), 0.05),
        "rwkv_ln_b": nrm(ks[19], (L, D_RWKV), 0.02),
        "pool_w": nrm(ks[20], (L, POOL_GROUPS, POOL_GROUP_DIM, POOL_GROUP_DIM), POOL_GROUP_DIM ** -0.5),
        "pool_scale": 1.0 + nrm(ks[21], (L, D_POOL), 0.1),
        "attn_sink": nrm(ks[22], (L, ATT_Q_HEADS), 1.0),
        "w_out": nrm(ks[23], (L, D_MIX, D), D_MIX ** -0.5),
        "ffn_w_gu": nrm(ks[24], (L, D, 2 * FFN_HIDDEN), D ** -0.5),
        "ffn_w_down": nrm(ks[25], (L, FFN_HIDDEN, D), FFN_HIDDEN ** -0.5),
        "final_norm_g": 1.0 + nrm(ks[26], (D,), 0.05),
    }


def reference(x, c, ctx, c_ctx, ada_w, ada_b, norm_mix_g, norm_ffn_g, w_in, rwkv_mu, rwkv_w0, rwkv_w_up,
              rwkv_a0, rwkv_a_up, rwkv_g_up, rwkv_k_k, rwkv_k_a, rwkv_r_k, rwkv_ln_g, rwkv_ln_b, pool_w,
              pool_scale, attn_sink, w_out, ffn_w_gu, ffn_w_down, final_norm_g):
    B, T, _ = x.shape
    C = ctx.shape[1]
    cos, sin = axial_rope_tables(T)
    xc = ctx
    for l in range(DEPTH):
        last = l == DEPTH - 1
        mod = (jax.nn.silu(c) @ ada_w[l] + ada_b[l])[:, None, :]
        mod_c = jax.nn.silu(c_ctx) @ ada_w[l] + ada_b[l]
        sh1, sc1, gt1, sh2, sc2, gt2 = jnp.split(mod, 6, axis=-1)
        csh1, csc1, cgt1, csh2, csc2, cgt2 = jnp.split(mod_c, 6, axis=-1)

        h = rms_norm(x, norm_mix_g[l]) * (1.0 + sc1) + sh1
        hc = rms_norm(xc, norm_mix_g[l]) * (1.0 + csc1) + csh1
        z = h @ w_in[l]
        zc = hc @ w_in[l]

        rw = (rwkv_mu[l], rwkv_w0[l], rwkv_w_up[l], rwkv_a0[l], rwkv_a_up[l], rwkv_g_up[l],
              rwkv_k_k[l], rwkv_k_a[l], rwkv_r_k[l])
        prep = rwkv_prepare(z[..., :RWKV_COLS], *rw)
        prep_c = rwkv_prepare(zc[..., :RWKV_COLS], *rw)
        y_r, yc_r = rwkv_mix(prep, prep_c, rwkv_ln_g[l], rwkv_ln_b[l], not last)

        y_p = multiscale_pool(z[..., OFF_POOL:OFF_AQ], pool_w[l], pool_scale[l])

        q = apply_axial_rope(z[..., OFF_AQ:OFF_AK].reshape(B, T, ATT_Q_HEADS, ATT_HEAD_DIM), cos, sin)
        k = apply_axial_rope(z[..., OFF_AK:OFF_AV].reshape(B, T, ATT_KV_HEADS, ATT_HEAD_DIM), cos, sin)
        v = z[..., OFF_AV:D_IN].reshape(B, T, ATT_KV_HEADS, ATT_HEAD_DIM)
        kc = zc[..., OFF_AK:OFF_AV].reshape(B, C, ATT_KV_HEADS, ATT_HEAD_DIM)
        vc = zc[..., OFF_AV:D_IN].reshape(B, C, ATT_KV_HEADS, ATT_HEAD_DIM)
        y_a = window_attention(q, k, v, kc, vc, attn_sink[l])

        y = jnp.concatenate([y_r, y_p.astype(y_r.dtype), y_a.astype(y_r.dtype)], axis=-1)
        x = x + gt1 * (y @ w_out[l])
        h2 = rms_norm(x, norm_ffn_g[l]) * (1.0 + sc2) + sh2
        x = x + gt2 * swiglu(h2, ffn_w_gu[l], ffn_w_down[l])

        if not last:
            yc_p = multiscale_pool(zc[..., OFF_POOL:OFF_AQ], pool_w[l], pool_scale[l])
            qc = zc[..., OFF_AQ:OFF_AK].reshape(B, C, ATT_Q_HEADS, ATT_HEAD_DIM)
            yc_a = context_attention(qc, kc, vc, attn_sink[l])
            ycat = jnp.concatenate([yc_r, yc_p.astype(yc_r.dtype), yc_a.astype(yc_r.dtype)], axis=-1)
            xc = xc + cgt1 * (ycat @ w_out[l])
            hc2 = rms_norm(xc, norm_ffn_g[l]) * (1.0 + csc2) + csh2
            xc = xc + cgt2 * swiglu(hc2, ffn_w_gu[l], ffn_w_down[l])
    return rms_norm(x, final_norm_g)
```

```python
import functools

import numpy as np
import jax
import jax.numpy as jnp
from jax import lax
from jax.experimental import pallas as pl
from jax.experimental.pallas import tpu as pltpu

F32 = jnp.float32
BF16 = jnp.bfloat16

HEAD_DIM = 64
RWKV_HEADS = 12
D_RWKV = RWKV_HEADS * HEAD_DIM
LORA_RANK = 64
GATE_RANK = 128
D_POOL = 512
POOL_WINDOWS = (2, 4, 8, 16)
POOL_GROUP_DIM = 128
POOL_HALO = 8
ATT_Q_HEADS = 12
ATT_KV_HEADS = 4
ATT_GROUP = ATT_Q_HEADS // ATT_KV_HEADS
D_ATT = ATT_Q_HEADS * HEAD_DIM
ATT_KV_DIM = ATT_KV_HEADS * HEAD_DIM
ATT_BLOCK = 128
GRID_W = 64
ROPE_BASE = 10000.0
NORM_EPS = 1e-6
RWKV_GN_EPS = 64e-5

OFF_K = D_RWKV
OFF_V = 2 * D_RWKV
OFF_WD = 3 * D_RWKV
OFF_AD = OFF_WD + 2 * LORA_RANK
OFF_GD = OFF_AD + 2 * LORA_RANK
RWKV_COLS = OFF_GD + GATE_RANK
OFF_POOL = RWKV_COLS
OFF_AQ = OFF_POOL + D_POOL
OFF_AK = OFF_AQ + D_ATT
OFF_AV = OFF_AK + ATT_KV_DIM
D_IN = OFF_AV + ATT_KV_DIM

LANES = 128
SUBLANES = 8
MXU_DIM = 256
HEADS_PER_GROUP = MXU_DIM // HEAD_DIM
N_GROUPS = D_RWKV // MXU_DIM
CHUNK = MXU_DIM // HEADS_PER_GROUP
INV_BASE = 8
N_INV_MASKS = 1 + (CHUNK // INV_BASE).bit_length() - 1
VMEM_LIMIT = 56 * 1024 * 1024


def _cparams(sem, vmem=VMEM_LIMIT):
    return pltpu.CompilerParams(dimension_semantics=sem, vmem_limit_bytes=vmem)


def _dot(a, b):
    return jnp.dot(a, b, preferred_element_type=F32)


def _dot_nt(a, b):
    return lax.dot_general(a, b, (((1,), (1,)), ((), ())), preferred_element_type=F32)


def _dot_tn(a, b):
    return lax.dot_general(a, b, (((0,), (0,)), ((), ())), preferred_element_type=F32)


def _split3(x):
    x1 = x.astype(BF16)
    r1 = x - x1.astype(F32)
    x2 = r1.astype(BF16)
    x3 = (r1 - x2.astype(F32)).astype(BF16)
    return x1, x2, x3


def _dot_exact_rhs(x, m_bf16):
    x1, x2, x3 = _split3(x)
    return _dot(x1, m_bf16) + _dot(x2, m_bf16) + _dot(x3, m_bf16)


def _dot_exact_lhs(m_bf16, x):
    x1, x2, x3 = _split3(x)
    return _dot(m_bf16, x1) + _dot(m_bf16, x2) + _dot(m_bf16, x3)


def _head_sum(x, hs):
    parts = [_dot_exact_rhs(x[:, MXU_DIM * g:MXU_DIM * (g + 1)], hs) for g in range(N_GROUPS)]
    return jnp.concatenate(parts, axis=1)


def _rms_mod(x, g, sc, sh):
    ms = jnp.mean(x * x, axis=-1, keepdims=True)
    return x * lax.rsqrt(ms + NORM_EPS) * g * (1.0 + sc) + sh


def _mod_kernel(c_ref, w_ref, b_ref, o_ref):
    c = c_ref[...]
    s = (c * jax.nn.sigmoid(c)).astype(BF16)
    o_ref[...] = _dot(s, w_ref[...].astype(BF16)) + b_ref[...]


def _modulation(cvec, ada_w, ada_b):
    n_layers, d, n6 = ada_w.shape
    tn = 1024
    return pl.pallas_call(
        _mod_kernel,
        grid=(n_layers, n6 // tn),
        in_specs=[
            pl.BlockSpec((SUBLANES, d), lambda l, j: (0, 0)),
            pl.BlockSpec((None, d, tn), lambda l, j: (l, 0, j)),
            pl.BlockSpec((None, 1, tn), lambda l, j: (l, 0, j)),
        ],
        out_specs=pl.BlockSpec((None, SUBLANES, tn), lambda l, j: (l, 0, j)),
        out_shape=jax.ShapeDtypeStruct((n_layers, SUBLANES, n6), F32),
        compiler_params=_cparams(("arbitrary", "arbitrary")),
        name="adaln_modulation",
    )(cvec, ada_w, ada_b.reshape(n_layers, 1, n6))


def _rope(z, cos, sin_a, sin_b):
    outs = []
    for j in range(z.shape[1] // LANES):
        zb = z[:, LANES * j:LANES * (j + 1)]
        outs.append(zb * cos + pltpu.roll(zb, LANES - 16, axis=1) * sin_a + pltpu.roll(zb, 16, axis=1) * sin_b)
    return jnp.concatenate(outs, axis=1)


def _inproj_kernel(x_ref, g_ref, sc_ref, sh_ref, w_ref, cos_ref, sa_ref, sb_ref,
                   zr_ref, zp_ref, q_ref, kv_ref):
    h = _rms_mod(x_ref[...], g_ref[...], sc_ref[...], sh_ref[...]).astype(BF16)
    zr_ref[...] = _dot(h, w_ref[:, 0:RWKV_COLS])
    zp_ref[...] = _dot(h, w_ref[:, OFF_POOL:OFF_AQ])
    cos, sa, sb = cos_ref[...], sa_ref[...], sb_ref[...]
    zq = _dot(h, w_ref[:, OFF_AQ:OFF_AK])
    q_ref[...] = (_rope(zq, cos, sa, sb) * (HEAD_DIM ** -0.5)).astype(BF16)
    zk = _dot(h, w_ref[:, OFF_AK:OFF_AV])
    zv = _dot(h, w_ref[:, OFF_AV:D_IN])
    kv_ref[...] = jnp.concatenate([_rope(zk, cos, sa, sb), zv], axis=1).astype(BF16)


def _in_projection(x_all, mod3, mod_base, norm_g, w_in_b, rope, dims):
    rows, d = x_all.shape
    tm = dims["tm"]
    grp = dims["grp"]
    cos, sin_a, sin_b = rope

    def mod_spec(k):
        return pl.BlockSpec((None, 1, d), lambda i: (mod_base + grp(i, tm) * 6 + k, 0, 0))

    row_spec = lambda w: pl.BlockSpec((tm, w), lambda i: (i, 0))
    return pl.pallas_call(
        _inproj_kernel,
        grid=(rows // tm,),
        in_specs=[
            row_spec(d),
            pl.BlockSpec((1, d), lambda i: (0, 0)),
            mod_spec(1), mod_spec(0),
            pl.BlockSpec((d, D_IN), lambda i: (0, 0), pipeline_mode=pl.Buffered(1)),
            row_spec(LANES), row_spec(LANES), row_spec(LANES),
        ],
        out_specs=[row_spec(RWKV_COLS), row_spec(D_POOL), row_spec(D_ATT), row_spec(2 * ATT_KV_DIM)],
        out_shape=[
            jax.ShapeDtypeStruct((rows, RWKV_COLS), F32),
            jax.ShapeDtypeStruct((rows, D_POOL), F32),
            jax.ShapeDtypeStruct((rows, D_ATT), BF16),
            jax.ShapeDtypeStruct((rows, 2 * ATT_KV_DIM), BF16),
        ],
        compiler_params=_cparams(("arbitrary",)),
        name="norm_inproj_rope",
    )(x_all, norm_g.reshape(1, d), mod3, mod3, w_in_b, cos, sin_a, sin_b)


def _seq_edges(i, tm, dims):
    nc, c_len, t_len = dims["nc"], dims["c_len"], dims["t_len"]
    r0 = i * tm
    is_ctx = r0 < nc
    seq_len = jnp.where(is_ctx, c_len, t_len)
    pos0 = jnp.where(is_ctx, lax.rem(r0, c_len), lax.rem(jnp.maximum(r0 - nc, 0), t_len))
    return pos0, seq_len, pos0 == 0, pos0 + tm == seq_len


def _prep_kernel(z_ref, zp_ref, zn_ref, mu_ref, w0_ref, wup_ref, a0_ref, aup_ref, gup_ref,
                 kk_ref, ka_ref, rk_ref, hs_ref,
                 r_ref, v_ref, av_ref, g_ref, bonus_ref, kd_ref, b_ref, lw_ref, *, tm, dims):
    i = pl.program_id(0)
    _, _, first, last = _seq_edges(i, tm, dims)
    z = z_ref[...]
    prev_row = jnp.where(first, 0.0, zp_ref[SUBLANES - 1:SUBLANES, :])
    next_row = jnp.where(last, 0.0, zn_ref[0:1, :])
    row = lax.broadcasted_iota(jnp.int32, (tm, 1), 0)
    z_m1 = jnp.where(row == 0, prev_row, pltpu.roll(z, 1, axis=0))
    z_p1 = jnp.where(row == tm - 1, next_row, pltpu.roll(z, tm - 1, axis=0))
    zs = z + mu_ref[...] * (0.5 * (z_m1 + z_p1) - z)

    r = zs[:, 0:OFF_K]
    k = zs[:, OFF_K:OFF_V]
    v = zs[:, OFF_V:OFF_WD]
    wd = zs[:, OFF_WD:OFF_AD]
    ad = zs[:, OFF_AD:OFF_GD]
    gd = zs[:, OFF_GD:RWKV_COLS]

    w_pre = w0_ref[...] + _dot(jnp.tanh(wd).astype(BF16), wup_ref[...])
    w_log = jnp.minimum(w_pre, 0.0) - jnp.log(1.0 + jnp.exp(-jnp.abs(w_pre))) - 0.5
    lw = -jnp.exp(w_log)
    a = jax.nn.sigmoid(a0_ref[...] + _dot(ad.astype(BF16), aup_ref[...]))
    g_ref[...] = _dot(jax.nn.sigmoid(gd).astype(BF16), gup_ref[...])

    hs = hs_ref[...]
    kx = k * kk_ref[...]
    kn = kx / jnp.maximum(jnp.sqrt(_head_sum(kx * kx, hs)), 1e-12)
    rk = r * rk_ref[...]
    ka = ka_ref[...]
    dot_rk = None
    for d in range(2):
        a_d = a[:, D_RWKV * d:D_RWKV * (d + 1)]
        kd = k * (1.0 + (a_d - 1.0) * ka)
        kd_ref[d] = kd
        b_ref[d] = kn * a_d
        lw_ref[d] = lw[:, D_RWKV * d:D_RWKV * (d + 1)]
        dot_rk = rk * kd if dot_rk is None else dot_rk + rk * kd
    r_ref[...] = r
    v_ref[...] = v
    av_ref[...] = -kn
    bonus_ref[...] = _head_sum(dot_rk, hs) * v


def _rwkv_prepare(z_r, p, hs, dims):
    rows = z_r.shape[0]
    tm = dims["tm"]
    nblk8 = rows // SUBLANES
    per = tm // SUBLANES
    full = lambda shape: pl.BlockSpec(shape, lambda i: (0,) * len(shape))
    row_spec = pl.BlockSpec((tm, D_RWKV), lambda i: (i, 0))
    dir_spec = pl.BlockSpec((2, tm, D_RWKV), lambda i: (0, i, 0))
    one = jax.ShapeDtypeStruct((rows, D_RWKV), F32)
    two = jax.ShapeDtypeStruct((2, rows, D_RWKV), F32)
    return pl.pallas_call(
        functools.partial(_prep_kernel, tm=tm, dims=dims),
        grid=(rows // tm,),
        in_specs=[
            pl.BlockSpec((tm, RWKV_COLS), lambda i: (i, 0)),
            pl.BlockSpec((SUBLANES, RWKV_COLS), lambda i: (jnp.maximum(i * per - 1, 0), 0)),
            pl.BlockSpec((SUBLANES, RWKV_COLS), lambda i: (jnp.minimum((i + 1) * per, nblk8 - 1), 0)),
            full((1, RWKV_COLS)),
            full((1, 2 * D_RWKV)), full((2 * LORA_RANK, 2 * D_RWKV)),
            full((1, 2 * D_RWKV)), full((2 * LORA_RANK, 2 * D_RWKV)),
            full((GATE_RANK, D_RWKV)),
            full((1, D_RWKV)), full((1, D_RWKV)), full((1, D_RWKV)),
            full((MXU_DIM, MXU_DIM)),
        ],
        out_specs=[row_spec, row_spec, row_spec, row_spec, row_spec, dir_spec, dir_spec, dir_spec],
        out_shape=[one, one, one, one, one, two, two, two],
        compiler_params=_cparams(("arbitrary",)),
        name="rwkv_prepare",
    )(z_r, z_r, z_r, p["mu"], p["w0"], p["w_up"], p["a0"], p["a_up"], p["g_up"],
      p["k_k"], p["k_a"], p["r_k"], hs)


def _scan_kernel(r_ref, v_ref, av_ref, kd_ref, b_ref, lw_ref, tri_ref, ms_ref, mi_ref, bm_ref, eye_ref, im_ref,
                 y_ref, st_ref):
    s = pl.program_id(1)
    d = pl.program_id(2)

    @pl.when(s == 0)
    def _():
        st_ref[d] = jnp.zeros(st_ref.shape[1:], F32)

    r, v, av = r_ref[...], v_ref[...], av_ref[...]
    kd, bd, lw = kd_ref[...], b_ref[...], lw_ref[...]
    c = _dot_exact_lhs(tri_ref[...], lw)
    c_tot = jnp.sum(lw, axis=0, keepdims=True)
    e_c = jnp.exp(c)
    e_cx = jnp.exp(c - lw)
    e_nc = jnp.exp(-c)
    e_rt = jnp.exp(c_tot - c)
    e_tot = jnp.exp(c_tot)

    m_strict, m_incl = ms_ref[...], mi_ref[...]
    bm, eye = bm_ref[...], eye_ref[...]
    inv_masks = [im_ref[lvl] for lvl in range(N_INV_MASKS)]
    bm_b = bm.astype(BF16)

    def expand(xb):
        return jnp.tile(xb, (HEADS_PER_GROUP, 1)) * bm_b

    def collapse(x):
        out = x[0:CHUNK]
        for h in range(1, HEADS_PER_GROUP):
            out = out + x[CHUNK * h:CHUNK * (h + 1)]
        return out

    ys = []
    for g in range(N_GROUPS):
        sl = slice(MXU_DIM * g, MXU_DIM * (g + 1))
        r_t = r[:, sl] * e_c[:, sl]
        a_t = (av[:, sl] * e_cx[:, sl]).astype(BF16)
        k_t = (kd[:, sl] * e_nc[:, sl]).astype(BF16)
        b_t = (bd[:, sl] * e_nc[:, sl]).astype(BF16)
        k_h = (kd[:, sl] * e_rt[:, sl]).astype(BF16)
        b_h = (bd[:, sl] * e_rt[:, sl]).astype(BF16)
        v_b = v[:, sl].astype(BF16)
        ea, er, eb, ek, ev = expand(a_t), expand(r_t.astype(BF16)), expand(b_t), expand(k_t), expand(v_b)

        g_ab = _dot_nt(ea, eb) * m_strict
        g_ak = (_dot_nt(ea, ek) * m_strict).astype(BF16)
        g_rb = (_dot_nt(er, eb) * m_incl).astype(BF16)
        g_rk = (_dot_nt(er, ek) * m_incl).astype(BF16)

        d1 = (g_ab * inv_masks[0]).astype(BF16)
        inv = eye + g_ab * inv_masks[0]
        d2 = _dot(d1, d1).astype(BF16)
        inv = inv + _dot(inv.astype(BF16), d2)
        inv = inv + _dot(inv.astype(BF16), _dot(d2, d2).astype(BF16))
        for lvl in range(1, N_INV_MASKS):
            inv_b = inv.astype(BF16)
            off = (g_ab * inv_masks[lvl]).astype(BF16)
            inv = inv + _dot(_dot(inv_b, off).astype(BF16), inv_b)
        inv_b = inv.astype(BF16)

        x1 = _dot(g_ak, ev).astype(BF16)
        e_w = _dot(inv_b, ea)
        e_u = _dot(inv_b, x1)
        e_wb, e_ub = e_w.astype(BF16), e_u.astype(BF16)
        q_eff = r_t + collapse(_dot(g_rb, e_wb))
        y_loc = collapse(_dot(g_rb, e_ub) + _dot(g_rk, ev))
        w_c = collapse(e_w).astype(BF16)
        u_c = collapse(e_u).astype(BF16)

        m_tr = _dot_tn(b_h, w_c) * bm + eye * e_tot[:, sl]
        n_tr = _dot_tn(jnp.concatenate([b_h, k_h], axis=0), jnp.concatenate([u_c, v_b], axis=0)) * bm

        z0 = st_ref[d, g]
        z0_b = z0.astype(BF16)
        ys.append(_dot(q_eff.astype(BF16), z0_b) + y_loc)
        st_ref[d, g] = _dot(m_tr.astype(BF16), z0_b) + n_tr
    y_ref[...] = jnp.concatenate(ys, axis=1)


def _scan_constants():
    i = np.arange(MXU_DIM)
    blk = (i[:, None] // CHUNK) == (i[None, :] // CHUNK)
    t, s = i[:, None] % CHUNK, i[None, :] % CHUNK
    strict = np.stack([blk & (t > s), blk & (t < s)])
    incl = np.stack([blk & (t >= s), blk & (t <= s)])
    j = np.arange(CHUNK)
    tri = np.stack([j[:, None] >= j[None, :], j[:, None] <= j[None, :]])
    inv_masks = [(i[:, None] // INV_BASE) == (i[None, :] // INV_BASE)]
    n = INV_BASE
    while n < CHUNK:
        inv_masks.append(((i[:, None] // (2 * n)) == (i[None, :] // (2 * n))) & ((i[:, None] // n) != (i[None, :] // n)))
        n *= 2
    assert len(inv_masks) == N_INV_MASKS
    return (jnp.asarray(tri, BF16), jnp.asarray(strict, F32), jnp.asarray(incl, F32),
            jnp.asarray(blk, F32), jnp.asarray(np.eye(MXU_DIM), F32), jnp.asarray(np.stack(inv_masks), F32))


def _rwkv_scan(r, v, av, kd, b, lw, dims):
    rows = r.shape[0]
    n_b, ncc, nct = dims["batch"], dims["c_len"] // CHUNK, dims["t_len"] // CHUNK
    tri, m_strict, m_incl, bm, eye, inv_masks = _scan_constants()

    def chunk(bi, s, d):
        in_ctx = s < ncc
        j_ctx = jnp.where(d == 0, s, ncc - 1 - s)
        j_lat = jnp.where(d == 0, s - ncc, nct - 1 - (s - ncc))
        return jnp.where(in_ctx, bi * ncc + j_ctx, n_b * ncc + bi * nct + j_lat)

    shared = pl.BlockSpec((CHUNK, D_RWKV), lambda bi, s, d: (chunk(bi, s, d), 0))
    per_dir = pl.BlockSpec((None, CHUNK, D_RWKV), lambda bi, s, d: (d, chunk(bi, s, d), 0))
    const2 = lambda n: pl.BlockSpec((None, n, n), lambda bi, s, d: (d, 0, 0))
    const1 = pl.BlockSpec((MXU_DIM, MXU_DIM), lambda bi, s, d: (0, 0))
    return pl.pallas_call(
        _scan_kernel,
        grid=(n_b, ncc + nct, 2),
        in_specs=[shared, shared, shared, per_dir, per_dir, per_dir,
                  const2(CHUNK), const2(MXU_DIM), const2(MXU_DIM), const1, const1,
                  pl.BlockSpec((N_INV_MASKS, MXU_DIM, MXU_DIM), lambda bi, s, d: (0, 0, 0))],
        out_specs=per_dir,
        out_shape=jax.ShapeDtypeStruct((2, rows, D_RWKV), F32),
        scratch_shapes=[pltpu.VMEM((2, N_GROUPS, MXU_DIM, MXU_DIM), F32)],
        compiler_params=_cparams(("arbitrary", "arbitrary", "arbitrary")),
        name="rwkv_chunk_scan",
    )(r, v, av, kd, b, lw, tri, m_strict, m_incl, bm, eye, inv_masks)


def _pool_kernel(z_ref, zp_ref, zn_ref, w_ref, sc_ref, o_ref, *, tm, dims):
    i = pl.program_id(0)
    pos0, seq_len, first, last = _seq_edges(i, tm, dims)
    prev = jnp.where(first, 0.0, zp_ref[...])
    nxt = jnp.where(last, 0.0, zn_ref[...])
    ext = jnp.concatenate([prev, z_ref[...], nxt], axis=0)
    n = tm + 2 * POOL_HALO
    t = pos0 + lax.broadcasted_iota(jnp.int32, (tm, 1), 0)
    outs = []
    for gi, w in enumerate(POOL_WINDOWS):
        e = ext[:, POOL_GROUP_DIM * gi:POOL_GROUP_DIM * (gi + 1)]
        acc, width = e, 1
        while width < w:
            acc = acc + pltpu.roll(acc, n - width, axis=0)
            width *= 2
        win = pltpu.roll(acc, w // 2, axis=0)[POOL_HALO:POOL_HALO + tm]
        lo = jnp.clip(t - w // 2, 0, seq_len)
        hi = jnp.clip(t - w // 2 + w, 0, seq_len)
        y = win / (hi - lo).astype(F32) - e[POOL_HALO:POOL_HALO + tm]
        outs.append(_dot(y.astype(BF16), w_ref[gi].astype(BF16)))
    o_ref[...] = (jnp.concatenate(outs, axis=1) * sc_ref[...]).astype(BF16)


def _pool(z_p, pool_w, pool_scale, dims):
    rows = z_p.shape[0]
    tm = dims["tm"]
    nblk8 = rows // SUBLANES
    per = tm // SUBLANES
    return pl.pallas_call(
        functools.partial(_pool_kernel, tm=tm, dims=dims),
        grid=(rows // tm,),
        in_specs=[
            pl.BlockSpec((tm, D_POOL), lambda i: (i, 0)),
            pl.BlockSpec((SUBLANES, D_POOL), lambda i: (jnp.maximum(i * per - 1, 0), 0)),
            pl.BlockSpec((SUBLANES, D_POOL), lambda i: (jnp.minimum((i + 1) * per, nblk8 - 1), 0)),
            pl.BlockSpec(pool_w.shape, lambda i: (0, 0, 0)),
            pl.BlockSpec((1, D_POOL), lambda i: (0, 0)),
        ],
        out_specs=pl.BlockSpec((tm, D_POOL), lambda i: (i, 0)),
        out_shape=jax.ShapeDtypeStruct((rows, D_POOL), BF16),
        compiler_params=_cparams(("arbitrary",)),
        name="multiscale_pool",
    )(z_p, z_p, z_p, pool_w, pool_scale.reshape(1, D_POOL))


def _attn_kernel(sink_ref, q_ref, *rest, windowed, nblk):
    if windowed:
        kp_ref, kc_ref, kn_ref, kx_ref, o_ref = rest
        kv = jnp.concatenate([kp_ref[...], kc_ref[...], kn_ref[...], kx_ref[...]], axis=0)
    else:
        kx_ref, o_ref = rest
        kv = kx_ref[...]
    n = pl.program_id(1)
    q = q_ref[...]
    k_all, v_all = kv[:, 0:ATT_KV_DIM], kv[:, ATT_KV_DIM:2 * ATT_KV_DIM]
    nk = kv.shape[0]
    rows = ATT_GROUP * ATT_BLOCK
    if windowed:
        qi = lax.broadcasted_iota(jnp.int32, (ATT_BLOCK, nk), 0)
        kj = lax.broadcasted_iota(jnp.int32, (ATT_BLOCK, nk), 1)
        rel = kj - ATT_BLOCK - qi
        k_lo = jnp.where(n == 0, ATT_BLOCK, 0)
        k_hi = jnp.where(n == nblk - 1, 2 * ATT_BLOCK, 3 * ATT_BLOCK)
        bad = jnp.where(rel > ATT_BLOCK, 1, 0) + jnp.where(rel < -ATT_BLOCK, 1, 0) \
            + jnp.where(kj < k_lo, 1, 0) + jnp.where(kj >= k_hi, 1, 0)
        bad = jnp.where(kj >= 3 * ATT_BLOCK, 0, bad)
        neg = jnp.tile(jnp.where(bad > 0, -1e30, 0.0).astype(F32), (ATT_GROUP, 1))
    lane = lax.shift_right_logical(lax.broadcasted_iota(jnp.int32, (1, ATT_KV_DIM), 1), HEAD_DIM.bit_length() - 1)
    grow = lax.shift_right_logical(lax.broadcasted_iota(jnp.int32, (rows, 1), 0), ATT_BLOCK.bit_length() - 1)
    outs = [jnp.zeros((ATT_BLOCK, ATT_KV_DIM), F32) for _ in range(ATT_GROUP)]
    for hk in range(ATT_KV_HEADS):
        lm_f = (lane == hk).astype(F32)
        lm_b = lm_f.astype(BF16)
        lhs = jnp.concatenate([q[:, ATT_KV_DIM * g:ATT_KV_DIM * (g + 1)] * lm_b for g in range(ATT_GROUP)], axis=0)
        s = _dot_nt(lhs, k_all)
        if windowed:
            s = s + neg
        sk = jnp.where(grow == 0, sink_ref[ATT_GROUP * hk],
                       jnp.where(grow == 1, sink_ref[ATT_GROUP * hk + 1], sink_ref[ATT_GROUP * hk + 2]))
        m = jnp.maximum(jnp.max(s, axis=1, keepdims=True), sk)
        p = jnp.exp(s - m)
        den = jnp.sum(p, axis=1, keepdims=True) + jnp.exp(sk - m)
        pv = _dot(p.astype(BF16), v_all) / den
        for g in range(ATT_GROUP):
            outs[g] = outs[g] + pv[ATT_BLOCK * g:ATT_BLOCK * (g + 1)] * lm_f
    o_ref[...] = jnp.concatenate(outs, axis=1).astype(BF16)


def _attention(q, kv, sink, dims, windowed):
    rows = q.shape[0]
    n_b, c_len, t_len, nc = dims["batch"], dims["c_len"], dims["t_len"], dims["nc"]
    seq = t_len if windowed else c_len
    nblk = seq // ATT_BLOCK
    base = (nc // ATT_BLOCK) if windowed else 0
    qrow = lambda bi, n: (base + bi * nblk + n, 0)
    ctx_spec = pl.BlockSpec((c_len, 2 * ATT_KV_DIM), lambda bi, n: (bi, 0))
    blk = lambda f: pl.BlockSpec((ATT_BLOCK, 2 * ATT_KV_DIM), f)
    in_specs = [pl.BlockSpec(memory_space=pltpu.SMEM), pl.BlockSpec((ATT_BLOCK, D_ATT), qrow)]
    args = [sink, q]
    if windowed:
        in_specs += [
            blk(lambda bi, n: (base + bi * nblk + jnp.maximum(n - 1, 0), 0)),
            blk(qrow),
            blk(lambda bi, n: (base + bi * nblk + jnp.minimum(n + 1, nblk - 1), 0)),
        ]
        args += [kv, kv, kv]
    in_specs.append(ctx_spec)
    args.append(kv)
    out_rows = n_b * seq
    return pl.pallas_call(
        functools.partial(_attn_kernel, windowed=windowed, nblk=nblk),
        grid=(n_b, nblk),
        in_specs=in_specs,
        out_specs=pl.BlockSpec((ATT_BLOCK, D_ATT), lambda bi, n: (bi * nblk + n, 0)),
        out_shape=jax.ShapeDtypeStruct((out_rows, D_ATT), BF16),
        compiler_params=_cparams(("arbitrary", "arbitrary")),
        name="window_attention" if windowed else "context_attention",
    )(*args)


def _outproj_kernel(yd_ref, bonus_ref, g_ref, yp_ref, ya_ref, x_ref, w_ref, lng_ref, lnb_ref, hs_ref,
                    gt_ref, ng_ref, sc_ref, sh_ref, xo_ref, ho_ref):
    hs = hs_ref[...]
    y = yd_ref[0] + yd_ref[1]
    inv_n = 1.0 / HEAD_DIM
    dev = y - _head_sum(y, hs) * inv_n
    var = _head_sum(dev * dev, hs) * inv_n
    yn = dev * lax.rsqrt(var + RWKV_GN_EPS) * lng_ref[...] + lnb_ref[...]
    y_r = ((yn + bonus_ref[...]) * g_ref[...]).astype(BF16)
    acc = (_dot(y_r, w_ref[0:D_RWKV])
           + _dot(yp_ref[...], w_ref[D_RWKV:D_RWKV + D_POOL])
           + _dot(ya_ref[...], w_ref[D_RWKV + D_POOL:D_RWKV + D_POOL + D_ATT]))
    x_new = x_ref[...] + gt_ref[...] * acc
    xo_ref[...] = x_new
    ho_ref[...] = _rms_mod(x_new, ng_ref[...], sc_ref[...], sh_ref[...]).astype(BF16)


def _out_projection(yd, bonus, g, y_p, y_a, x_all, w_out_b, ln_g, ln_b, hs, mod3, mod_base, norm_g,
                    dims, first_tile, att_first_tile):
    rows, d = x_all.shape
    tm = dims["tm"]
    grp = dims["grp"]
    n_tiles = rows // tm - first_tile
    off = lambda w: pl.BlockSpec((tm, w), lambda i: (i + first_tile, 0))

    def mod_spec(k):
        return pl.BlockSpec((None, 1, d), lambda i: (mod_base + grp(i + first_tile, tm) * 6 + k, 0, 0))

    vec = lambda w: pl.BlockSpec((1, w), lambda i: (0, 0))
    out_rows = n_tiles * tm
    return pl.pallas_call(
        _outproj_kernel,
        grid=(n_tiles,),
        in_specs=[
            pl.BlockSpec((2, tm, D_RWKV), lambda i: (0, i + first_tile, 0)),
            off(D_RWKV), off(D_RWKV), off(D_POOL),
            pl.BlockSpec((tm, D_ATT), lambda i: (i + first_tile - att_first_tile, 0)),
            off(d),
            pl.BlockSpec((d, d), lambda i: (0, 0), pipeline_mode=pl.Buffered(1)),
            vec(D_RWKV), vec(D_RWKV),
            pl.BlockSpec((MXU_DIM, MXU_DIM), lambda i: (0, 0)),
            mod_spec(2), vec(d), mod_spec(4), mod_spec(3),
        ],
        out_specs=[pl.BlockSpec((tm, d), lambda i: (i, 0)), pl.BlockSpec((tm, d), lambda i: (i, 0))],
        out_shape=[jax.ShapeDtypeStruct((out_rows, d), F32), jax.ShapeDtypeStruct((out_rows, d), BF16)],
        compiler_params=_cparams(("arbitrary",)),
        name="readout_outproj_norm",
    )(yd, bonus, g, y_p, y_a, x_all, w_out_b, ln_g.reshape(1, -1), ln_b.reshape(1, -1), hs,
      mod3, norm_g.reshape(1, d), mod3, mod3)


def _ffn_kernel(h_ref, x_ref, wg_ref, wu_ref, wd_ref, gt_ref, fg_ref, o_ref, acc_ref, *, final):
    j = pl.program_id(1)
    h = h_ref[...]
    gate = _dot(h, wg_ref[...])
    up = _dot(h, wu_ref[...])
    act = (gate * jax.nn.sigmoid(gate) * up).astype(BF16)
    part = _dot(act, wd_ref[...])

    @pl.when(j == 0)
    def _():
        acc_ref[...] = part

    @pl.when(j > 0)
    def _():
        acc_ref[...] += part

    @pl.when(j == pl.num_programs(1) - 1)
    def _():
        x_new = x_ref[...] + gt_ref[...] * acc_ref[...]
        if final:
            ms = jnp.mean(x_new * x_new, axis=-1, keepdims=True)
            x_new = x_new * lax.rsqrt(ms + NORM_EPS) * fg_ref[...]
        o_ref[...] = x_new


def _ffn(h2, x_new, w_gu_b, w_down_b, mod3, mod_base, final_g, dims, first_tile, final):
    rows, d = x_new.shape
    tmf, th = dims["tmf"], dims["th"]
    hidden = w_down_b.shape[0]
    nh = hidden // th
    grp = dims["grp"]
    return pl.pallas_call(
        functools.partial(_ffn_kernel, final=final),
        grid=(rows // tmf, nh),
        in_specs=[
            pl.BlockSpec((tmf, d), lambda i, j: (i, 0)),
            pl.BlockSpec((tmf, d), lambda i, j: (i, 0)),
            pl.BlockSpec((d, th), lambda i, j: (0, j)),
            pl.BlockSpec((d, th), lambda i, j: (0, nh + j)),
            pl.BlockSpec((th, d), lambda i, j: (j, 0)),
            pl.BlockSpec((None, 1, d), lambda i, j: (mod_base + grp(i + first_tile, tmf) * 6 + 5, 0, 0)),
            pl.BlockSpec((1, d), lambda i, j: (0, 0)),
        ],
        out_specs=pl.BlockSpec((tmf, d), lambda i, j: (i, 0)),
        out_shape=jax.ShapeDtypeStruct((rows, d), F32),
        scratch_shapes=[pltpu.VMEM((tmf, d), F32)],
        compiler_params=_cparams(("arbitrary", "arbitrary")),
        name="swiglu_ffn_final" if final else "swiglu_ffn",
    )(h2, x_new, w_gu_b, w_gu_b, w_down_b, mod3, final_g.reshape(1, d))


def _rope_tables(dims):
    t_len, nc, n_b = dims["t_len"], dims["nc"], dims["batch"]
    grid_rows = t_len // GRID_W
    row = jnp.repeat(jnp.arange(grid_rows), GRID_W).astype(F32)
    col = jnp.tile(jnp.arange(GRID_W), grid_rows).astype(F32)
    n_freq = HEAD_DIM // 4
    inv = ROPE_BASE ** (-jnp.arange(n_freq, dtype=F32) / n_freq)
    ang_r = row[:, None] * inv
    ang_c = col[:, None] * inv
    ang = jnp.concatenate([ang_r, ang_r, ang_c, ang_c], axis=-1)
    ang = jnp.tile(ang, (1, LANES // HEAD_DIM))
    cos, sin = jnp.cos(ang), jnp.sin(ang)
    first_half = (jnp.arange(LANES) % 32) < 16
    sin_a = jnp.where(first_half, -sin, 0.0)
    sin_b = jnp.where(first_half, 0.0, sin)
    flat = lambda tab, fill: jnp.concatenate([jnp.full((nc, LANES), fill, F32), jnp.tile(tab, (n_b, 1))], axis=0)
    return flat(cos, 1.0), flat(sin_a, 0.0), flat(sin_b, 0.0)


def _block_diag2(w):
    z = jnp.zeros_like(w[0])
    return jnp.concatenate([jnp.concatenate([w[0], z], axis=1), jnp.concatenate([z, w[1]], axis=1)], axis=0)


def kernel(x, c, ctx, c_ctx, ada_w, ada_b, norm_mix_g, norm_ffn_g, w_in, rwkv_mu, rwkv_w0, rwkv_w_up, rwkv_a0,
           rwkv_a_up, rwkv_g_up, rwkv_k_k, rwkv_k_a, rwkv_r_k, rwkv_ln_g, rwkv_ln_b, pool_w, pool_scale,
           attn_sink, w_out, ffn_w_gu, ffn_w_down, final_norm_g):
    n_b, t_len, d = x.shape
    c_len = ctx.shape[1]
    depth = w_in.shape[0]
    nc = n_b * c_len
    rows = nc + n_b * t_len
    tm = 256 if c_len % 256 == 0 else c_len
    tmf = nc
    assert n_b + 1 <= SUBLANES and c_len % tm == 0 and t_len % tm == 0 and t_len % tmf == 0
    assert tm % ATT_BLOCK == 0 and c_len % CHUNK == 0 and t_len % GRID_W == 0 and d == w_out.shape[1]

    def grp(i, tile):
        r0 = i * tile
        return jnp.where(r0 < nc, 0, 1 + jnp.maximum(r0 - nc, 0) // t_len)

    hidden = ffn_w_down.shape[1]
    th = 512 if hidden % 512 == 0 else hidden
    dims = dict(batch=n_b, c_len=c_len, t_len=t_len, nc=nc, tm=tm, tmf=tmf, th=th, grp=grp)

    cvec = jnp.concatenate([c_ctx[None, :], c, jnp.zeros((SUBLANES - 1 - n_b, d), F32)], axis=0)
    mod = _modulation(cvec, ada_w, ada_b)
    mod3 = mod.reshape(depth * SUBLANES * 6, 1, d)

    x_all = jnp.concatenate([ctx.reshape(nc, d), x.reshape(n_b * t_len, d)], axis=0)
    rope = _rope_tables(dims)
    hs = jnp.asarray(np.kron(np.eye(HEADS_PER_GROUP), np.ones((HEAD_DIM, HEAD_DIM))), BF16)

    out = None
    for l in range(depth):
        last = l == depth - 1
        mod_base = l * SUBLANES * 6
        wq = w_in[l][:, OFF_AQ:OFF_AK].reshape(d, ATT_KV_HEADS, ATT_GROUP, HEAD_DIM)
        wq = wq.transpose(0, 2, 1, 3).reshape(d, D_ATT)
        w_in_b = jnp.concatenate([w_in[l][:, :OFF_AQ], wq, w_in[l][:, OFF_AK:]], axis=1).astype(BF16)
        wo_att = w_out[l][D_RWKV + D_POOL:].reshape(ATT_KV_HEADS, ATT_GROUP, HEAD_DIM, d)
        wo_att = wo_att.transpose(1, 0, 2, 3).reshape(D_ATT, d)
        w_out_b = jnp.concatenate([w_out[l][:D_RWKV + D_POOL], wo_att], axis=0).astype(BF16)
        w_gu_b = ffn_w_gu[l].astype(BF16)
        w_down_b = ffn_w_down[l].astype(BF16)
        rw = dict(
            mu=rwkv_mu[l].reshape(1, RWKV_COLS),
            w0=rwkv_w0[l].reshape(1, 2 * D_RWKV),
            w_up=_block_diag2(rwkv_w_up[l]).astype(BF16),
            a0=rwkv_a0[l].reshape(1, 2 * D_RWKV),
            a_up=_block_diag2(rwkv_a_up[l]).astype(BF16),
            g_up=rwkv_g_up[l].astype(BF16),
            k_k=rwkv_k_k[l].reshape(1, D_RWKV),
            k_a=rwkv_k_a[l].reshape(1, D_RWKV),
            r_k=rwkv_r_k[l].reshape(1, D_RWKV),
        )

        z_r, z_p, q, kv = _in_projection(x_all, mod3, mod_base, norm_mix_g[l], w_in_b, rope, dims)
        r, v, av, g, bonus, kd, b, lw = _rwkv_prepare(z_r, rw, hs, dims)
        yd = _rwkv_scan(r, v, av, kd, b, lw, dims)
        y_p = _pool(z_p, pool_w[l], pool_scale[l], dims)
        y_a = _attention(q, kv, attn_sink[l], dims, windowed=True)
        if last:
            first_tile = nc // tm
            att_first = nc // tm
        else:
            y_a = jnp.concatenate([_attention(q, kv, attn_sink[l], dims, windowed=False), y_a], axis=0)
            first_tile = 0
            att_first = 0
        x_new, h2 = _out_projection(yd, bonus, g, y_p, y_a, x_all, w_out_b, rwkv_ln_g[l], rwkv_ln_b[l], hs,
                                    mod3, mod_base, norm_ffn_g[l], dims, first_tile, att_first)
        x_next = _ffn(h2, x_new, w_gu_b, w_down_b, mod3, mod_base, final_norm_g, dims,
                      first_tile * tm // tmf, final=last)
        if last:
            out = x_next.reshape(n_b, t_len, d)
        else:
            x_all = x_next
    return out
```

```python
import functools

import numpy as np
import jax
import jax.numpy as jnp
from jax import lax
from jax.experimental import pallas as pl
from jax.experimental.pallas import tpu as pltpu

F32 = jnp.float32
BF16 = jnp.bfloat16

HEAD_DIM = 64
RWKV_HEADS = 12
D_RWKV = RWKV_HEADS * HEAD_DIM
LORA_RANK = 64
GATE_RANK = 128
D_POOL = 512
POOL_WINDOWS = (2, 4, 8, 16)
POOL_GROUP_DIM = 128
POOL_HALO = 8
ATT_Q_HEADS = 12
ATT_KV_HEADS = 4
ATT_GROUP = ATT_Q_HEADS // ATT_KV_HEADS
D_ATT = ATT_Q_HEADS * HEAD_DIM
ATT_KV_DIM = ATT_KV_HEADS * HEAD_DIM
ATT_BLOCK = 128
GRID_W = 64
ROPE_BASE = 10000.0
NORM_EPS = 1e-6
RWKV_GN_EPS = 64e-5

OFF_K = D_RWKV
OFF_V = 2 * D_RWKV
OFF_WD = 3 * D_RWKV
OFF_AD = OFF_WD + 2 * LORA_RANK
OFF_GD = OFF_AD + 2 * LORA_RANK
RWKV_COLS = OFF_GD + GATE_RANK
OFF_POOL = RWKV_COLS
OFF_AQ = OFF_POOL + D_POOL
OFF_AK = OFF_AQ + D_ATT
OFF_AV = OFF_AK + ATT_KV_DIM
D_IN = OFF_AV + ATT_KV_DIM

LANES = 128
SUBLANES = 8
MXU_DIM = 256
HEADS_PER_GROUP = MXU_DIM // HEAD_DIM
N_GROUPS = D_RWKV // MXU_DIM
CHUNK = MXU_DIM // HEADS_PER_GROUP
INV_BASE = 8
N_INV_MASKS = 1 + (CHUNK // INV_BASE).bit_length() - 1
VMEM_LIMIT = 56 * 1024 * 1024


def _cparams(sem, vmem=VMEM_LIMIT):
    return pltpu.CompilerParams(dimension_semantics=sem, vmem_limit_bytes=vmem)


def _dot(a, b):
    return jnp.dot(a, b, preferred_element_type=F32)


def _dot_nt(a, b):
    return lax.dot_general(a, b, (((1,), (1,)), ((), ())), preferred_element_type=F32)


def _dot_tn(a, b):
    return lax.dot_general(a, b, (((0,), (0,)), ((), ())), preferred_element_type=F32)


def _split3(x):
    x1 = x.astype(BF16)
    r1 = x - x1.astype(F32)
    x2 = r1.astype(BF16)
    x3 = (r1 - x2.astype(F32)).astype(BF16)
    return x1, x2, x3


def _dot_exact_rhs(x, m_bf16):
    x1, x2, x3 = _split3(x)
    return _dot(x1, m_bf16) + _dot(x2, m_bf16) + _dot(x3, m_bf16)


def _dot_exact_lhs(m_bf16, x):
    x1, x2, x3 = _split3(x)
    return _dot(m_bf16, x1) + _dot(m_bf16, x2) + _dot(m_bf16, x3)


def _head_sum(x, hs):
    parts = [_dot_exact_rhs(x[:, MXU_DIM * g:MXU_DIM * (g + 1)], hs) for g in range(N_GROUPS)]
    return jnp.concatenate(parts, axis=1)


def _rms_mod(x, g, sc, sh):
    ms = jnp.mean(x * x, axis=-1, keepdims=True)
    return x * lax.rsqrt(ms + NORM_EPS) * g * (1.0 + sc) + sh


def _mod_kernel(c_ref, w_ref, b_ref, o_ref):
    c = c_ref[...]
    s = (c * jax.nn.sigmoid(c)).astype(BF16)
    o_ref[...] = _dot(s, w_ref[...].astype(BF16)) + b_ref[...]


def _modulation(cvec, ada_w, ada_b):
    n_layers, d, n6 = ada_w.shape
    tn = 1024
    return pl.pallas_call(
        _mod_kernel,
        grid=(n_layers, n6 // tn),
        in_specs=[
            pl.BlockSpec((SUBLANES, d), lambda l, j: (0, 0)),
            pl.BlockSpec((None, d, tn), lambda l, j: (l, 0, j)),
            pl.BlockSpec((None, 1, tn), lambda l, j: (l, 0, j)),
        ],
        out_specs=pl.BlockSpec((None, SUBLANES, tn), lambda l, j: (l, 0, j)),
        out_shape=jax.ShapeDtypeStruct((n_layers, SUBLANES, n6), F32),
        compiler_params=_cparams(("arbitrary", "arbitrary")),
        name="adaln_modulation",
    )(cvec, ada_w, ada_b.reshape(n_layers, 1, n6))


def _rope(z, cos, sin_a, sin_b):
    outs = []
    for j in range(z.shape[1] // LANES):
        zb = z[:, LANES * j:LANES * (j + 1)]
        outs.append(zb * cos + pltpu.roll(zb, LANES - 16, axis=1) * sin_a + pltpu.roll(zb, 16, axis=1) * sin_b)
    return jnp.concatenate(outs, axis=1)


def _inproj_kernel(x_ref, g_ref, sc_ref, sh_ref, w_ref, cos_ref, sa_ref, sb_ref,
                   zr_ref, zp_ref, q_ref, kv_ref):
    h = _rms_mod(x_ref[...], g_ref[...], sc_ref[...], sh_ref[...]).astype(BF16)
    zr_ref[...] = _dot(h, w_ref[:, 0:RWKV_COLS])
    zp_ref[...] = _dot(h, w_ref[:, OFF_POOL:OFF_AQ])
    cos, sa, sb = cos_ref[...], sa_ref[...], sb_ref[...]
    zq = _dot(h, w_ref[:, OFF_AQ:OFF_AK])
    q_ref[...] = (_rope(zq, cos, sa, sb) * (HEAD_DIM ** -0.5)).astype(BF16)
    zk = _dot(h, w_ref[:, OFF_AK:OFF_AV])
    zv = _dot(h, w_ref[:, OFF_AV:D_IN])
    kv_ref[...] = jnp.concatenate([_rope(zk, cos, sa, sb), zv], axis=1).astype(BF16)


def _in_projection(x_all, mod3, mod_base, norm_g, w_in_b, rope, dims):
    rows, d = x_all.shape
    tm = dims["tm"]
    grp = dims["grp"]
    cos, sin_a, sin_b = rope

    def mod_spec(k):
        return pl.BlockSpec((None, 1, d), lambda i: (mod_base + grp(i, tm) * 6 + k, 0, 0))

    row_spec = lambda w: pl.BlockSpec((tm, w), lambda i: (i, 0))
    return pl.pallas_call(
        _inproj_kernel,
        grid=(rows // tm,),
        in_specs=[
            row_spec(d),
            pl.BlockSpec((1, d), lambda i: (0, 0)),
            mod_spec(1), mod_spec(0),
            pl.BlockSpec((d, D_IN), lambda i: (0, 0), pipeline_mode=pl.Buffered(1)),
            row_spec(LANES), row_spec(LANES), row_spec(LANES),
        ],
        out_specs=[row_spec(RWKV_COLS), row_spec(D_POOL), row_spec(D_ATT), row_spec(2 * ATT_KV_DIM)],
        out_shape=[
            jax.ShapeDtypeStruct((rows, RWKV_COLS), F32),
            jax.ShapeDtypeStruct((rows, D_POOL), F32),
            jax.ShapeDtypeStruct((rows, D_ATT), BF16),
            jax.ShapeDtypeStruct((rows, 2 * ATT_KV_DIM), BF16),
        ],
        compiler_params=_cparams(("arbitrary",)),
        name="norm_inproj_rope",
    )(x_all, norm_g.reshape(1, d), mod3, mod3, w_in_b, cos, sin_a, sin_b)


def _seq_edges(i, tm, dims):
    nc, c_len, t_len = dims["nc"], dims["c_len"], dims["t_len"]
    r0 = i * tm
    is_ctx = r0 < nc
    seq_len = jnp.where(is_ctx, c_len, t_len)
    pos0 = jnp.where(is_ctx, lax.rem(r0, c_len), lax.rem(jnp.maximum(r0 - nc, 0), t_len))
    return pos0, seq_len, pos0 == 0, pos0 + tm == seq_len


def _prep_kernel(z_ref, zp_ref, zn_ref, mu_ref, w0_ref, wup_ref, a0_ref, aup_ref, gup_ref,
                 kk_ref, ka_ref, rk_ref, hs_ref,
                 r_ref, v_ref, av_ref, g_ref, bonus_ref, kd_ref, b_ref, lw_ref, *, tm, dims):
    i = pl.program_id(0)
    _, _, first, last = _seq_edges(i, tm, dims)
    z = z_ref[...]
    prev_row = jnp.where(first, 0.0, zp_ref[SUBLANES - 1:SUBLANES, :])
    next_row = jnp.where(last, 0.0, zn_ref[0:1, :])
    row = lax.broadcasted_iota(jnp.int32, (tm, 1), 0)
    z_m1 = jnp.where(row == 0, prev_row, pltpu.roll(z, 1, axis=0))
    z_p1 = jnp.where(row == tm - 1, next_row, pltpu.roll(z, tm - 1, axis=0))
    zs = z + mu_ref[...] * (0.5 * (z_m1 + z_p1) - z)

    r = zs[:, 0:OFF_K]
    k = zs[:, OFF_K:OFF_V]
    v = zs[:, OFF_V:OFF_WD]
    wd = zs[:, OFF_WD:OFF_AD]
    ad = zs[:, OFF_AD:OFF_GD]
    gd = zs[:, OFF_GD:RWKV_COLS]

    w_pre = w0_ref[...] + _dot(jnp.tanh(wd).astype(BF16), wup_ref[...])
    w_log = jnp.minimum(w_pre, 0.0) - jnp.log(1.0 + jnp.exp(-jnp.abs(w_pre))) - 0.5
    lw = -jnp.exp(w_log)
    a = jax.nn.sigmoid(a0_ref[...] + _dot(ad.astype(BF16), aup_ref[...]))
    g_ref[...] = _dot(jax.nn.sigmoid(gd).astype(BF16), gup_ref[...])

    hs = hs_ref[...]
    kx = k * kk_ref[...]
    kn = kx / jnp.maximum(jnp.sqrt(_head_sum(kx * kx, hs)), 1e-12)
    rk = r * rk_ref[...]
    ka = ka_ref[...]
    dot_rk = None
    for d in range(2):
        a_d = a[:, D_RWKV * d:D_RWKV * (d + 1)]
        kd = k * (1.0 + (a_d - 1.0) * ka)
        kd_ref[d] = kd
        b_ref[d] = kn * a_d
        lw_ref[d] = lw[:, D_RWKV * d:D_RWKV * (d + 1)]
        dot_rk = rk * kd if dot_rk is None else dot_rk + rk * kd
    r_ref[...] = r
    v_ref[...] = v
    av_ref[...] = -kn
    bonus_ref[...] = _head_sum(dot_rk, hs) * v


def _rwkv_prepare(z_r, p, hs, dims):
    rows = z_r.shape[0]
    tm = dims["tm"]
    nblk8 = rows // SUBLANES
    per = tm // SUBLANES
    full = lambda shape: pl.BlockSpec(shape, lambda i: (0,) * len(shape))
    row_spec = pl.BlockSpec((tm, D_RWKV), lambda i: (i, 0))
    dir_spec = pl.BlockSpec((2, tm, D_RWKV), lambda i: (0, i, 0))
    one = jax.ShapeDtypeStruct((rows, D_RWKV), F32)
    two = jax.ShapeDtypeStruct((2, rows, D_RWKV), F32)
    return pl.pallas_call(
        functools.partial(_prep_kernel, tm=tm, dims=dims),
        grid=(rows // tm,),
        in_specs=[
            pl.BlockSpec((tm, RWKV_COLS), lambda i: (i, 0)),
            pl.BlockSpec((SUBLANES, RWKV_COLS), lambda i: (jnp.maximum(i * per - 1, 0), 0)),
            pl.BlockSpec((SUBLANES, RWKV_COLS), lambda i: (jnp.minimum((i + 1) * per, nblk8 - 1), 0)),
            full((1, RWKV_COLS)),
            full((1, 2 * D_RWKV)), full((2 * LORA_RANK, 2 * D_RWKV)),
            full((1, 2 * D_RWKV)), full((2 * LORA_RANK, 2 * D_RWKV)),
            full((GATE_RANK, D_RWKV)),
            full((1, D_RWKV)), full((1, D_RWKV)), full((1, D_RWKV)),
            full((MXU_DIM, MXU_DIM)),
        ],
        out_specs=[row_spec, row_spec, row_spec, row_spec, row_spec, dir_spec, dir_spec, dir_spec],
        out_shape=[one, one, one, one, one, two, two, two],
        compiler_params=_cparams(("arbitrary",)),
        name="rwkv_prepare",
    )(z_r, z_r, z_r, p["mu"], p["w0"], p["w_up"], p["a0"], p["a_up"], p["g_up"],
      p["k_k"], p["k_a"], p["r_k"], hs)


def _scan_unit(r, v, av, kd, bd, e_c, e_cx, e_nc, e_rt, e_tot, masks, st_ref, d, g):
    m_strict, m_incl, bm, bm_b, eye, inv_masks = masks
    sl = slice(MXU_DIM * g, MXU_DIM * (g + 1))

    def expand(xb):
        return jnp.tile(xb, (HEADS_PER_GROUP, 1)) * bm_b

    def collapse(x):
        out = x[0:CHUNK]
        for h in range(1, HEADS_PER_GROUP):
            out = out + x[CHUNK * h:CHUNK * (h + 1)]
        return out

    r_t = (r[:, sl] * e_c[:, sl]).astype(BF16)
    a_t = (av[:, sl] * e_cx[:, sl]).astype(BF16)
    k_t = (kd[:, sl] * e_nc[:, sl]).astype(BF16)
    b_t = (bd[:, sl] * e_nc[:, sl]).astype(BF16)
    k_h = (kd[:, sl] * e_rt[:, sl]).astype(BF16)
    b_h = (bd[:, sl] * e_rt[:, sl]).astype(BF16)
    v_b = v[:, sl].astype(BF16)
    ea, er, eb, ek, ev = expand(a_t), expand(r_t), expand(b_t), expand(k_t), expand(v_b)
    yield

    g_ab = _dot_nt(ea, eb).astype(BF16) * m_strict
    yield
    g_ak = _dot_nt(ea, ek).astype(BF16) * m_strict
    yield
    g_rb = _dot_nt(er, eb).astype(BF16) * m_incl
    yield
    g_rk = _dot_nt(er, ek).astype(BF16) * m_incl
    yield

    d1 = g_ab * inv_masks[0]
    d2 = _dot(d1, d1).astype(BF16)
    yield
    base = eye + d1.astype(F32)
    base = base + _dot(base.astype(BF16), d2)
    yield
    d4 = _dot(d2, d2).astype(BF16)
    yield
    base = base + _dot(base.astype(BF16), d4)
    yield
    inv = base.astype(BF16)
    for lvl in range(1, N_INV_MASKS):
        off = g_ab * inv_masks[lvl]
        t_off = _dot(inv, off).astype(BF16)
        yield
        inv = inv + _dot(t_off, inv).astype(BF16)
        yield

    s0 = st_ref[d, g]
    s0_b = s0.astype(BF16)
    x1 = collapse(_dot(g_ak, ev))
    yield
    rhs = _dot_nt(a_t, s0_b) + x1
    y_loc = _dot_nt(r_t, s0_b) + collapse(_dot(g_rk, ev))
    yield
    e_u = _dot(inv, expand(rhs.astype(BF16)))
    yield
    y = y_loc + collapse(_dot(g_rb, e_u.astype(BF16)))
    yield
    u_c = collapse(e_u).astype(BF16)
    upd = _dot_tn(jnp.concatenate([u_c, v_b], axis=0), jnp.concatenate([b_h, k_h], axis=0))
    st_ref[d, g] = s0 * e_tot[:, sl] + upd * bm
    return y


def _run_interleaved(gens):
    results = [None] * len(gens)
    active = list(range(len(gens)))
    while active:
        for i in list(active):
            try:
                next(gens[i])
            except StopIteration as stop:
                results[i] = stop.value
                active.remove(i)
    return results


def _scan_kernel(rf_ref, vf_ref, af_ref, rb_ref, vb_ref, ab_ref, kdf_ref, bf_ref, lwf_ref, kdb_ref, bb_ref, lwb_ref,
                 tri_ref, ms_ref, mi_ref, bm_ref, eye_ref, im_ref, yf_ref, yb_ref, st_ref):
    @pl.when(pl.program_id(1) == 0)
    def _():
        st_ref[...] = jnp.zeros(st_ref.shape, F32)

    bm = bm_ref[...]
    bm_b = bm.astype(BF16)
    inv_masks = [im_ref[lvl] for lvl in range(N_INV_MASKS)]
    dirs = ((rf_ref, vf_ref, af_ref, kdf_ref, bf_ref, lwf_ref, yf_ref),
            (rb_ref, vb_ref, ab_ref, kdb_ref, bb_ref, lwb_ref, yb_ref))
    units = []
    for d, (r_ref, v_ref, av_ref, kd_ref, b_ref, lw_ref, _) in enumerate(dirs):
        lw = lw_ref[...]
        c = _dot_exact_lhs(tri_ref[d], lw)
        c_tot = jnp.sum(lw, axis=0, keepdims=True)
        e_c, e_cx, e_nc = jnp.exp(c), jnp.exp(c - lw), jnp.exp(-c)
        e_rt, e_tot = jnp.exp(c_tot - c), jnp.exp(c_tot)
        masks = (ms_ref[d], mi_ref[d], bm, bm_b, eye_ref[...], inv_masks)
        r, v, av, kd, bd = r_ref[...], v_ref[...], av_ref[...], kd_ref[...], b_ref[...]
        units += [_scan_unit(r, v, av, kd, bd, e_c, e_cx, e_nc, e_rt, e_tot, masks, st_ref, d, g)
                  for g in range(N_GROUPS)]
    ys = _run_interleaved(units)
    for d in range(2):
        dirs[d][-1][...] = jnp.concatenate(ys[N_GROUPS * d:N_GROUPS * (d + 1)], axis=1)


def _scan_constants():
    i = np.arange(MXU_DIM)
    blk = (i[:, None] // CHUNK) == (i[None, :] // CHUNK)
    t, s = i[:, None] % CHUNK, i[None, :] % CHUNK
    strict = np.stack([blk & (t > s), blk & (t < s)])
    incl = np.stack([blk & (t >= s), blk & (t <= s)])
    j = np.arange(CHUNK)
    tri = np.stack([j[:, None] >= j[None, :], j[:, None] <= j[None, :]])
    inv_masks = [(i[:, None] // INV_BASE) == (i[None, :] // INV_BASE)]
    n = INV_BASE
    while n < CHUNK:
        inv_masks.append(((i[:, None] // (2 * n)) == (i[None, :] // (2 * n))) & ((i[:, None] // n) != (i[None, :] // n)))
        n *= 2
    assert len(inv_masks) == N_INV_MASKS
    return (jnp.asarray(tri, BF16), jnp.asarray(strict, BF16), jnp.asarray(incl, BF16),
            jnp.asarray(blk, F32), jnp.asarray(np.eye(MXU_DIM), F32), jnp.asarray(np.stack(inv_masks), BF16))


def _rwkv_scan(r, v, av, kd, b, lw, dims):
    rows = r.shape[0]
    n_b, ncc, nct = dims["batch"], dims["c_len"] // CHUNK, dims["t_len"] // CHUNK
    consts = _scan_constants()

    def chunk(bi, s, d):
        j_ctx = s if d == 0 else ncc - 1 - s
        j_lat = s - ncc if d == 0 else nct - 1 - (s - ncc)
        return jnp.where(s < ncc, bi * ncc + j_ctx, n_b * ncc + bi * nct + j_lat)

    shared = lambda d: pl.BlockSpec((CHUNK, D_RWKV), lambda bi, s: (chunk(bi, s, d), 0))
    per_dir = lambda d: pl.BlockSpec((None, CHUNK, D_RWKV), lambda bi, s: (d, chunk(bi, s, d), 0))
    whole = lambda a: pl.BlockSpec(a.shape, lambda bi, s: (0,) * a.ndim)
    y_shape = jax.ShapeDtypeStruct((rows, D_RWKV), F32)
    return pl.pallas_call(
        _scan_kernel,
        grid=(n_b, ncc + nct),
        in_specs=[shared(0)] * 3 + [shared(1)] * 3 + [per_dir(0)] * 3 + [per_dir(1)] * 3 + [whole(a) for a in consts],
        out_specs=[shared(0), shared(1)],
        out_shape=[y_shape, y_shape],
        scratch_shapes=[pltpu.VMEM((2, N_GROUPS, MXU_DIM, MXU_DIM), F32)],
        compiler_params=_cparams(("arbitrary", "arbitrary")),
        name="rwkv_chunk_scan",
    )(r, v, av, r, v, av, kd, b, lw, kd, b, lw, *consts)


def _pool_kernel(z_ref, zp_ref, zn_ref, w_ref, sc_ref, o_ref, *, tm, dims):
    i = pl.program_id(0)
    pos0, seq_len, first, last = _seq_edges(i, tm, dims)
    prev = jnp.where(first, 0.0, zp_ref[...])
    nxt = jnp.where(last, 0.0, zn_ref[...])
    ext = jnp.concatenate([prev, z_ref[...], nxt], axis=0)
    n = tm + 2 * POOL_HALO
    t = pos0 + lax.broadcasted_iota(jnp.int32, (tm, 1), 0)
    outs = []
    for gi, w in enumerate(POOL_WINDOWS):
        e = ext[:, POOL_GROUP_DIM * gi:POOL_GROUP_DIM * (gi + 1)]
        acc, width = e, 1
        while width < w:
            acc = acc + pltpu.roll(acc, n - width, axis=0)
            width *= 2
        win = pltpu.roll(acc, w // 2, axis=0)[POOL_HALO:POOL_HALO + tm]
        lo = jnp.clip(t - w // 2, 0, seq_len)
        hi = jnp.clip(t - w // 2 + w, 0, seq_len)
        y = win / (hi - lo).astype(F32) - e[POOL_HALO:POOL_HALO + tm]
        outs.append(_dot(y.astype(BF16), w_ref[gi].astype(BF16)))
    o_ref[...] = (jnp.concatenate(outs, axis=1) * sc_ref[...]).astype(BF16)


def _pool(z_p, pool_w, pool_scale, dims):
    rows = z_p.shape[0]
    tm = dims["tm"]
    nblk8 = rows // SUBLANES
    per = tm // SUBLANES
    return pl.pallas_call(
        functools.partial(_pool_kernel, tm=tm, dims=dims),
        grid=(rows // tm,),
        in_specs=[
            pl.BlockSpec((tm, D_POOL), lambda i: (i, 0)),
            pl.BlockSpec((SUBLANES, D_POOL), lambda i: (jnp.maximum(i * per - 1, 0), 0)),
            pl.BlockSpec((SUBLANES, D_POOL), lambda i: (jnp.minimum((i + 1) * per, nblk8 - 1), 0)),
            pl.BlockSpec(pool_w.shape, lambda i: (0, 0, 0)),
            pl.BlockSpec((1, D_POOL), lambda i: (0, 0)),
        ],
        out_specs=pl.BlockSpec((tm, D_POOL), lambda i: (i, 0)),
        out_shape=jax.ShapeDtypeStruct((rows, D_POOL), BF16),
        compiler_params=_cparams(("arbitrary",)),
        name="multiscale_pool",
    )(z_p, z_p, z_p, pool_w, pool_scale.reshape(1, D_POOL))


def _attn_kernel(sink_ref, q_ref, *rest, windowed, nblk):
    if windowed:
        kp_ref, kc_ref, kn_ref, kx_ref, o_ref = rest
        kv = jnp.concatenate([kp_ref[...], kc_ref[...], kn_ref[...], kx_ref[...]], axis=0)
    else:
        kx_ref, o_ref = rest
        kv = kx_ref[...]
    n = pl.program_id(1)
    q = q_ref[...]
    k_all, v_all = kv[:, 0:ATT_KV_DIM], kv[:, ATT_KV_DIM:2 * ATT_KV_DIM]
    nk = kv.shape[0]
    rows = ATT_GROUP * ATT_BLOCK
    if windowed:
        qi = lax.broadcasted_iota(jnp.int32, (ATT_BLOCK, nk), 0)
        kj = lax.broadcasted_iota(jnp.int32, (ATT_BLOCK, nk), 1)
        rel = kj - ATT_BLOCK - qi
        k_lo = jnp.where(n == 0, ATT_BLOCK, 0)
        k_hi = jnp.where(n == nblk - 1, 2 * ATT_BLOCK, 3 * ATT_BLOCK)
        bad = jnp.where(rel > ATT_BLOCK, 1, 0) + jnp.where(rel < -ATT_BLOCK, 1, 0) \
            + jnp.where(kj < k_lo, 1, 0) + jnp.where(kj >= k_hi, 1, 0)
        bad = jnp.where(kj >= 3 * ATT_BLOCK, 0, bad)
        neg = jnp.tile(jnp.where(bad > 0, -1e30, 0.0).astype(F32), (ATT_GROUP, 1))
    lane = lax.shift_right_logical(lax.broadcasted_iota(jnp.int32, (1, ATT_KV_DIM), 1), HEAD_DIM.bit_length() - 1)
    grow = lax.shift_right_logical(lax.broadcasted_iota(jnp.int32, (rows, 1), 0), ATT_BLOCK.bit_length() - 1)
    outs = [jnp.zeros((ATT_BLOCK, ATT_KV_DIM), F32) for _ in range(ATT_GROUP)]
    for hk in range(ATT_KV_HEADS):
        lm_f = (lane == hk).astype(F32)
        lm_b = lm_f.astype(BF16)
        lhs = jnp.concatenate([q[:, ATT_KV_DIM * g:ATT_KV_DIM * (g + 1)] * lm_b for g in range(ATT_GROUP)], axis=0)
        s = _dot_nt(lhs, k_all)
        if windowed:
            s = s + neg
        sk = jnp.where(grow == 0, sink_ref[ATT_GROUP * hk],
                       jnp.where(grow == 1, sink_ref[ATT_GROUP * hk + 1], sink_ref[ATT_GROUP * hk + 2]))
        m = jnp.maximum(jnp.max(s, axis=1, keepdims=True), sk)
        p = jnp.exp(s - m)
        den = jnp.sum(p, axis=1, keepdims=True) + jnp.exp(sk - m)
        pv = _dot(p.astype(BF16), v_all) / den
        for g in range(ATT_GROUP):
            outs[g] = outs[g] + pv[ATT_BLOCK * g:ATT_BLOCK * (g + 1)] * lm_f
    o_ref[...] = jnp.concatenate(outs, axis=1).astype(BF16)


def _attention(q, kv, sink, dims, windowed):
    rows = q.shape[0]
    n_b, c_len, t_len, nc = dims["batch"], dims["c_len"], dims["t_len"], dims["nc"]
    seq = t_len if windowed else c_len
    nblk = seq // ATT_BLOCK
    base = (nc // ATT_BLOCK) if windowed else 0
    qrow = lambda bi, n: (base + bi * nblk + n, 0)
    ctx_spec = pl.BlockSpec((c_len, 2 * ATT_KV_DIM), lambda bi, n: (bi, 0))
    blk = lambda f: pl.BlockSpec((ATT_BLOCK, 2 * ATT_KV_DIM), f)
    in_specs = [pl.BlockSpec(memory_space=pltpu.SMEM), pl.BlockSpec((ATT_BLOCK, D_ATT), qrow)]
    args = [sink, q]
    if windowed:
        in_specs += [
            blk(lambda bi, n: (base + bi * nblk + jnp.maximum(n - 1, 0), 0)),
            blk(qrow),
            blk(lambda bi, n: (base + bi * nblk + jnp.minimum(n + 1, nblk - 1), 0)),
        ]
        args += [kv, kv, kv]
    in_specs.append(ctx_spec)
    args.append(kv)
    out_rows = n_b * seq
    return pl.pallas_call(
        functools.partial(_attn_kernel, windowed=windowed, nblk=nblk),
        grid=(n_b, nblk),
        in_specs=in_specs,
        out_specs=pl.BlockSpec((ATT_BLOCK, D_ATT), lambda bi, n: (bi * nblk + n, 0)),
        out_shape=jax.ShapeDtypeStruct((out_rows, D_ATT), BF16),
        compiler_params=_cparams(("arbitrary", "arbitrary")),
        name="window_attention" if windowed else "context_attention",
    )(*args)


def _outproj_kernel(yf_ref, yb_ref, bonus_ref, g_ref, yp_ref, ya_ref, x_ref, w_ref, lng_ref, lnb_ref, hs_ref,
                    gt_ref, ng_ref, sc_ref, sh_ref, xo_ref, ho_ref):
    hs = hs_ref[...]
    y = yf_ref[...] + yb_ref[...]
    inv_n = 1.0 / HEAD_DIM
    dev = y - _head_sum(y, hs) * inv_n
    var = _head_sum(dev * dev, hs) * inv_n
    yn = dev * lax.rsqrt(var + RWKV_GN_EPS) * lng_ref[...] + lnb_ref[...]
    y_r = ((yn + bonus_ref[...]) * g_ref[...]).astype(BF16)
    acc = (_dot(y_r, w_ref[0:D_RWKV])
           + _dot(yp_ref[...], w_ref[D_RWKV:D_RWKV + D_POOL])
           + _dot(ya_ref[...], w_ref[D_RWKV + D_POOL:D_RWKV + D_POOL + D_ATT]))
    x_new = x_ref[...] + gt_ref[...] * acc
    xo_ref[...] = x_new
    ho_ref[...] = _rms_mod(x_new, ng_ref[...], sc_ref[...], sh_ref[...]).astype(BF16)


def _out_projection(yf, yb, bonus, g, y_p, y_a, x_all, w_out_b, ln_g, ln_b, hs, mod3, mod_base, norm_g,
                    dims, first_tile, att_first_tile):
    rows, d = x_all.shape
    tm = dims["tm"]
    grp = dims["grp"]
    n_tiles = rows // tm - first_tile
    off = lambda w: pl.BlockSpec((tm, w), lambda i: (i + first_tile, 0))

    def mod_spec(k):
        return pl.BlockSpec((None, 1, d), lambda i: (mod_base + grp(i + first_tile, tm) * 6 + k, 0, 0))

    vec = lambda w: pl.BlockSpec((1, w), lambda i: (0, 0))
    out_rows = n_tiles * tm
    return pl.pallas_call(
        _outproj_kernel,
        grid=(n_tiles,),
        in_specs=[
            off(D_RWKV), off(D_RWKV), off(D_RWKV), off(D_RWKV), off(D_POOL),
            pl.BlockSpec((tm, D_ATT), lambda i: (i + first_tile - att_first_tile, 0)),
            off(d),
            pl.BlockSpec((d, d), lambda i: (0, 0), pipeline_mode=pl.Buffered(1)),
            vec(D_RWKV), vec(D_RWKV),
            pl.BlockSpec((MXU_DIM, MXU_DIM), lambda i: (0, 0)),
            mod_spec(2), vec(d), mod_spec(4), mod_spec(3),
        ],
        out_specs=[pl.BlockSpec((tm, d), lambda i: (i, 0)), pl.BlockSpec((tm, d), lambda i: (i, 0))],
        out_shape=[jax.ShapeDtypeStruct((out_rows, d), F32), jax.ShapeDtypeStruct((out_rows, d), BF16)],
        compiler_params=_cparams(("arbitrary",)),
        name="readout_outproj_norm",
    )(yf, yb, bonus, g, y_p, y_a, x_all, w_out_b, ln_g.reshape(1, -1), ln_b.reshape(1, -1), hs,
      mod3, norm_g.reshape(1, d), mod3, mod3)


def _ffn_kernel(h_ref, x_ref, wg_ref, wu_ref, wd_ref, gt_ref, fg_ref, o_ref, acc_ref, *, final):
    j = pl.program_id(1)
    h = h_ref[...]
    gate = _dot(h, wg_ref[...])
    up = _dot(h, wu_ref[...])
    act = (gate * jax.nn.sigmoid(gate) * up).astype(BF16)
    part = _dot(act, wd_ref[...])

    @pl.when(j == 0)
    def _():
        acc_ref[...] = part

    @pl.when(j > 0)
    def _():
        acc_ref[...] += part

    @pl.when(j == pl.num_programs(1) - 1)
    def _():
        x_new = x_ref[...] + gt_ref[...] * acc_ref[...]
        if final:
            ms = jnp.mean(x_new * x_new, axis=-1, keepdims=True)
            x_new = x_new * lax.rsqrt(ms + NORM_EPS) * fg_ref[...]
        o_ref[...] = x_new


def _ffn(h2, x_new, w_gu_b, w_down_b, mod3, mod_base, final_g, dims, first_tile, final):
    rows, d = x_new.shape
    tmf, th = dims["tmf"], dims["th"]
    hidden = w_down_b.shape[0]
    nh = hidden // th
    grp = dims["grp"]
    return pl.pallas_call(
        functools.partial(_ffn_kernel, final=final),
        grid=(rows // tmf, nh),
        in_specs=[
            pl.BlockSpec((tmf, d), lambda i, j: (i, 0)),
            pl.BlockSpec((tmf, d), lambda i, j: (i, 0)),
            pl.BlockSpec((d, th), lambda i, j: (0, j)),
            pl.BlockSpec((d, th), lambda i, j: (0, nh + j)),
            pl.BlockSpec((th, d), lambda i, j: (j, 0)),
            pl.BlockSpec((None, 1, d), lambda i, j: (mod_base + grp(i + first_tile, tmf) * 6 + 5, 0, 0)),
            pl.BlockSpec((1, d), lambda i, j: (0, 0)),
        ],
        out_specs=pl.BlockSpec((tmf, d), lambda i, j: (i, 0)),
        out_shape=jax.ShapeDtypeStruct((rows, d), F32),
        scratch_shapes=[pltpu.VMEM((tmf, d), F32)],
        compiler_params=_cparams(("arbitrary", "arbitrary")),
        name="swiglu_ffn_final" if final else "swiglu_ffn",
    )(h2, x_new, w_gu_b, w_gu_b, w_down_b, mod3, final_g.reshape(1, d))


def _rope_tables(dims):
    t_len, nc, n_b = dims["t_len"], dims["nc"], dims["batch"]
    grid_rows = t_len // GRID_W
    row = jnp.repeat(jnp.arange(grid_rows), GRID_W).astype(F32)
    col = jnp.tile(jnp.arange(GRID_W), grid_rows).astype(F32)
    n_freq = HEAD_DIM // 4
    inv = ROPE_BASE ** (-jnp.arange(n_freq, dtype=F32) / n_freq)
    ang_r = row[:, None] * inv
    ang_c = col[:, None] * inv
    ang = jnp.concatenate([ang_r, ang_r, ang_c, ang_c], axis=-1)
    ang = jnp.tile(ang, (1, LANES // HEAD_DIM))
    cos, sin = jnp.cos(ang), jnp.sin(ang)
    first_half = (jnp.arange(LANES) % 32) < 16
    sin_a = jnp.where(first_half, -sin, 0.0)
    sin_b = jnp.where(first_half, 0.0, sin)
    flat = lambda tab, fill: jnp.concatenate([jnp.full((nc, LANES), fill, F32), jnp.tile(tab, (n_b, 1))], axis=0)
    return flat(cos, 1.0), flat(sin_a, 0.0), flat(sin_b, 0.0)


def _block_diag2(w):
    z = jnp.zeros_like(w[0])
    return jnp.concatenate([jnp.concatenate([w[0], z], axis=1), jnp.concatenate([z, w[1]], axis=1)], axis=0)


def kernel(x, c, ctx, c_ctx, ada_w, ada_b, norm_mix_g, norm_ffn_g, w_in, rwkv_mu, rwkv_w0, rwkv_w_up, rwkv_a0,
           rwkv_a_up, rwkv_g_up, rwkv_k_k, rwkv_k_a, rwkv_r_k, rwkv_ln_g, rwkv_ln_b, pool_w, pool_scale,
           attn_sink, w_out, ffn_w_gu, ffn_w_down, final_norm_g):
    n_b, t_len, d = x.shape
    c_len = ctx.shape[1]
    depth = w_in.shape[0]
    nc = n_b * c_len
    rows = nc + n_b * t_len
    tm = 256 if c_len % 256 == 0 else c_len
    tmf = nc
    assert n_b + 1 <= SUBLANES and c_len % tm == 0 and t_len % tm == 0 and t_len % tmf == 0
    assert tm % ATT_BLOCK == 0 and c_len % CHUNK == 0 and t_len % GRID_W == 0 and d == w_out.shape[1]

    def grp(i, tile):
        r0 = i * tile
        return jnp.where(r0 < nc, 0, 1 + jnp.maximum(r0 - nc, 0) // t_len)

    hidden = ffn_w_down.shape[1]
    th = 512 if hidden % 512 == 0 else hidden
    dims = dict(batch=n_b, c_len=c_len, t_len=t_len, nc=nc, tm=tm, tmf=tmf, th=th, grp=grp)

    cvec = jnp.concatenate([c_ctx[None, :], c, jnp.zeros((SUBLANES - 1 - n_b, d), F32)], axis=0)
    mod = _modulation(cvec, ada_w, ada_b)
    mod3 = mod.reshape(depth * SUBLANES * 6, 1, d)

    x_all = jnp.concatenate([ctx.reshape(nc, d), x.reshape(n_b * t_len, d)], axis=0)
    rope = _rope_tables(dims)
    hs = jnp.asarray(np.kron(np.eye(HEADS_PER_GROUP), np.ones((HEAD_DIM, HEAD_DIM))), BF16)

    out = None
    for l in range(depth):
        last = l == depth - 1
        mod_base = l * SUBLANES * 6
        wq = w_in[l][:, OFF_AQ:OFF_AK].reshape(d, ATT_KV_HEADS, ATT_GROUP, HEAD_DIM)
        wq = wq.transpose(0, 2, 1, 3).reshape(d, D_ATT)
        w_in_b = jnp.concatenate([w_in[l][:, :OFF_AQ], wq, w_in[l][:, OFF_AK:]], axis=1).astype(BF16)
        wo_att = w_out[l][D_RWKV + D_POOL:].reshape(ATT_KV_HEADS, ATT_GROUP, HEAD_DIM, d)
        wo_att = wo_att.transpose(1, 0, 2, 3).reshape(D_ATT, d)
        w_out_b = jnp.concatenate([w_out[l][:D_RWKV + D_POOL], wo_att], axis=0).astype(BF16)
        w_gu_b = ffn_w_gu[l].astype(BF16)
        w_down_b = ffn_w_down[l].astype(BF16)
        rw = dict(
            mu=rwkv_mu[l].reshape(1, RWKV_COLS),
            w0=rwkv_w0[l].reshape(1, 2 * D_RWKV),
            w_up=_block_diag2(rwkv_w_up[l]).astype(BF16),
            a0=rwkv_a0[l].reshape(1, 2 * D_RWKV),
            a_up=_block_diag2(rwkv_a_up[l]).astype(BF16),
            g_up=rwkv_g_up[l].astype(BF16),
            k_k=rwkv_k_k[l].reshape(1, D_RWKV),
            k_a=rwkv_k_a[l].reshape(1, D_RWKV),
            r_k=rwkv_r_k[l].reshape(1, D_RWKV),
        )

        z_r, z_p, q, kv = _in_projection(x_all, mod3, mod_base, norm_mix_g[l], w_in_b, rope, dims)
        r, v, av, g, bonus, kd, b, lw = _rwkv_prepare(z_r, rw, hs, dims)
        yf, yb = _rwkv_scan(r, v, av, kd, b, lw, dims)
        y_p = _pool(z_p, pool_w[l], pool_scale[l], dims)
        y_a = _attention(q, kv, attn_sink[l], dims, windowed=True)
        if last:
            first_tile = nc // tm
            att_first = nc // tm
        else:
            y_a = jnp.concatenate([_attention(q, kv, attn_sink[l], dims, windowed=False), y_a], axis=0)
            first_tile = 0
            att_first = 0
        x_new, h2 = _out_projection(yf, yb, bonus, g, y_p, y_a, x_all, w_out_b, rwkv_ln_g[l], rwkv_ln_b[l], hs,
                                    mod3, mod_base, norm_ffn_g[l], dims, first_tile, att_first)
        x_next = _ffn(h2, x_new, w_gu_b, w_down_b, mod3, mod_base, final_norm_g, dims,
                      first_tile * tm // tmf, final=last)
        if last:
            out = x_next.reshape(n_b, t_len, d)
        else:
            x_all = x_next
    return out
```

```python
import functools

import numpy as np
import jax
import jax.numpy as jnp
from jax import lax
from jax.experimental import pallas as pl
from jax.experimental.pallas import tpu as pltpu

F32 = jnp.float32
BF16 = jnp.bfloat16

HEAD_DIM = 64
RWKV_HEADS = 12
D_RWKV = RWKV_HEADS * HEAD_DIM
LORA_RANK = 64
GATE_RANK = 128
D_POOL = 512
POOL_WINDOWS = (2, 4, 8, 16)
POOL_GROUP_DIM = 128
POOL_HALO = 8
ATT_Q_HEADS = 12
ATT_KV_HEADS = 4
ATT_GROUP = ATT_Q_HEADS // ATT_KV_HEADS
D_ATT = ATT_Q_HEADS * HEAD_DIM
ATT_KV_DIM = ATT_KV_HEADS * HEAD_DIM
ATT_BLOCK = 128
GRID_W = 64
ROPE_BASE = 10000.0
NORM_EPS = 1e-6
RWKV_GN_EPS = 64e-5

OFF_K = D_RWKV
OFF_V = 2 * D_RWKV
OFF_WD = 3 * D_RWKV
OFF_AD = OFF_WD + 2 * LORA_RANK
OFF_GD = OFF_AD + 2 * LORA_RANK
RWKV_COLS = OFF_GD + GATE_RANK
OFF_POOL = RWKV_COLS
OFF_AQ = OFF_POOL + D_POOL
OFF_AK = OFF_AQ + D_ATT
OFF_AV = OFF_AK + ATT_KV_DIM
D_IN = OFF_AV + ATT_KV_DIM

LANES = 128
SUBLANES = 8
MXU_DIM = 256
HEADS_PER_GROUP = MXU_DIM // HEAD_DIM
N_GROUPS = D_RWKV // MXU_DIM
CHUNK = MXU_DIM // HEADS_PER_GROUP
INV_BASE = 8
N_INV_MASKS = 1 + (CHUNK // INV_BASE).bit_length() - 1
VMEM_LIMIT = 56 * 1024 * 1024


def _cparams(sem, vmem=VMEM_LIMIT):
    return pltpu.CompilerParams(dimension_semantics=sem, vmem_limit_bytes=vmem)


def _dot(a, b):
    return jnp.dot(a, b, preferred_element_type=F32)


def _dot_nt(a, b):
    return lax.dot_general(a, b, (((1,), (1,)), ((), ())), preferred_element_type=F32)


def _dot_tn(a, b):
    return lax.dot_general(a, b, (((0,), (0,)), ((), ())), preferred_element_type=F32)


def _split3(x):
    x1 = x.astype(BF16)
    r1 = x - x1.astype(F32)
    x2 = r1.astype(BF16)
    x3 = (r1 - x2.astype(F32)).astype(BF16)
    return x1, x2, x3


def _dot_exact_rhs(x, m_bf16):
    x1, x2, x3 = _split3(x)
    return _dot(x1, m_bf16) + _dot(x2, m_bf16) + _dot(x3, m_bf16)


def _dot_exact_lhs(m_bf16, x):
    x1, x2, x3 = _split3(x)
    return _dot(m_bf16, x1) + _dot(m_bf16, x2) + _dot(m_bf16, x3)


def _head_sum(x, hs):
    parts = [_dot_exact_rhs(x[:, MXU_DIM * g:MXU_DIM * (g + 1)], hs) for g in range(N_GROUPS)]
    return jnp.concatenate(parts, axis=1)


def _rms_mod(x, g, sc, sh):
    ms = jnp.mean(x * x, axis=-1, keepdims=True)
    return x * lax.rsqrt(ms + NORM_EPS) * g * (1.0 + sc) + sh


def _mod_kernel(c_ref, w_ref, b_ref, o_ref):
    c = c_ref[...]
    s = (c * jax.nn.sigmoid(c)).astype(BF16)
    o_ref[...] = _dot(s, w_ref[...].astype(BF16)) + b_ref[...]


def _modulation(cvec, ada_w, ada_b):
    n_layers, d, n6 = ada_w.shape
    tn = 1024
    return pl.pallas_call(
        _mod_kernel,
        grid=(n_layers, n6 // tn),
        in_specs=[
            pl.BlockSpec((SUBLANES, d), lambda l, j: (0, 0)),
            pl.BlockSpec((None, d, tn), lambda l, j: (l, 0, j)),
            pl.BlockSpec((None, 1, tn), lambda l, j: (l, 0, j)),
        ],
        out_specs=pl.BlockSpec((None, SUBLANES, tn), lambda l, j: (l, 0, j)),
        out_shape=jax.ShapeDtypeStruct((n_layers, SUBLANES, n6), F32),
        compiler_params=_cparams(("arbitrary", "arbitrary")),
        name="adaln_modulation",
    )(cvec, ada_w, ada_b.reshape(n_layers, 1, n6))


def _rope(z, cos, sin_a, sin_b):
    outs = []
    for j in range(z.shape[1] // LANES):
        zb = z[:, LANES * j:LANES * (j + 1)]
        outs.append(zb * cos + pltpu.roll(zb, LANES - 16, axis=1) * sin_a + pltpu.roll(zb, 16, axis=1) * sin_b)
    return jnp.concatenate(outs, axis=1)


def _inproj_kernel(x_ref, g_ref, sc_ref, sh_ref, w_ref, cos_ref, sa_ref, sb_ref,
                   zr_ref, zp_ref, q_ref, kv_ref):
    h = _rms_mod(x_ref[...], g_ref[...], sc_ref[...], sh_ref[...]).astype(BF16)
    zr_ref[...] = _dot(h, w_ref[:, 0:RWKV_COLS])
    zp_ref[...] = _dot(h, w_ref[:, OFF_POOL:OFF_AQ])
    cos, sa, sb = cos_ref[...], sa_ref[...], sb_ref[...]
    zq = _dot(h, w_ref[:, OFF_AQ:OFF_AK])
    q_ref[...] = (_rope(zq, cos, sa, sb) * (HEAD_DIM ** -0.5)).astype(BF16)
    zk = _dot(h, w_ref[:, OFF_AK:OFF_AV])
    zv = _dot(h, w_ref[:, OFF_AV:D_IN])
    kv_ref[...] = jnp.concatenate([_rope(zk, cos, sa, sb), zv], axis=1).astype(BF16)


def _in_projection(x_all, mod3, mod_base, norm_g, w_in_b, rope, dims):
    rows, d = x_all.shape
    tm = dims["tm"]
    grp = dims["grp"]
    cos, sin_a, sin_b = rope

    def mod_spec(k):
        return pl.BlockSpec((None, 1, d), lambda i: (mod_base + grp(i, tm) * 6 + k, 0, 0))

    row_spec = lambda w: pl.BlockSpec((tm, w), lambda i: (i, 0))
    return pl.pallas_call(
        _inproj_kernel,
        grid=(rows // tm,),
        in_specs=[
            row_spec(d),
            pl.BlockSpec((1, d), lambda i: (0, 0)),
            mod_spec(1), mod_spec(0),
            pl.BlockSpec((d, D_IN), lambda i: (0, 0), pipeline_mode=pl.Buffered(1)),
            row_spec(LANES), row_spec(LANES), row_spec(LANES),
        ],
        out_specs=[row_spec(RWKV_COLS), row_spec(D_POOL), row_spec(D_ATT), row_spec(2 * ATT_KV_DIM)],
        out_shape=[
            jax.ShapeDtypeStruct((rows, RWKV_COLS), F32),
            jax.ShapeDtypeStruct((rows, D_POOL), F32),
            jax.ShapeDtypeStruct((rows, D_ATT), BF16),
            jax.ShapeDtypeStruct((rows, 2 * ATT_KV_DIM), BF16),
        ],
        compiler_params=_cparams(("arbitrary",)),
        name="norm_inproj_rope",
    )(x_all, norm_g.reshape(1, d), mod3, mod3, w_in_b, cos, sin_a, sin_b)


def _seq_edges(i, tm, dims):
    nc, c_len, t_len = dims["nc"], dims["c_len"], dims["t_len"]
    r0 = i * tm
    is_ctx = r0 < nc
    seq_len = jnp.where(is_ctx, c_len, t_len)
    pos0 = jnp.where(is_ctx, lax.rem(r0, c_len), lax.rem(jnp.maximum(r0 - nc, 0), t_len))
    return pos0, seq_len, pos0 == 0, pos0 + tm == seq_len


def _prep_kernel(z_ref, zp_ref, zn_ref, mu_ref, w0_ref, wup_ref, a0_ref, aup_ref, gup_ref,
                 kk_ref, ka_ref, rk_ref, hs_ref,
                 r_ref, v_ref, av_ref, g_ref, bonus_ref, kd_ref, b_ref, lw_ref, *, tm, dims):
    i = pl.program_id(0)
    _, _, first, last = _seq_edges(i, tm, dims)
    z = z_ref[...]
    prev_row = jnp.where(first, 0.0, zp_ref[SUBLANES - 1:SUBLANES, :])
    next_row = jnp.where(last, 0.0, zn_ref[0:1, :])
    row = lax.broadcasted_iota(jnp.int32, (tm, 1), 0)
    z_m1 = jnp.where(row == 0, prev_row, pltpu.roll(z, 1, axis=0))
    z_p1 = jnp.where(row == tm - 1, next_row, pltpu.roll(z, tm - 1, axis=0))
    zs = z + mu_ref[...] * (0.5 * (z_m1 + z_p1) - z)

    r = zs[:, 0:OFF_K]
    k = zs[:, OFF_K:OFF_V]
    v = zs[:, OFF_V:OFF_WD]
    wd = zs[:, OFF_WD:OFF_AD]
    ad = zs[:, OFF_AD:OFF_GD]
    gd = zs[:, OFF_GD:RWKV_COLS]

    w_pre = w0_ref[...] + _dot(jnp.tanh(wd).astype(BF16), wup_ref[...])
    w_log = jnp.minimum(w_pre, 0.0) - jnp.log(1.0 + jnp.exp(-jnp.abs(w_pre))) - 0.5
    lw = -jnp.exp(w_log)
    a = jax.nn.sigmoid(a0_ref[...] + _dot(ad.astype(BF16), aup_ref[...]))
    g_ref[...] = _dot(jax.nn.sigmoid(gd).astype(BF16), gup_ref[...])

    hs = hs_ref[...]
    kx = k * kk_ref[...]
    kn = kx / jnp.maximum(jnp.sqrt(_head_sum(kx * kx, hs)), 1e-12)
    rk = r * rk_ref[...]
    ka = ka_ref[...]
    dot_rk = None
    for d in range(2):
        a_d = a[:, D_RWKV * d:D_RWKV * (d + 1)]
        kd = k * (1.0 + (a_d - 1.0) * ka)
        kd_ref[d] = kd
        b_ref[d] = kn * a_d
        lw_ref[d] = lw[:, D_RWKV * d:D_RWKV * (d + 1)]
        dot_rk = rk * kd if dot_rk is None else dot_rk + rk * kd
    r_ref[...] = r
    v_ref[...] = v
    av_ref[...] = -kn
    bonus_ref[...] = _head_sum(dot_rk, hs) * v


def _rwkv_prepare(z_r, p, hs, dims):
    rows = z_r.shape[0]
    tm = dims["tm"]
    nblk8 = rows // SUBLANES
    per = tm // SUBLANES
    full = lambda shape: pl.BlockSpec(shape, lambda i: (0,) * len(shape))
    row_spec = pl.BlockSpec((tm, D_RWKV), lambda i: (i, 0))
    dir_spec = pl.BlockSpec((2, tm, D_RWKV), lambda i: (0, i, 0))
    one = jax.ShapeDtypeStruct((rows, D_RWKV), F32)
    two = jax.ShapeDtypeStruct((2, rows, D_RWKV), F32)
    return pl.pallas_call(
        functools.partial(_prep_kernel, tm=tm, dims=dims),
        grid=(rows // tm,),
        in_specs=[
            pl.BlockSpec((tm, RWKV_COLS), lambda i: (i, 0)),
            pl.BlockSpec((SUBLANES, RWKV_COLS), lambda i: (jnp.maximum(i * per - 1, 0), 0)),
            pl.BlockSpec((SUBLANES, RWKV_COLS), lambda i: (jnp.minimum((i + 1) * per, nblk8 - 1), 0)),
            full((1, RWKV_COLS)),
            full((1, 2 * D_RWKV)), full((2 * LORA_RANK, 2 * D_RWKV)),
            full((1, 2 * D_RWKV)), full((2 * LORA_RANK, 2 * D_RWKV)),
            full((GATE_RANK, D_RWKV)),
            full((1, D_RWKV)), full((1, D_RWKV)), full((1, D_RWKV)),
            full((MXU_DIM, MXU_DIM)),
        ],
        out_specs=[row_spec, row_spec, row_spec, row_spec, row_spec, dir_spec, dir_spec, dir_spec],
        out_shape=[one, one, one, one, one, two, two, two],
        compiler_params=_cparams(("arbitrary",)),
        name="rwkv_prepare",
    )(z_r, z_r, z_r, p["mu"], p["w0"], p["w_up"], p["a0"], p["a_up"], p["g_up"],
      p["k_k"], p["k_a"], p["r_k"], hs)


def _take_rows(x, blk, phase):
    return jnp.concatenate([x[s:s + blk] for s in range(phase * blk, x.shape[0], 2 * blk)], axis=0)


def _add_rows(full, part, blk, phase):
    pieces = []
    for i, s in enumerate(range(0, full.shape[0], blk)):
        piece = full[s:s + blk]
        if i % 2 == phase:
            piece = piece + part[(i // 2) * blk:(i // 2 + 1) * blk]
        pieces.append(piece)
    return jnp.concatenate(pieces, axis=0)


def _scatter_rows(part, blk, phase):
    zero = jnp.zeros((blk, part.shape[1]), part.dtype)
    pieces = [part[(i // 2) * blk:(i // 2 + 1) * blk] if i % 2 == phase else zero
              for i in range(2 * part.shape[0] // blk)]
    return jnp.concatenate(pieces, axis=0)


def _scan_unit(r, v, av, kd, bd, e_c, e_cx, e_nc, e_rt, e_tot, masks, st_ref, d, g):
    m_strict, mt_strict, mt_incl, bm, bm_b, eye, inv_masks = masks
    sl = slice(MXU_DIM * g, MXU_DIM * (g + 1))
    phase = 1 - d

    def expand(xb):
        return jnp.tile(xb, (HEADS_PER_GROUP, 1)) * bm_b

    def collapse(x):
        out = x[0:CHUNK]
        for h in range(1, HEADS_PER_GROUP):
            out = out + x[CHUNK * h:CHUNK * (h + 1)]
        return out

    r_t = (r[:, sl] * e_c[:, sl]).astype(BF16)
    a_t = (av[:, sl] * e_cx[:, sl]).astype(BF16)
    k_t = (kd[:, sl] * e_nc[:, sl]).astype(BF16)
    b_t = (bd[:, sl] * e_nc[:, sl]).astype(BF16)
    k_h = (kd[:, sl] * e_rt[:, sl]).astype(BF16)
    b_h = (bd[:, sl] * e_rt[:, sl]).astype(BF16)
    v_b = v[:, sl].astype(BF16)
    ea, er, eb = expand(a_t), expand(r_t), expand(b_t)
    yield

    g_ab = _dot_nt(ea, eb).astype(BF16) * m_strict
    yield
    a_ak = _dot_nt(ea, k_t).astype(BF16) * mt_strict
    yield
    a_r = _dot_nt(er, jnp.concatenate([b_t, k_t], axis=0)).astype(BF16) * mt_incl
    yield

    d1 = g_ab * inv_masks[0]
    d2 = _dot(d1, d1).astype(BF16)
    yield
    base = eye + d1.astype(F32)
    base = base + _dot(base.astype(BF16), d2)
    yield
    d4 = _dot(d2, d2).astype(BF16)
    yield
    base = base + _dot(base.astype(BF16), d4)
    yield
    inv = base.astype(BF16)
    off = g_ab * inv_masks[1]
    t_off = _dot(inv, off).astype(BF16)
    yield
    inv = inv + _dot(t_off, inv).astype(BF16)
    yield
    blk = 2 * INV_BASE
    t_off = _dot(_take_rows(inv, blk, phase), g_ab * inv_masks[2]).astype(BF16)
    yield
    inv = _add_rows(inv, _dot(t_off, inv).astype(BF16), blk, phase)
    yield

    s0 = st_ref[d, g]
    s0_b = s0.astype(BF16)
    x1 = collapse(_dot(a_ak, v_b) * bm)
    yield
    rhs = _dot_nt(a_t, s0_b) + x1
    y_st = _dot_nt(r_t, s0_b)
    yield
    blk = 4 * INV_BASE
    u32 = _dot(inv, expand(rhs.astype(BF16)))
    yield
    o_u = _dot(_take_rows(g_ab * inv_masks[3], blk, phase), u32.astype(BF16)).astype(BF16)
    yield
    e_u = _add_rows(u32, _dot(_take_rows(inv, blk, phase), _scatter_rows(o_u, blk, phase)), blk, phase)
    yield
    u_c = collapse(e_u).astype(BF16)
    y = y_st + collapse(_dot(a_r, jnp.concatenate([u_c, v_b], axis=0)) * bm)
    yield
    upd = _dot_tn(jnp.concatenate([u_c, v_b], axis=0), jnp.concatenate([b_h, k_h], axis=0))
    st_ref[d, g] = s0 * e_tot[:, sl] + upd * bm
    return y


def _run_interleaved(gens):
    results = [None] * len(gens)
    active = list(range(len(gens)))
    while active:
        for i in list(active):
            try:
                next(gens[i])
            except StopIteration as stop:
                results[i] = stop.value
                active.remove(i)
    return results


def _scan_kernel(rf_ref, vf_ref, af_ref, rb_ref, vb_ref, ab_ref, kdf_ref, bf_ref, lwf_ref, kdb_ref, bb_ref, lwb_ref,
                 tri_ref, ms_ref, mts_ref, mti_ref, bm_ref, eye_ref, im_ref, yf_ref, yb_ref, st_ref):
    @pl.when(pl.program_id(1) == 0)
    def _():
        st_ref[...] = jnp.zeros(st_ref.shape, F32)

    bm = bm_ref[...]
    bm_b = bm.astype(BF16)
    inv_masks = [im_ref[lvl] for lvl in range(N_INV_MASKS)]
    dirs = ((rf_ref, vf_ref, af_ref, kdf_ref, bf_ref, lwf_ref, yf_ref),
            (rb_ref, vb_ref, ab_ref, kdb_ref, bb_ref, lwb_ref, yb_ref))
    units = []
    for d, (r_ref, v_ref, av_ref, kd_ref, b_ref, lw_ref, _) in enumerate(dirs):
        lw = lw_ref[...]
        c = _dot_exact_lhs(tri_ref[d], lw)
        c_tot = jnp.sum(lw, axis=0, keepdims=True)
        e_c, e_cx, e_nc = jnp.exp(c), jnp.exp(c - lw), jnp.exp(-c)
        e_rt, e_tot = jnp.exp(c_tot - c), jnp.exp(c_tot)
        masks = (ms_ref[d], mts_ref[d], mti_ref[d], bm, bm_b, eye_ref[...], inv_masks)
        r, v, av, kd, bd = r_ref[...], v_ref[...], av_ref[...], kd_ref[...], b_ref[...]
        units += [_scan_unit(r, v, av, kd, bd, e_c, e_cx, e_nc, e_rt, e_tot, masks, st_ref, d, g)
                  for g in range(N_GROUPS)]
    ys = _run_interleaved(units)
    for d in range(2):
        dirs[d][-1][...] = jnp.concatenate(ys[N_GROUPS * d:N_GROUPS * (d + 1)], axis=1)


def _scan_constants():
    i = np.arange(MXU_DIM)
    blk = (i[:, None] // CHUNK) == (i[None, :] // CHUNK)
    t, s = i[:, None] % CHUNK, i[None, :] % CHUNK
    strict = np.stack([blk & (t > s), blk & (t < s)])
    j = np.arange(CHUNK)
    tall_strict = np.stack([t > j[None, :], t < j[None, :]])
    j2 = np.tile(j, 2)
    tall_incl = np.stack([t >= j2[None, :], t <= j2[None, :]])
    tri = np.stack([j[:, None] >= j[None, :], j[:, None] <= j[None, :]])
    inv_masks = [(i[:, None] // INV_BASE) == (i[None, :] // INV_BASE)]
    n = INV_BASE
    while n < CHUNK:
        inv_masks.append(((i[:, None] // (2 * n)) == (i[None, :] // (2 * n))) & ((i[:, None] // n) != (i[None, :] // n)))
        n *= 2
    assert len(inv_masks) == N_INV_MASKS
    return (jnp.asarray(tri, BF16), jnp.asarray(strict, BF16), jnp.asarray(tall_strict, BF16),
            jnp.asarray(tall_incl, BF16), jnp.asarray(blk, F32), jnp.asarray(np.eye(MXU_DIM), F32), jnp.asarray(np.stack(inv_masks), BF16))


def _rwkv_scan(r, v, av, kd, b, lw, dims):
    rows = r.shape[0]
    n_b, ncc, nct = dims["batch"], dims["c_len"] // CHUNK, dims["t_len"] // CHUNK
    consts = _scan_constants()

    def chunk(bi, s, d):
        j_ctx = s if d == 0 else ncc - 1 - s
        j_lat = s - ncc if d == 0 else nct - 1 - (s - ncc)
        return jnp.where(s < ncc, bi * ncc + j_ctx, n_b * ncc + bi * nct + j_lat)

    shared = lambda d: pl.BlockSpec((CHUNK, D_RWKV), lambda bi, s: (chunk(bi, s, d), 0))
    per_dir = lambda d: pl.BlockSpec((None, CHUNK, D_RWKV), lambda bi, s: (d, chunk(bi, s, d), 0))
    whole = lambda a: pl.BlockSpec(a.shape, lambda bi, s: (0,) * a.ndim)
    y_shape = jax.ShapeDtypeStruct((rows, D_RWKV), F32)
    return pl.pallas_call(
        _scan_kernel,
        grid=(n_b, ncc + nct),
        in_specs=[shared(0)] * 3 + [shared(1)] * 3 + [per_dir(0)] * 3 + [per_dir(1)] * 3 + [whole(a) for a in consts],
        out_specs=[shared(0), shared(1)],
        out_shape=[y_shape, y_shape],
        scratch_shapes=[pltpu.VMEM((2, N_GROUPS, MXU_DIM, MXU_DIM), F32)],
        compiler_params=_cparams(("arbitrary", "arbitrary")),
        name="rwkv_chunk_scan",
    )(r, v, av, r, v, av, kd, b, lw, kd, b, lw, *consts)


def _pool_kernel(z_ref, zp_ref, zn_ref, w_ref, sc_ref, o_ref, *, tm, dims):
    i = pl.program_id(0)
    pos0, seq_len, first, last = _seq_edges(i, tm, dims)
    prev = jnp.where(first, 0.0, zp_ref[...])
    nxt = jnp.where(last, 0.0, zn_ref[...])
    ext = jnp.concatenate([prev, z_ref[...], nxt], axis=0)
    n = tm + 2 * POOL_HALO
    t = pos0 + lax.broadcasted_iota(jnp.int32, (tm, 1), 0)
    outs = []
    for gi, w in enumerate(POOL_WINDOWS):
        e = ext[:, POOL_GROUP_DIM * gi:POOL_GROUP_DIM * (gi + 1)]
        acc, width = e, 1
        while width < w:
            acc = acc + pltpu.roll(acc, n - width, axis=0)
            width *= 2
        win = pltpu.roll(acc, w // 2, axis=0)[POOL_HALO:POOL_HALO + tm]
        lo = jnp.clip(t - w // 2, 0, seq_len)
        hi = jnp.clip(t - w // 2 + w, 0, seq_len)
        y = win / (hi - lo).astype(F32) - e[POOL_HALO:POOL_HALO + tm]
        outs.append(_dot(y.astype(BF16), w_ref[gi].astype(BF16)))
    o_ref[...] = (jnp.concatenate(outs, axis=1) * sc_ref[...]).astype(BF16)


def _pool(z_p, pool_w, pool_scale, dims):
    rows = z_p.shape[0]
    tm = dims["tm"]
    nblk8 = rows // SUBLANES
    per = tm // SUBLANES
    return pl.pallas_call(
        functools.partial(_pool_kernel, tm=tm, dims=dims),
        grid=(rows // tm,),
        in_specs=[
            pl.BlockSpec((tm, D_POOL), lambda i: (i, 0)),
            pl.BlockSpec((SUBLANES, D_POOL), lambda i: (jnp.maximum(i * per - 1, 0), 0)),
            pl.BlockSpec((SUBLANES, D_POOL), lambda i: (jnp.minimum((i + 1) * per, nblk8 - 1), 0)),
            pl.BlockSpec(pool_w.shape, lambda i: (0, 0, 0)),
            pl.BlockSpec((1, D_POOL), lambda i: (0, 0)),
        ],
        out_specs=pl.BlockSpec((tm, D_POOL), lambda i: (i, 0)),
        out_shape=jax.ShapeDtypeStruct((rows, D_POOL), BF16),
        compiler_params=_cparams(("arbitrary",)),
        name="multiscale_pool",
    )(z_p, z_p, z_p, pool_w, pool_scale.reshape(1, D_POOL))


def _attn_kernel(sink_ref, q_ref, *rest, windowed, nblk):
    if windowed:
        kp_ref, kc_ref, kn_ref, kx_ref, o_ref = rest
        kv = jnp.concatenate([kp_ref[...], kc_ref[...], kn_ref[...], kx_ref[...]], axis=0)
    else:
        kx_ref, o_ref = rest
        kv = kx_ref[...]
    n = pl.program_id(1)
    q = q_ref[...]
    k_all, v_all = kv[:, 0:ATT_KV_DIM], kv[:, ATT_KV_DIM:2 * ATT_KV_DIM]
    nk = kv.shape[0]
    rows = ATT_GROUP * ATT_BLOCK
    if windowed:
        qi = lax.broadcasted_iota(jnp.int32, (ATT_BLOCK, nk), 0)
        kj = lax.broadcasted_iota(jnp.int32, (ATT_BLOCK, nk), 1)
        rel = kj - ATT_BLOCK - qi
        k_lo = jnp.where(n == 0, ATT_BLOCK, 0)
        k_hi = jnp.where(n == nblk - 1, 2 * ATT_BLOCK, 3 * ATT_BLOCK)
        bad = jnp.where(rel > ATT_BLOCK, 1, 0) + jnp.where(rel < -ATT_BLOCK, 1, 0) \
            + jnp.where(kj < k_lo, 1, 0) + jnp.where(kj >= k_hi, 1, 0)
        bad = jnp.where(kj >= 3 * ATT_BLOCK, 0, bad)
        neg = jnp.tile(jnp.where(bad > 0, -1e30, 0.0).astype(F32), (ATT_GROUP, 1))
    lane = lax.shift_right_logical(lax.broadcasted_iota(jnp.int32, (1, ATT_KV_DIM), 1), HEAD_DIM.bit_length() - 1)
    grow = lax.shift_right_logical(lax.broadcasted_iota(jnp.int32, (rows, 1), 0), ATT_BLOCK.bit_length() - 1)
    def kv_head(hk):
        lm_f = (lane == hk).astype(F32)
        lm_b = lm_f.astype(BF16)
        lhs = jnp.concatenate([q[:, ATT_KV_DIM * g:ATT_KV_DIM * (g + 1)] * lm_b for g in range(ATT_GROUP)], axis=0)
        s = _dot_nt(lhs, k_all)
        yield
        if windowed:
            s = s + neg
        sk = jnp.where(grow == 0, sink_ref[ATT_GROUP * hk],
                       jnp.where(grow == 1, sink_ref[ATT_GROUP * hk + 1], sink_ref[ATT_GROUP * hk + 2]))
        m = jnp.maximum(jnp.max(s, axis=1, keepdims=True), sk)
        p = jnp.exp(s - m)
        den = jnp.sum(p, axis=1, keepdims=True) + jnp.exp(sk - m)
        yield
        pv = _dot(p.astype(BF16), v_all) / den
        return [pv[ATT_BLOCK * g:ATT_BLOCK * (g + 1)] * lm_f for g in range(ATT_GROUP)]

    heads = _run_interleaved([kv_head(hk) for hk in range(ATT_KV_HEADS)])
    outs = [sum((head[g] for head in heads[1:]), heads[0][g]) for g in range(ATT_GROUP)]
    o_ref[...] = jnp.concatenate(outs, axis=1).astype(BF16)


def _attention(q, kv, sink, dims, windowed):
    rows = q.shape[0]
    n_b, c_len, t_len, nc = dims["batch"], dims["c_len"], dims["t_len"], dims["nc"]
    seq = t_len if windowed else c_len
    nblk = seq // ATT_BLOCK
    base = (nc // ATT_BLOCK) if windowed else 0
    qrow = lambda bi, n: (base + bi * nblk + n, 0)
    ctx_spec = pl.BlockSpec((c_len, 2 * ATT_KV_DIM), lambda bi, n: (bi, 0))
    blk = lambda f: pl.BlockSpec((ATT_BLOCK, 2 * ATT_KV_DIM), f)
    in_specs = [pl.BlockSpec(memory_space=pltpu.SMEM), pl.BlockSpec((ATT_BLOCK, D_ATT), qrow)]
    args = [sink, q]
    if windowed:
        in_specs += [
            blk(lambda bi, n: (base + bi * nblk + jnp.maximum(n - 1, 0), 0)),
            blk(qrow),
            blk(lambda bi, n: (base + bi * nblk + jnp.minimum(n + 1, nblk - 1), 0)),
        ]
        args += [kv, kv, kv]
    in_specs.append(ctx_spec)
    args.append(kv)
    out_rows = n_b * seq
    return pl.pallas_call(
        functools.partial(_attn_kernel, windowed=windowed, nblk=nblk),
        grid=(n_b, nblk),
        in_specs=in_specs,
        out_specs=pl.BlockSpec((ATT_BLOCK, D_ATT), lambda bi, n: (bi * nblk + n, 0)),
        out_shape=jax.ShapeDtypeStruct((out_rows, D_ATT), BF16),
        compiler_params=_cparams(("arbitrary", "arbitrary")),
        name="window_attention" if windowed else "context_attention",
    )(*args)


def _outproj_kernel(yf_ref, yb_ref, bonus_ref, g_ref, yp_ref, ya_ref, x_ref, w_ref, lng_ref, lnb_ref, hs_ref,
                    gt_ref, ng_ref, sc_ref, sh_ref, xo_ref, ho_ref):
    hs = hs_ref[...]
    y = yf_ref[...] + yb_ref[...]
    inv_n = 1.0 / HEAD_DIM
    dev = y - _head_sum(y, hs) * inv_n
    var = _head_sum(dev * dev, hs) * inv_n
    yn = dev * lax.rsqrt(var + RWKV_GN_EPS) * lng_ref[...] + lnb_ref[...]
    y_r = ((yn + bonus_ref[...]) * g_ref[...]).astype(BF16)
    acc = (_dot(y_r, w_ref[0:D_RWKV])
           + _dot(yp_ref[...], w_ref[D_RWKV:D_RWKV + D_POOL])
           + _dot(ya_ref[...], w_ref[D_RWKV + D_POOL:D_RWKV + D_POOL + D_ATT]))
    x_new = x_ref[...] + gt_ref[...] * acc
    xo_ref[...] = x_new
    ho_ref[...] = _rms_mod(x_new, ng_ref[...], sc_ref[...], sh_ref[...]).astype(BF16)


def _out_projection(yf, yb, bonus, g, y_p, y_a, x_all, w_out_b, ln_g, ln_b, hs, mod3, mod_base, norm_g,
                    dims, first_tile, att_first_tile):
    rows, d = x_all.shape
    tm = dims["tm"]
    grp = dims["grp"]
    n_tiles = rows // tm - first_tile
    off = lambda w: pl.BlockSpec((tm, w), lambda i: (i + first_tile, 0))

    def mod_spec(k):
        return pl.BlockSpec((None, 1, d), lambda i: (mod_base + grp(i + first_tile, tm) * 6 + k, 0, 0))

    vec = lambda w: pl.BlockSpec((1, w), lambda i: (0, 0))
    out_rows = n_tiles * tm
    return pl.pallas_call(
        _outproj_kernel,
        grid=(n_tiles,),
        in_specs=[
            off(D_RWKV), off(D_RWKV), off(D_RWKV), off(D_RWKV), off(D_POOL),
            pl.BlockSpec((tm, D_ATT), lambda i: (i + first_tile - att_first_tile, 0)),
            off(d),
            pl.BlockSpec((d, d), lambda i: (0, 0), pipeline_mode=pl.Buffered(1)),
            vec(D_RWKV), vec(D_RWKV),
            pl.BlockSpec((MXU_DIM, MXU_DIM), lambda i: (0, 0)),
            mod_spec(2), vec(d), mod_spec(4), mod_spec(3),
        ],
        out_specs=[pl.BlockSpec((tm, d), lambda i: (i, 0)), pl.BlockSpec((tm, d), lambda i: (i, 0))],
        out_shape=[jax.ShapeDtypeStruct((out_rows, d), F32), jax.ShapeDtypeStruct((out_rows, d), BF16)],
        compiler_params=_cparams(("arbitrary",)),
        name="readout_outproj_norm",
    )(yf, yb, bonus, g, y_p, y_a, x_all, w_out_b, ln_g.reshape(1, -1), ln_b.reshape(1, -1), hs,
      mod3, norm_g.reshape(1, d), mod3, mod3)


def _ffn_kernel(h_ref, x_ref, wg_ref, wu_ref, wd_ref, gt_ref, fg_ref, o_ref, acc_ref, *, final):
    j = pl.program_id(1)

    @pl.when(j == 0)
    def _():
        acc_ref[...] = jnp.zeros(acc_ref.shape, F32)

    h = h_ref[...]
    n_sub = wg_ref.shape[1] // MXU_DIM
    slabs = [slice(MXU_DIM * q, MXU_DIM * (q + 1)) for q in range(n_sub)]
    gate_up = [(_dot(h, wg_ref[:, sl]), _dot(h, wu_ref[:, sl])) for sl in slabs]
    part = None
    for sl, (gate, up) in zip(slabs, gate_up):
        act = (gate * jax.nn.sigmoid(gate) * up).astype(BF16)
        p = _dot(act, wd_ref[sl, :])
        part = p if part is None else part + p
    acc_ref[...] += part

    @pl.when(j == pl.num_programs(1) - 1)
    def _():
        x_new = x_ref[...] + gt_ref[...] * acc_ref[...]
        if final:
            ms = jnp.mean(x_new * x_new, axis=-1, keepdims=True)
            x_new = x_new * lax.rsqrt(ms + NORM_EPS) * fg_ref[...]
        o_ref[...] = x_new


def _ffn(h2, x_new, w_gu_b, w_down_b, mod3, mod_base, final_g, dims, first_tile, final):
    rows, d = x_new.shape
    tmf, th = dims["tmf"], dims["th"]
    hidden = w_down_b.shape[0]
    nh = hidden // th
    grp = dims["grp"]
    return pl.pallas_call(
        functools.partial(_ffn_kernel, final=final),
        grid=(rows // tmf, nh),
        in_specs=[
            pl.BlockSpec((tmf, d), lambda i, j: (i, 0)),
            pl.BlockSpec((tmf, d), lambda i, j: (i, 0)),
            pl.BlockSpec((d, th), lambda i, j: (0, j)),
            pl.BlockSpec((d, th), lambda i, j: (0, nh + j)),
            pl.BlockSpec((th, d), lambda i, j: (j, 0)),
            pl.BlockSpec((None, 1, d), lambda i, j: (mod_base + grp(i + first_tile, tmf) * 6 + 5, 0, 0)),
            pl.BlockSpec((1, d), lambda i, j: (0, 0)),
        ],
        out_specs=pl.BlockSpec((tmf, d), lambda i, j: (i, 0)),
        out_shape=jax.ShapeDtypeStruct((rows, d), F32),
        scratch_shapes=[pltpu.VMEM((tmf, d), F32)],
        compiler_params=_cparams(("arbitrary", "arbitrary")),
        name="swiglu_ffn_final" if final else "swiglu_ffn",
    )(h2, x_new, w_gu_b, w_gu_b, w_down_b, mod3, final_g.reshape(1, d))


def _rope_tables(dims):
    t_len, nc, n_b = dims["t_len"], dims["nc"], dims["batch"]
    grid_rows = t_len // GRID_W
    row = jnp.repeat(jnp.arange(grid_rows), GRID_W).astype(F32)
    col = jnp.tile(jnp.arange(GRID_W), grid_rows).astype(F32)
    n_freq = HEAD_DIM // 4
    inv = ROPE_BASE ** (-jnp.arange(n_freq, dtype=F32) / n_freq)
    ang_r = row[:, None] * inv
    ang_c = col[:, None] * inv
    ang = jnp.concatenate([ang_r, ang_r, ang_c, ang_c], axis=-1)
    ang = jnp.tile(ang, (1, LANES // HEAD_DIM))
    cos, sin = jnp.cos(ang), jnp.sin(ang)
    first_half = (jnp.arange(LANES) % 32) < 16
    sin_a = jnp.where(first_half, -sin, 0.0)
    sin_b = jnp.where(first_half, 0.0, sin)
    flat = lambda tab, fill: jnp.concatenate([jnp.full((nc, LANES), fill, F32), jnp.tile(tab, (n_b, 1))], axis=0)
    return flat(cos, 1.0), flat(sin_a, 0.0), flat(sin_b, 0.0)


def _block_diag2(w):
    z = jnp.zeros_like(w[0])
    return jnp.concatenate([jnp.concatenate([w[0], z], axis=1), jnp.concatenate([z, w[1]], axis=1)], axis=0)


def kernel(x, c, ctx, c_ctx, ada_w, ada_b, norm_mix_g, norm_ffn_g, w_in, rwkv_mu, rwkv_w0, rwkv_w_up, rwkv_a0,
           rwkv_a_up, rwkv_g_up, rwkv_k_k, rwkv_k_a, rwkv_r_k, rwkv_ln_g, rwkv_ln_b, pool_w, pool_scale,
           attn_sink, w_out, ffn_w_gu, ffn_w_down, final_norm_g):
    n_b, t_len, d = x.shape
    c_len = ctx.shape[1]
    depth = w_in.shape[0]
    nc = n_b * c_len
    rows = nc + n_b * t_len
    tm = 256 if c_len % 256 == 0 else c_len
    tmf = nc
    assert n_b + 1 <= SUBLANES and c_len % tm == 0 and t_len % tm == 0 and t_len % tmf == 0
    assert tm % ATT_BLOCK == 0 and c_len % CHUNK == 0 and t_len % GRID_W == 0 and d == w_out.shape[1]

    def grp(i, tile):
        r0 = i * tile
        return jnp.where(r0 < nc, 0, 1 + jnp.maximum(r0 - nc, 0) // t_len)

    hidden = ffn_w_down.shape[1]
    th = 512 if hidden % 512 == 0 else hidden
    dims = dict(batch=n_b, c_len=c_len, t_len=t_len, nc=nc, tm=tm, tmf=tmf, th=th, grp=grp)

    cvec = jnp.concatenate([c_ctx[None, :], c, jnp.zeros((SUBLANES - 1 - n_b, d), F32)], axis=0)
    mod = _modulation(cvec, ada_w, ada_b)
    mod3 = mod.reshape(depth * SUBLANES * 6, 1, d)

    x_all = jnp.concatenate([ctx.reshape(nc, d), x.reshape(n_b * t_len, d)], axis=0)
    rope = _rope_tables(dims)
    hs = jnp.asarray(np.kron(np.eye(HEADS_PER_GROUP), np.ones((HEAD_DIM, HEAD_DIM))), BF16)

    out = None
    for l in range(depth):
        last = l == depth - 1
        mod_base = l * SUBLANES * 6
        wq = w_in[l][:, OFF_AQ:OFF_AK].reshape(d, ATT_KV_HEADS, ATT_GROUP, HEAD_DIM)
        wq = wq.transpose(0, 2, 1, 3).reshape(d, D_ATT)
        w_in_b = jnp.concatenate([w_in[l][:, :OFF_AQ], wq, w_in[l][:, OFF_AK:]], axis=1).astype(BF16)
        wo_att = w_out[l][D_RWKV + D_POOL:].reshape(ATT_KV_HEADS, ATT_GROUP, HEAD_DIM, d)
        wo_att = wo_att.transpose(1, 0, 2, 3).reshape(D_ATT, d)
        w_out_b = jnp.concatenate([w_out[l][:D_RWKV + D_POOL], wo_att], axis=0).astype(BF16)
        w_gu_b = ffn_w_gu[l].astype(BF16)
        w_down_b = ffn_w_down[l].astype(BF16)
        rw = dict(
            mu=rwkv_mu[l].reshape(1, RWKV_COLS),
            w0=rwkv_w0[l].reshape(1, 2 * D_RWKV),
            w_up=_block_diag2(rwkv_w_up[l]).astype(BF16),
            a0=rwkv_a0[l].reshape(1, 2 * D_RWKV),
            a_up=_block_diag2(rwkv_a_up[l]).astype(BF16),
            g_up=rwkv_g_up[l].astype(BF16),
            k_k=rwkv_k_k[l].reshape(1, D_RWKV),
            k_a=rwkv_k_a[l].reshape(1, D_RWKV),
            r_k=rwkv_r_k[l].reshape(1, D_RWKV),
        )

        z_r, z_p, q, kv = _in_projection(x_all, mod3, mod_base, norm_mix_g[l], w_in_b, rope, dims)
        r, v, av, g, bonus, kd, b, lw = _rwkv_prepare(z_r, rw, hs, dims)
        yf, yb = _rwkv_scan(r, v, av, kd, b, lw, dims)
        y_p = _pool(z_p, pool_w[l], pool_scale[l], dims)
        y_a = _attention(q, kv, attn_sink[l], dims, windowed=True)
        if last:
            first_tile = nc // tm
            att_first = nc // tm
        else:
            y_a = jnp.concatenate([_attention(q, kv, attn_sink[l], dims, windowed=False), y_a], axis=0)
            first_tile = 0
            att_first = 0
        x_new, h2 = _out_projection(yf, yb, bonus, g, y_p, y_a, x_all, w_out_b, rwkv_ln_g[l], rwkv_ln_b[l], hs,
                                    mod3, mod_base, norm_ffn_g[l], dims, first_tile, att_first)
        x_next = _ffn(h2, x_new, w_gu_b, w_down_b, mod3, mod_base, final_norm_g, dims,
                      first_tile * tm // tmf, final=last)
        if last:
            out = x_next.reshape(n_b, t_len, d)
        else:
            x_all = x_next
    return out
```

```python
import functools

import numpy as np
import jax
import jax.numpy as jnp
from jax import lax
from jax.experimental import pallas as pl
from jax.experimental.pallas import tpu as pltpu

F32 = jnp.float32
BF16 = jnp.bfloat16

HEAD_DIM = 64
RWKV_HEADS = 12
D_RWKV = RWKV_HEADS * HEAD_DIM
LORA_RANK = 64
GATE_RANK = 128
D_POOL = 512
POOL_WINDOWS = (2, 4, 8, 16)
POOL_GROUP_DIM = 128
POOL_HALO = 8
ATT_Q_HEADS = 12
ATT_KV_HEADS = 4
ATT_GROUP = ATT_Q_HEADS // ATT_KV_HEADS
D_ATT = ATT_Q_HEADS * HEAD_DIM
ATT_KV_DIM = ATT_KV_HEADS * HEAD_DIM
ATT_BLOCK = 128
GRID_W = 64
ROPE_BASE = 10000.0
NORM_EPS = 1e-6
RWKV_GN_EPS = 64e-5

OFF_K = D_RWKV
OFF_V = 2 * D_RWKV
OFF_WD = 3 * D_RWKV
OFF_AD = OFF_WD + 2 * LORA_RANK
OFF_GD = OFF_AD + 2 * LORA_RANK
RWKV_COLS = OFF_GD + GATE_RANK
OFF_POOL = RWKV_COLS
OFF_AQ = OFF_POOL + D_POOL
OFF_AK = OFF_AQ + D_ATT
OFF_AV = OFF_AK + ATT_KV_DIM
D_IN = OFF_AV + ATT_KV_DIM

LANES = 128
SUBLANES = 8
MXU_DIM = 256
HEADS_PER_GROUP = MXU_DIM // HEAD_DIM
N_GROUPS = D_RWKV // MXU_DIM
CHUNK = MXU_DIM // HEADS_PER_GROUP
INV_BASE = 8
N_INV_MASKS = (CHUNK // INV_BASE).bit_length() - 1
VMEM_LIMIT = 56 * 1024 * 1024


def _cparams(sem, vmem=VMEM_LIMIT):
    return pltpu.CompilerParams(dimension_semantics=sem, vmem_limit_bytes=vmem)


def _dot(a, b):
    return jnp.dot(a, b, preferred_element_type=F32)


def _dot_nt(a, b):
    return lax.dot_general(a, b, (((1,), (1,)), ((), ())), preferred_element_type=F32)


def _dot_tn(a, b):
    return lax.dot_general(a, b, (((0,), (0,)), ((), ())), preferred_element_type=F32)


def _split3(x):
    x1 = x.astype(BF16)
    r1 = x - x1.astype(F32)
    x2 = r1.astype(BF16)
    x3 = (r1 - x2.astype(F32)).astype(BF16)
    return x1, x2, x3


def _dot_exact_rhs(x, m_bf16):
    x1, x2, x3 = _split3(x)
    return _dot(x1, m_bf16) + _dot(x2, m_bf16) + _dot(x3, m_bf16)


def _dot_exact_lhs(m_bf16, x):
    x1, x2, x3 = _split3(x)
    return _dot(m_bf16, x1) + _dot(m_bf16, x2) + _dot(m_bf16, x3)


def _head_sum(x, hs):
    parts = [_dot_exact_rhs(x[:, MXU_DIM * g:MXU_DIM * (g + 1)], hs) for g in range(N_GROUPS)]
    return jnp.concatenate(parts, axis=1)


def _rms_mod(x, g, sc, sh):
    ms = jnp.mean(x * x, axis=-1, keepdims=True)
    return x * lax.rsqrt(ms + NORM_EPS) * g * (1.0 + sc) + sh


def _mod_kernel(c_ref, w_ref, b_ref, o_ref):
    c = c_ref[...]
    s = (c * jax.nn.sigmoid(c)).astype(BF16)
    o_ref[...] = _dot(s, w_ref[...].astype(BF16)) + b_ref[...]


def _modulation(cvec, ada_w, ada_b):
    n_layers, d, n6 = ada_w.shape
    tn = 1024
    return pl.pallas_call(
        _mod_kernel,
        grid=(n_layers, n6 // tn),
        in_specs=[
            pl.BlockSpec((SUBLANES, d), lambda l, j: (0, 0)),
            pl.BlockSpec((None, d, tn), lambda l, j: (l, 0, j)),
            pl.BlockSpec((None, 1, tn), lambda l, j: (l, 0, j)),
        ],
        out_specs=pl.BlockSpec((None, SUBLANES, tn), lambda l, j: (l, 0, j)),
        out_shape=jax.ShapeDtypeStruct((n_layers, SUBLANES, n6), F32),
        compiler_params=_cparams(("arbitrary", "arbitrary")),
        name="adaln_modulation",
    )(cvec, ada_w, ada_b.reshape(n_layers, 1, n6))


def _rope(z, cos, sin_a, sin_b):
    outs = []
    for j in range(z.shape[1] // LANES):
        zb = z[:, LANES * j:LANES * (j + 1)]
        outs.append(zb * cos + pltpu.roll(zb, LANES - 16, axis=1) * sin_a + pltpu.roll(zb, 16, axis=1) * sin_b)
    return jnp.concatenate(outs, axis=1)


def _inproj_kernel(x_ref, g_ref, sc_ref, sh_ref, w_ref, wq_ref, cos_ref, sa_ref, sb_ref,
                   zr_ref, zp_ref, q_ref, kv_ref):
    h = _rms_mod(x_ref[...], g_ref[...], sc_ref[...], sh_ref[...]).astype(BF16)
    zr_ref[...] = _dot(h, w_ref[:, 0:RWKV_COLS])
    zp_ref[...] = _dot(h, w_ref[:, OFF_POOL:OFF_AQ])
    cos, sa, sb = cos_ref[...], sa_ref[...], sb_ref[...]
    zq = _dot(h, wq_ref[...])
    q_ref[...] = (_rope(zq, cos, sa, sb) * (HEAD_DIM ** -0.5)).astype(BF16)
    zk = _dot(h, w_ref[:, OFF_AK:OFF_AV])
    zv = _dot(h, w_ref[:, OFF_AV:D_IN])
    kv_ref[...] = jnp.concatenate([_rope(zk, cos, sa, sb), zv], axis=1).astype(BF16)


def _in_projection(x_all, mod3, mod_base, norm_g, w_in_b, wq_b, layer, rope, dims):
    rows, d = x_all.shape
    tm = dims["tm"]
    grp = dims["grp"]
    cos, sin_a, sin_b = rope

    def mod_spec(k):
        return pl.BlockSpec((None, 1, d), lambda i: (mod_base + grp(i, tm) * 6 + k, 0, 0))

    row_spec = lambda w: pl.BlockSpec((tm, w), lambda i: (i, 0))
    return pl.pallas_call(
        _inproj_kernel,
        grid=(rows // tm,),
        in_specs=[
            row_spec(d),
            pl.BlockSpec((1, d), lambda i: (0, 0)),
            mod_spec(1), mod_spec(0),
            pl.BlockSpec((None, d, D_IN), lambda i: (layer, 0, 0), pipeline_mode=pl.Buffered(1)),
            pl.BlockSpec((None, d, D_ATT), lambda i: (layer, 0, 0), pipeline_mode=pl.Buffered(1)),
            row_spec(LANES), row_spec(LANES), row_spec(LANES),
        ],
        out_specs=[row_spec(RWKV_COLS), row_spec(D_POOL), row_spec(D_ATT), row_spec(2 * ATT_KV_DIM)],
        out_shape=[
            jax.ShapeDtypeStruct((rows, RWKV_COLS), F32),
            jax.ShapeDtypeStruct((rows, D_POOL), F32),
            jax.ShapeDtypeStruct((rows, D_ATT), BF16),
            jax.ShapeDtypeStruct((rows, 2 * ATT_KV_DIM), BF16),
        ],
        compiler_params=_cparams(("arbitrary",)),
        name="norm_inproj_rope",
    )(x_all, norm_g.reshape(1, d), mod3, mod3, w_in_b, wq_b, cos, sin_a, sin_b)


def _seq_edges(i, tm, dims):
    nc, c_len, t_len = dims["nc"], dims["c_len"], dims["t_len"]
    r0 = i * tm
    is_ctx = r0 < nc
    seq_len = jnp.where(is_ctx, c_len, t_len)
    pos0 = jnp.where(is_ctx, lax.rem(r0, c_len), lax.rem(jnp.maximum(r0 - nc, 0), t_len))
    return pos0, seq_len, pos0 == 0, pos0 + tm == seq_len


def _prep_kernel(z_ref, zp_ref, zn_ref, mu_ref, w0_ref, wup_ref, a0_ref, aup_ref, gup_ref,
                 kk_ref, ka_ref, rk_ref, hs_ref,
                 r_ref, v_ref, av_ref, g_ref, bonus_ref, kd_ref, b_ref, lw_ref, *, tm, dims):
    i = pl.program_id(0)
    _, _, first, last = _seq_edges(i, tm, dims)
    z = z_ref[...]
    prev_row = jnp.where(first, 0.0, zp_ref[SUBLANES - 1:SUBLANES, :])
    next_row = jnp.where(last, 0.0, zn_ref[0:1, :])
    row = lax.broadcasted_iota(jnp.int32, (tm, 1), 0)
    z_m1 = jnp.where(row == 0, prev_row, pltpu.roll(z, 1, axis=0))
    z_p1 = jnp.where(row == tm - 1, next_row, pltpu.roll(z, tm - 1, axis=0))
    zs = z + mu_ref[...] * (0.5 * (z_m1 + z_p1) - z)

    r = zs[:, 0:OFF_K]
    k = zs[:, OFF_K:OFF_V]
    v = zs[:, OFF_V:OFF_WD]
    wd = zs[:, OFF_WD:OFF_AD]
    ad = zs[:, OFF_AD:OFF_GD]
    gd = zs[:, OFF_GD:RWKV_COLS]

    w_pre = w0_ref[...] + _dot(jnp.tanh(wd).astype(BF16), wup_ref[...])
    w_log = jnp.minimum(w_pre, 0.0) - jnp.log(1.0 + jnp.exp(-jnp.abs(w_pre))) - 0.5
    lw = -jnp.exp(w_log)
    a = jax.nn.sigmoid(a0_ref[...] + _dot(ad.astype(BF16), aup_ref[...]))
    g_ref[...] = _dot(jax.nn.sigmoid(gd).astype(BF16), gup_ref[...])

    hs = hs_ref[...]
    kx = k * kk_ref[...]
    kn = kx / jnp.maximum(jnp.sqrt(_head_sum(kx * kx, hs)), 1e-12)
    rk = r * rk_ref[...]
    ka = ka_ref[...]
    dot_rk = None
    for d in range(2):
        a_d = a[:, D_RWKV * d:D_RWKV * (d + 1)]
        kd = k * (1.0 + (a_d - 1.0) * ka)
        kd_ref[d] = kd
        b_ref[d] = kn * a_d
        lw_ref[d] = lw[:, D_RWKV * d:D_RWKV * (d + 1)]
        dot_rk = rk * kd if dot_rk is None else dot_rk + rk * kd
    r_ref[...] = r
    v_ref[...] = v
    av_ref[...] = -kn
    bonus_ref[...] = _head_sum(dot_rk, hs) * v


def _rwkv_prepare(z_r, p, hs, dims):
    rows = z_r.shape[0]
    tm = dims["tm"]
    nblk8 = rows // SUBLANES
    per = tm // SUBLANES
    full = lambda shape: pl.BlockSpec(shape, lambda i: (0,) * len(shape))
    row_spec = pl.BlockSpec((tm, D_RWKV), lambda i: (i, 0))
    dir_spec = pl.BlockSpec((2, tm, D_RWKV), lambda i: (0, i, 0))
    one = jax.ShapeDtypeStruct((rows, D_RWKV), F32)
    two = jax.ShapeDtypeStruct((2, rows, D_RWKV), F32)
    return pl.pallas_call(
        functools.partial(_prep_kernel, tm=tm, dims=dims),
        grid=(rows // tm,),
        in_specs=[
            pl.BlockSpec((tm, RWKV_COLS), lambda i: (i, 0)),
            pl.BlockSpec((SUBLANES, RWKV_COLS), lambda i: (jnp.maximum(i * per - 1, 0), 0)),
            pl.BlockSpec((SUBLANES, RWKV_COLS), lambda i: (jnp.minimum((i + 1) * per, nblk8 - 1), 0)),
            full((1, RWKV_COLS)),
            full((1, 2 * D_RWKV)), full((2 * LORA_RANK, 2 * D_RWKV)),
            full((1, 2 * D_RWKV)), full((2 * LORA_RANK, 2 * D_RWKV)),
            full((GATE_RANK, D_RWKV)),
            full((1, D_RWKV)), full((1, D_RWKV)), full((1, D_RWKV)),
            full((MXU_DIM, MXU_DIM)),
        ],
        out_specs=[row_spec, row_spec, row_spec, row_spec, row_spec, dir_spec, dir_spec, dir_spec],
        out_shape=[one, one, one, one, one, two, two, two],
        compiler_params=_cparams(("arbitrary",)),
        name="rwkv_prepare",
    )(z_r, z_r, z_r, p["mu"], p["w0"], p["w_up"], p["a0"], p["a_up"], p["g_up"],
      p["k_k"], p["k_a"], p["r_k"], hs)


def _take_rows(x, blk, phase):
    return jnp.concatenate([x[s:s + blk] for s in range(phase * blk, x.shape[0], 2 * blk)], axis=0)


def _add_rows(full, part, blk, phase):
    pieces = []
    for i, s in enumerate(range(0, full.shape[0], blk)):
        piece = full[s:s + blk]
        if i % 2 == phase:
            piece = piece + part[(i // 2) * blk:(i // 2 + 1) * blk]
        pieces.append(piece)
    return jnp.concatenate(pieces, axis=0)


def _scatter_rows(part, blk, phase):
    zero = jnp.zeros((blk, part.shape[1]), part.dtype)
    pieces = [part[(i // 2) * blk:(i // 2 + 1) * blk] if i % 2 == phase else zero
              for i in range(2 * part.shape[0] // blk)]
    return jnp.concatenate(pieces, axis=0)


def _block_rows_sum(x):
    parts = [x[s:s + INV_BASE] for s in range(0, x.shape[0], INV_BASE)]
    while len(parts) > 1:
        parts = [parts[i] + parts[i + 1] for i in range(0, len(parts), 2)]
    return parts[0]


def _group_bcast(a, m, lane0):
    halves = []
    for h in range(a.shape[1] // LANES):
        z = a[:, LANES * h:LANES * (h + 1)]
        if m:
            z = pltpu.roll(z, LANES - m, axis=1)
        z = z * lane0
        for shift in (1, 2, 4):
            z = z + pltpu.roll(z, shift, axis=1)
        halves.append(z)
    return jnp.concatenate(halves, axis=1)


def _scan_unit(r, v, av, kd, bd, e_c, e_cx, e_nc, e_rt, e_tot, masks, st_ref, d, g):
    m_strict, mt_strict, mt_incl, bm, bm_b, inv_masks, m8, m8_strict, m8_sel, eye_c, lane0 = masks
    sl = slice(MXU_DIM * g, MXU_DIM * (g + 1))
    phase = 1 - d

    def expand(xb):
        return jnp.tile(xb, (HEADS_PER_GROUP, 1)) * bm_b

    def collapse(x):
        out = x[0:CHUNK]
        for h in range(1, HEADS_PER_GROUP):
            out = out + x[CHUNK * h:CHUNK * (h + 1)]
        return out

    r_t = (r[:, sl] * e_c[:, sl]).astype(BF16)
    a_t = (av[:, sl] * e_cx[:, sl]).astype(BF16)
    k_t = (kd[:, sl] * e_nc[:, sl]).astype(BF16)
    b_t = (bd[:, sl] * e_nc[:, sl]).astype(BF16)
    k_h = (kd[:, sl] * e_rt[:, sl]).astype(BF16)
    b_h = (bd[:, sl] * e_rt[:, sl]).astype(BF16)
    v_b = v[:, sl].astype(BF16)
    ea, er, eb = expand(a_t), expand(r_t), expand(b_t)
    yield

    g_f = _dot_nt(ea, eb)
    yield
    g_ab = g_f.astype(BF16) * m_strict
    d_c = _block_rows_sum(g_f * m8_strict)
    a_ak = _dot_nt(ea, k_t).astype(BF16) * mt_strict
    yield
    d_cols = [(m, _group_bcast(d_c, m, lane0)) for m in range(d, INV_BASE - 1 + d)]
    a_r = _dot_nt(er, jnp.concatenate([b_t, k_t], axis=0)).astype(BF16) * mt_incl
    yield

    def horner(x):
        out = eye_c
        for m, col in d_cols:
            out = out + col * jnp.broadcast_to(x[m:m + 1, :], x.shape)
        return out

    t8_c = eye_c + d_c
    for _ in range((INV_BASE - 2) // 2):
        t8_c = horner(t8_c)
    x1 = collapse(_dot(a_ak, v_b) * bm)
    yield
    for _ in range(INV_BASE - 2 - (INV_BASE - 2) // 2):
        t8_c = horner(t8_c)
    s0 = st_ref[d, g]
    s0_b = s0.astype(BF16)
    rhs = _dot_nt(a_t, s0_b) + x1
    y_st = _dot_nt(r_t, s0_b)
    yield

    t8 = jnp.tile(t8_c, (MXU_DIM // INV_BASE, 1)) * m8
    inv = t8.astype(BF16)
    t8_sel = (jnp.tile(t8_c, (MXU_DIM // (2 * INV_BASE), 1)) * m8_sel).astype(BF16)
    t_off = _dot(t8_sel, g_ab * inv_masks[0]).astype(BF16)
    yield
    inv = _add_rows(t8, _dot(t_off, inv), INV_BASE, phase).astype(BF16)
    yield
    blk = 2 * INV_BASE
    t_off = _dot(_take_rows(inv, blk, phase), g_ab * inv_masks[1]).astype(BF16)
    yield
    inv = _add_rows(inv, _dot(t_off, inv).astype(BF16), blk, phase)
    yield
    blk = 4 * INV_BASE
    u32 = _dot(inv, expand(rhs.astype(BF16)))
    yield
    o_u = _dot(_take_rows(g_ab * inv_masks[2], blk, phase), u32.astype(BF16)).astype(BF16)
    yield
    e_u = _add_rows(u32, _dot(_take_rows(inv, blk, phase), _scatter_rows(o_u, blk, phase)), blk, phase)
    yield
    u_c = collapse(e_u).astype(BF16)
    y = y_st + collapse(_dot(a_r, jnp.concatenate([u_c, v_b], axis=0)) * bm)
    yield
    upd = _dot_tn(jnp.concatenate([u_c, v_b], axis=0), jnp.concatenate([b_h, k_h], axis=0))
    st_ref[d, g] = s0 * e_tot[:, sl] + upd * bm
    return y


def _run_interleaved(gens):
    results = [None] * len(gens)
    active = list(range(len(gens)))
    while active:
        for i in list(active):
            try:
                next(gens[i])
            except StopIteration as stop:
                results[i] = stop.value
                active.remove(i)
    return results


N_SCAN_CONSTS = 11
N_SCAN_STREAMS = 6


def _scan_kernel(*refs, n_b):
    n_in = n_b * 2 * N_SCAN_STREAMS
    streams, consts = refs[:n_in], refs[n_in:n_in + N_SCAN_CONSTS]
    y_refs, st_ref = refs[n_in + N_SCAN_CONSTS:-1], refs[-1]
    tri_ref, ms_ref, mts_ref, mti_ref, bm_ref, im_ref, m8_ref, m8s_ref, m8sel_ref, eyec_ref, lane0_ref = consts

    @pl.when(pl.program_id(0) == 0)
    def _():
        st_ref[...] = jnp.zeros(st_ref.shape, F32)

    bm = bm_ref[...]
    bm_b = bm.astype(BF16)
    inv_masks = [im_ref[lvl] for lvl in range(N_INV_MASKS)]
    m8, eye_c, lane0 = m8_ref[...], eyec_ref[...], lane0_ref[...]
    units = []
    for bi in range(n_b):
        for d in range(2):
            base = (bi * 2 + d) * N_SCAN_STREAMS
            r_ref, v_ref, av_ref, kd_ref, b_ref, lw_ref = streams[base:base + N_SCAN_STREAMS]
            lw = lw_ref[...]
            c = _dot_exact_lhs(tri_ref[d], lw)
            c_tot = jnp.sum(lw, axis=0, keepdims=True)
            e_c, e_cx, e_nc = jnp.exp(c), jnp.exp(c - lw), jnp.exp(-c)
            e_rt, e_tot = jnp.exp(c_tot - c), jnp.exp(c_tot)
            masks = (ms_ref[d], mts_ref[d], mti_ref[d], bm, bm_b, inv_masks, m8, m8s_ref[d], m8sel_ref[1 - d],
                     eye_c, lane0)
            r, v, av, kd, bd = r_ref[...], v_ref[...], av_ref[...], kd_ref[...], b_ref[...]
            units += [_scan_unit(r, v, av, kd, bd, e_c, e_cx, e_nc, e_rt, e_tot, masks, st_ref.at[bi], d, g)
                      for g in range(N_GROUPS)]
    ys = _run_interleaved(units)
    for bi in range(n_b):
        for d in range(2):
            first = (bi * 2 + d) * N_GROUPS
            y_refs[d][bi] = jnp.concatenate(ys[first:first + N_GROUPS], axis=1)


def _scan_constants():
    i = np.arange(MXU_DIM)
    blk = (i[:, None] // CHUNK) == (i[None, :] // CHUNK)
    t, s = i[:, None] % CHUNK, i[None, :] % CHUNK
    strict = np.stack([blk & (t > s), blk & (t < s)])
    j = np.arange(CHUNK)
    tall_strict = np.stack([t > j[None, :], t < j[None, :]])
    j2 = np.tile(j, 2)
    tall_incl = np.stack([t >= j2[None, :], t <= j2[None, :]])
    tri = np.stack([j[:, None] >= j[None, :], j[:, None] <= j[None, :]])
    inv_masks = []
    n = INV_BASE
    while n < CHUNK:
        inv_masks.append(((i[:, None] // (2 * n)) == (i[None, :] // (2 * n))) & ((i[:, None] // n) != (i[None, :] // n)))
        n *= 2
    assert len(inv_masks) == N_INV_MASKS
    m8 = (i[:, None] // INV_BASE) == (i[None, :] // INV_BASE)
    m8_strict = m8[None] & strict
    row_blk = np.arange(MXU_DIM // 2) // INV_BASE
    m8_sel = np.stack([m8[(2 * row_blk + p) * INV_BASE + np.arange(MXU_DIM // 2) % INV_BASE] for p in (0, 1)])
    eye_c = np.arange(INV_BASE)[:, None] == (i[None, :] % INV_BASE)
    lane0 = np.broadcast_to((np.arange(LANES) % INV_BASE) == 0, (INV_BASE, LANES))
    f32 = lambda a: jnp.asarray(a, F32)
    b16 = lambda a: jnp.asarray(a, BF16)
    return (b16(tri), b16(strict), b16(tall_strict), b16(tall_incl), f32(blk), b16(np.stack(inv_masks)),
            f32(m8), f32(m8_strict), f32(m8_sel), f32(eye_c), f32(lane0))


def _rwkv_scan(r, v, av, kd, b, lw, dims):
    rows = r.shape[0]
    n_b, ncc, nct = dims["batch"], dims["c_len"] // CHUNK, dims["t_len"] // CHUNK
    consts = _scan_constants()

    assert len(consts) == N_SCAN_CONSTS

    def chunk(s, d):
        j_ctx = s if d == 0 else ncc - 1 - s
        j_lat = s - ncc if d == 0 else nct - 1 - (s - ncc)
        return j_ctx, j_lat

    def flat_chunk(bi, d):
        def index(s):
            j_ctx, j_lat = chunk(s, d)
            return jnp.where(s < ncc, bi * ncc + j_ctx, n_b * ncc + bi * nct + j_lat)
        return index

    in_specs, args = [], []
    for bi in range(n_b):
        for d in range(2):
            idx = flat_chunk(bi, d)
            shared = pl.BlockSpec((CHUNK, D_RWKV), lambda s, idx=idx: (idx(s), 0))
            per_dir = pl.BlockSpec((None, CHUNK, D_RWKV), lambda s, idx=idx, d=d: (d, idx(s), 0))
            in_specs += [shared] * 3 + [per_dir] * 3
            args += [r, v, av, kd, b, lw]
    in_specs += [pl.BlockSpec(a.shape, lambda s, nd=a.ndim: (0,) * nd) for a in consts]

    def y_spec(d):
        def index(s):
            j_ctx, j_lat = chunk(s, d)
            return (0, jnp.where(s < ncc, j_ctx, ncc + j_lat), 0)
        return pl.BlockSpec((n_b, CHUNK, D_RWKV), index)

    y_shape = jax.ShapeDtypeStruct((n_b, rows // n_b, D_RWKV), F32)
    return pl.pallas_call(
        functools.partial(_scan_kernel, n_b=n_b),
        grid=(ncc + nct,),
        in_specs=in_specs,
        out_specs=[y_spec(0), y_spec(1)],
        out_shape=[y_shape, y_shape],
        scratch_shapes=[pltpu.VMEM((n_b, 2, N_GROUPS, MXU_DIM, MXU_DIM), F32)],
        compiler_params=_cparams(("arbitrary",)),
        name="rwkv_chunk_scan",
    )(*args, *consts)


def _pool_kernel(z_ref, zp_ref, zn_ref, w_ref, sc_ref, o_ref, *, tm, dims):
    i = pl.program_id(0)
    pos0, seq_len, first, last = _seq_edges(i, tm, dims)
    prev = jnp.where(first, 0.0, zp_ref[...])
    nxt = jnp.where(last, 0.0, zn_ref[...])
    ext = jnp.concatenate([prev, z_ref[...], nxt], axis=0)
    n = tm + 2 * POOL_HALO
    t = pos0 + lax.broadcasted_iota(jnp.int32, (tm, 1), 0)
    outs = []
    for gi, w in enumerate(POOL_WINDOWS):
        e = ext[:, POOL_GROUP_DIM * gi:POOL_GROUP_DIM * (gi + 1)]
        acc, width = e, 1
        while width < w:
            acc = acc + pltpu.roll(acc, n - width, axis=0)
            width *= 2
        win = pltpu.roll(acc, w // 2, axis=0)[POOL_HALO:POOL_HALO + tm]
        lo = jnp.clip(t - w // 2, 0, seq_len)
        hi = jnp.clip(t - w // 2 + w, 0, seq_len)
        y = win / (hi - lo).astype(F32) - e[POOL_HALO:POOL_HALO + tm]
        outs.append(_dot(y.astype(BF16), w_ref[gi].astype(BF16)))
    o_ref[...] = (jnp.concatenate(outs, axis=1) * sc_ref[...]).astype(BF16)


def _pool(z_p, pool_w, pool_scale, dims):
    rows = z_p.shape[0]
    tm = dims["tm"]
    nblk8 = rows // SUBLANES
    per = tm // SUBLANES
    return pl.pallas_call(
        functools.partial(_pool_kernel, tm=tm, dims=dims),
        grid=(rows // tm,),
        in_specs=[
            pl.BlockSpec((tm, D_POOL), lambda i: (i, 0)),
            pl.BlockSpec((SUBLANES, D_POOL), lambda i: (jnp.maximum(i * per - 1, 0), 0)),
            pl.BlockSpec((SUBLANES, D_POOL), lambda i: (jnp.minimum((i + 1) * per, nblk8 - 1), 0)),
            pl.BlockSpec(pool_w.shape, lambda i: (0, 0, 0)),
            pl.BlockSpec((1, D_POOL), lambda i: (0, 0)),
        ],
        out_specs=pl.BlockSpec((tm, D_POOL), lambda i: (i, 0)),
        out_shape=jax.ShapeDtypeStruct((rows, D_POOL), BF16),
        compiler_params=_cparams(("arbitrary",)),
        name="multiscale_pool",
    )(z_p, z_p, z_p, pool_w, pool_scale.reshape(1, D_POOL))


def _attn_kernel(sink_ref, q_ref, *rest, windowed, nblk):
    if windowed:
        kp_ref, kc_ref, kn_ref, kx_ref, o_ref = rest
        kv = jnp.concatenate([kp_ref[...], kc_ref[...], kn_ref[...], kx_ref[...]], axis=0)
    else:
        kx_ref, o_ref = rest
        kv = kx_ref[...]
    n = pl.program_id(1)
    q = q_ref[...]
    k_all, v_all = kv[:, 0:ATT_KV_DIM], kv[:, ATT_KV_DIM:2 * ATT_KV_DIM]
    nk = kv.shape[0]
    rows = ATT_GROUP * ATT_BLOCK
    if windowed:
        qi = lax.broadcasted_iota(jnp.int32, (ATT_BLOCK, nk), 0)
        kj = lax.broadcasted_iota(jnp.int32, (ATT_BLOCK, nk), 1)
        rel = kj - ATT_BLOCK - qi
        k_lo = jnp.where(n == 0, ATT_BLOCK, 0)
        k_hi = jnp.where(n == nblk - 1, 2 * ATT_BLOCK, 3 * ATT_BLOCK)
        bad = jnp.where(rel > ATT_BLOCK, 1, 0) + jnp.where(rel < -ATT_BLOCK, 1, 0) \
            + jnp.where(kj < k_lo, 1, 0) + jnp.where(kj >= k_hi, 1, 0)
        bad = jnp.where(kj >= 3 * ATT_BLOCK, 0, bad)
        neg = jnp.tile(jnp.where(bad > 0, -1e30, 0.0).astype(F32), (ATT_GROUP, 1))
    lane = lax.shift_right_logical(lax.broadcasted_iota(jnp.int32, (1, ATT_KV_DIM), 1), HEAD_DIM.bit_length() - 1)
    grow = lax.shift_right_logical(lax.broadcasted_iota(jnp.int32, (rows, 1), 0), ATT_BLOCK.bit_length() - 1)
    def kv_head(hk):
        lm_f = (lane == hk).astype(F32)
        lm_b = lm_f.astype(BF16)
        lhs = jnp.concatenate([q[:, ATT_KV_DIM * g:ATT_KV_DIM * (g + 1)] * lm_b for g in range(ATT_GROUP)], axis=0)
        s = _dot_nt(lhs, k_all)
        yield
        if windowed:
            s = s + neg
        sk = jnp.where(grow == 0, sink_ref[ATT_GROUP * hk],
                       jnp.where(grow == 1, sink_ref[ATT_GROUP * hk + 1], sink_ref[ATT_GROUP * hk + 2]))
        m = jnp.maximum(jnp.max(s, axis=1, keepdims=True), sk)
        p = jnp.exp(s - m)
        den = jnp.sum(p, axis=1, keepdims=True) + jnp.exp(sk - m)
        yield
        pv = _dot(p.astype(BF16), v_all) / den
        return [pv[ATT_BLOCK * g:ATT_BLOCK * (g + 1)] * lm_f for g in range(ATT_GROUP)]

    heads = _run_interleaved([kv_head(hk) for hk in range(ATT_KV_HEADS)])
    outs = [sum((head[g] for head in heads[1:]), heads[0][g]) for g in range(ATT_GROUP)]
    o_ref[...] = jnp.concatenate(outs, axis=1).astype(BF16)


def _attention(q, kv, sink, dims, windowed):
    rows = q.shape[0]
    n_b, c_len, t_len, nc = dims["batch"], dims["c_len"], dims["t_len"], dims["nc"]
    seq = t_len if windowed else c_len
    nblk = seq // ATT_BLOCK
    base = (nc // ATT_BLOCK) if windowed else 0
    qrow = lambda bi, n: (base + bi * nblk + n, 0)
    ctx_spec = pl.BlockSpec((c_len, 2 * ATT_KV_DIM), lambda bi, n: (bi, 0))
    blk = lambda f: pl.BlockSpec((ATT_BLOCK, 2 * ATT_KV_DIM), f)
    in_specs = [pl.BlockSpec(memory_space=pltpu.SMEM), pl.BlockSpec((ATT_BLOCK, D_ATT), qrow)]
    args = [sink, q]
    if windowed:
        in_specs += [
            blk(lambda bi, n: (base + bi * nblk + jnp.maximum(n - 1, 0), 0)),
            blk(qrow),
            blk(lambda bi, n: (base + bi * nblk + jnp.minimum(n + 1, nblk - 1), 0)),
        ]
        args += [kv, kv, kv]
    in_specs.append(ctx_spec)
    args.append(kv)
    out_rows = n_b * seq
    return pl.pallas_call(
        functools.partial(_attn_kernel, windowed=windowed, nblk=nblk),
        grid=(n_b, nblk),
        in_specs=in_specs,
        out_specs=pl.BlockSpec((ATT_BLOCK, D_ATT), lambda bi, n: (bi * nblk + n, 0)),
        out_shape=jax.ShapeDtypeStruct((out_rows, D_ATT), BF16),
        compiler_params=_cparams(("arbitrary", "arbitrary")),
        name="window_attention" if windowed else "context_attention",
    )(*args)


def _outproj_kernel(yf_ref, yb_ref, bonus_ref, g_ref, yp_ref, ya_ref, x_ref, w_ref, wa_ref, lng_ref, lnb_ref, hs_ref,
                    gt_ref, ng_ref, sc_ref, sh_ref, xo_ref, ho_ref):
    hs = hs_ref[...]
    y = yf_ref[...] + yb_ref[...]
    inv_n = 1.0 / HEAD_DIM
    dev = y - _head_sum(y, hs) * inv_n
    var = _head_sum(dev * dev, hs) * inv_n
    yn = dev * lax.rsqrt(var + RWKV_GN_EPS) * lng_ref[...] + lnb_ref[...]
    y_r = ((yn + bonus_ref[...]) * g_ref[...]).astype(BF16)
    acc = (_dot(y_r, w_ref[0:D_RWKV])
           + _dot(yp_ref[...], w_ref[D_RWKV:D_RWKV + D_POOL])
           + _dot(ya_ref[...], wa_ref[...]))
    x_new = x_ref[...] + gt_ref[...] * acc
    xo_ref[...] = x_new
    ho_ref[...] = _rms_mod(x_new, ng_ref[...], sc_ref[...], sh_ref[...]).astype(BF16)


def _out_projection(yf, yb, bonus, g, y_p, y_a, x_all, w_out_b, wa_b, layer, ln_g, ln_b, hs, mod3, mod_base, norm_g,
                    dims, first_tile, att_first_tile):
    rows, d = x_all.shape
    tm = dims["tm"]
    grp = dims["grp"]
    n_tiles = rows // tm - first_tile
    off = lambda w: pl.BlockSpec((tm, w), lambda i: (i + first_tile, 0))

    def mod_spec(k):
        return pl.BlockSpec((None, 1, d), lambda i: (mod_base + grp(i + first_tile, tm) * 6 + k, 0, 0))

    def y_index(i):
        ft = i + first_tile
        ctx_tiles, per_ctx, per_lat = dims["nc"] // tm, dims["c_len"] // tm, dims["t_len"] // tm
        lt = jnp.maximum(ft - ctx_tiles, 0)
        in_ctx = ft < ctx_tiles
        return (jnp.where(in_ctx, ft // per_ctx, lt // per_lat),
                jnp.where(in_ctx, ft % per_ctx, per_ctx + lt % per_lat), 0)

    y_spec = pl.BlockSpec((None, tm, D_RWKV), y_index)
    vec = lambda w: pl.BlockSpec((1, w), lambda i: (0, 0))
    out_rows = n_tiles * tm
    return pl.pallas_call(
        _outproj_kernel,
        grid=(n_tiles,),
        in_specs=[
            y_spec, y_spec, off(D_RWKV), off(D_RWKV), off(D_POOL),
            pl.BlockSpec((tm, D_ATT), lambda i: (i + first_tile - att_first_tile, 0)),
            off(d),
            pl.BlockSpec((None, d, d), lambda i: (layer, 0, 0), pipeline_mode=pl.Buffered(1)),
            pl.BlockSpec((None, D_ATT, d), lambda i: (layer, 0, 0), pipeline_mode=pl.Buffered(1)),
            vec(D_RWKV), vec(D_RWKV),
            pl.BlockSpec((MXU_DIM, MXU_DIM), lambda i: (0, 0)),
            mod_spec(2), vec(d), mod_spec(4), mod_spec(3),
        ],
        out_specs=[pl.BlockSpec((tm, d), lambda i: (i, 0)), pl.BlockSpec((tm, d), lambda i: (i, 0))],
        out_shape=[jax.ShapeDtypeStruct((out_rows, d), F32), jax.ShapeDtypeStruct((out_rows, d), BF16)],
        compiler_params=_cparams(("arbitrary",)),
        name="readout_outproj_norm",
    )(yf, yb, bonus, g, y_p, y_a, x_all, w_out_b, wa_b, ln_g.reshape(1, -1), ln_b.reshape(1, -1), hs,
      mod3, norm_g.reshape(1, d), mod3, mod3)


def _ffn_kernel(h_ref, x_ref, wg_ref, wu_ref, wd_ref, gt_ref, fg_ref, o_ref, acc_ref, *, final):
    j = pl.program_id(1)

    @pl.when(j == 0)
    def _():
        acc_ref[...] = jnp.zeros(acc_ref.shape, F32)

    h = h_ref[...]
    n_sub = wg_ref.shape[1] // MXU_DIM
    slabs = [slice(MXU_DIM * q, MXU_DIM * (q + 1)) for q in range(n_sub)]
    gate_up = [(_dot(h, wg_ref[:, sl]), _dot(h, wu_ref[:, sl])) for sl in slabs]
    part = None
    for sl, (gate, up) in zip(slabs, gate_up):
        act = (gate * jax.nn.sigmoid(gate) * up).astype(BF16)
        p = _dot(act, wd_ref[sl, :])
        part = p if part is None else part + p
    acc_ref[...] += part

    @pl.when(j == pl.num_programs(1) - 1)
    def _():
        x_new = x_ref[...] + gt_ref[...] * acc_ref[...]
        if final:
            ms = jnp.mean(x_new * x_new, axis=-1, keepdims=True)
            x_new = x_new * lax.rsqrt(ms + NORM_EPS) * fg_ref[...]
        o_ref[...] = x_new


def _ffn(h2, x_new, w_gu_b, w_down_b, layer, mod3, mod_base, final_g, dims, first_tile, final):
    rows, d = x_new.shape
    tmf, th = dims["tmf"], dims["th"]
    hidden = w_down_b.shape[1]
    nh = hidden // th
    grp = dims["grp"]
    return pl.pallas_call(
        functools.partial(_ffn_kernel, final=final),
        grid=(rows // tmf, nh),
        in_specs=[
            pl.BlockSpec((tmf, d), lambda i, j: (i, 0)),
            pl.BlockSpec((tmf, d), lambda i, j: (i, 0)),
            pl.BlockSpec((None, d, th), lambda i, j: (layer, 0, j)),
            pl.BlockSpec((None, d, th), lambda i, j: (layer, 0, nh + j)),
            pl.BlockSpec((None, th, d), lambda i, j: (layer, j, 0)),
            pl.BlockSpec((None, 1, d), lambda i, j: (mod_base + grp(i + first_tile, tmf) * 6 + 5, 0, 0)),
            pl.BlockSpec((1, d), lambda i, j: (0, 0)),
        ],
        out_specs=pl.BlockSpec((tmf, d), lambda i, j: (i, 0)),
        out_shape=jax.ShapeDtypeStruct((rows, d), F32),
        scratch_shapes=[pltpu.VMEM((tmf, d), F32)],
        compiler_params=_cparams(("arbitrary", "arbitrary")),
        name="swiglu_ffn_final" if final else "swiglu_ffn",
    )(h2, x_new, w_gu_b, w_gu_b, w_down_b, mod3, final_g.reshape(1, d))


def _rope_tables(dims):
    t_len, nc, n_b = dims["t_len"], dims["nc"], dims["batch"]
    grid_rows = t_len // GRID_W
    n_freq = HEAD_DIM // 4
    inv = ROPE_BASE ** (-jnp.arange(n_freq, dtype=F32) / n_freq)
    ang_r = jnp.arange(grid_rows).astype(F32)[:, None] * inv
    ang_c = jnp.arange(GRID_W).astype(F32)[:, None] * inv

    def per_token(fn):
        by_row = jnp.repeat(fn(ang_r), GRID_W, axis=0)
        by_col = jnp.tile(fn(ang_c), (grid_rows, 1))
        return jnp.tile(jnp.concatenate([by_row, by_row, by_col, by_col], axis=-1), (1, LANES // HEAD_DIM))

    cos, sin = per_token(jnp.cos), per_token(jnp.sin)
    first_half = (jnp.arange(LANES) % 32) < 16
    sin_a = jnp.where(first_half, -sin, 0.0)
    sin_b = jnp.where(first_half, 0.0, sin)
    flat = lambda tab, fill: jnp.concatenate([jnp.full((nc, LANES), fill, F32), jnp.tile(tab, (n_b, 1))], axis=0)
    return flat(cos, 1.0), flat(sin_a, 0.0), flat(sin_b, 0.0)


def _block_diag2(w):
    z = jnp.zeros_like(w[0])
    return jnp.concatenate([jnp.concatenate([w[0], z], axis=1), jnp.concatenate([z, w[1]], axis=1)], axis=0)


def kernel(x, c, ctx, c_ctx, ada_w, ada_b, norm_mix_g, norm_ffn_g, w_in, rwkv_mu, rwkv_w0, rwkv_w_up, rwkv_a0,
           rwkv_a_up, rwkv_g_up, rwkv_k_k, rwkv_k_a, rwkv_r_k, rwkv_ln_g, rwkv_ln_b, pool_w, pool_scale,
           attn_sink, w_out, ffn_w_gu, ffn_w_down, final_norm_g):
    n_b, t_len, d = x.shape
    c_len = ctx.shape[1]
    depth = w_in.shape[0]
    nc = n_b * c_len
    rows = nc + n_b * t_len
    tm = 256 if c_len % 256 == 0 else c_len
    tmf = nc
    assert n_b + 1 <= SUBLANES and c_len % tm == 0 and t_len % tm == 0 and t_len % tmf == 0
    assert tm % ATT_BLOCK == 0 and c_len % CHUNK == 0 and t_len % GRID_W == 0 and d == w_out.shape[1]

    def grp(i, tile):
        r0 = i * tile
        return jnp.where(r0 < nc, 0, 1 + jnp.maximum(r0 - nc, 0) // t_len)

    hidden = ffn_w_down.shape[1]
    th = 512 if hidden % 512 == 0 else hidden
    dims = dict(batch=n_b, c_len=c_len, t_len=t_len, nc=nc, tm=tm, tmf=tmf, th=th, grp=grp)

    cvec = jnp.concatenate([c_ctx[None, :], c, jnp.zeros((SUBLANES - 1 - n_b, d), F32)], axis=0)
    mod = _modulation(cvec, ada_w, ada_b)
    mod3 = mod.reshape(depth * SUBLANES * 6, 1, d)

    x_all = jnp.concatenate([ctx.reshape(nc, d), x.reshape(n_b * t_len, d)], axis=0)
    rope = _rope_tables(dims)
    hs = jnp.asarray(np.kron(np.eye(HEADS_PER_GROUP), np.ones((HEAD_DIM, HEAD_DIM))), BF16)

    w_in_b = w_in.astype(BF16)
    wq_b = w_in_b[:, :, OFF_AQ:OFF_AK].reshape(depth, d, ATT_KV_HEADS, ATT_GROUP, HEAD_DIM)
    wq_b = wq_b.transpose(0, 1, 3, 2, 4).reshape(depth, d, D_ATT)
    w_out_b = w_out.astype(BF16)
    wa_b = w_out_b[:, D_RWKV + D_POOL:].reshape(depth, ATT_KV_HEADS, ATT_GROUP, HEAD_DIM, d)
    wa_b = wa_b.transpose(0, 2, 1, 3, 4).reshape(depth, D_ATT, d)
    w_gu_b = ffn_w_gu.astype(BF16)
    w_down_b = ffn_w_down.astype(BF16)

    out = None
    for l in range(depth):
        last = l == depth - 1
        mod_base = l * SUBLANES * 6
        rw = dict(
            mu=rwkv_mu[l].reshape(1, RWKV_COLS),
            w0=rwkv_w0[l].reshape(1, 2 * D_RWKV),
            w_up=_block_diag2(rwkv_w_up[l]).astype(BF16),
            a0=rwkv_a0[l].reshape(1, 2 * D_RWKV),
            a_up=_block_diag2(rwkv_a_up[l]).astype(BF16),
            g_up=rwkv_g_up[l].astype(BF16),
            k_k=rwkv_k_k[l].reshape(1, D_RWKV),
            k_a=rwkv_k_a[l].reshape(1, D_RWKV),
            r_k=rwkv_r_k[l].reshape(1, D_RWKV),
        )

        z_r, z_p, q, kv = _in_projection(x_all, mod3, mod_base, norm_mix_g[l], w_in_b, wq_b, l, rope, dims)
        r, v, av, g, bonus, kd, b, lw = _rwkv_prepare(z_r, rw, hs, dims)
        yf, yb = _rwkv_scan(r, v, av, kd, b, lw, dims)
        y_p = _pool(z_p, pool_w[l], pool_scale[l], dims)
        y_a = _attention(q, kv, attn_sink[l], dims, windowed=True)
        if last:
            first_tile = nc // tm
            att_first = nc // tm
        else:
            y_a = jnp.concatenate([_attention(q, kv, attn_sink[l], dims, windowed=False), y_a], axis=0)
            first_tile = 0
            att_first = 0
        x_new, h2 = _out_projection(yf, yb, bonus, g, y_p, y_a, x_all, w_out_b, wa_b, l, rwkv_ln_g[l], rwkv_ln_b[l],
                                    hs, mod3, mod_base, norm_ffn_g[l], dims, first_tile, att_first)
        x_next = _ffn(h2, x_new, w_gu_b, w_down_b, l, mod3, mod_base, final_norm_g, dims,
                      first_tile * tm // tmf, final=last)
        if last:
            out = x_next.reshape(n_b, t_len, d)
        else:
            x_all = x_next
    return out
```

```python
import functools

import numpy as np
import jax
import jax.numpy as jnp
from jax import lax
from jax.experimental import pallas as pl
from jax.experimental.pallas import tpu as pltpu

F32 = jnp.float32
BF16 = jnp.bfloat16

HEAD_DIM = 64
RWKV_HEADS = 12
D_RWKV = RWKV_HEADS * HEAD_DIM
LORA_RANK = 64
GATE_RANK = 128
D_POOL = 512
POOL_WINDOWS = (2, 4, 8, 16)
POOL_GROUP_DIM = 128
POOL_HALO = 8
ATT_Q_HEADS = 12
ATT_KV_HEADS = 4
ATT_GROUP = ATT_Q_HEADS // ATT_KV_HEADS
D_ATT = ATT_Q_HEADS * HEAD_DIM
ATT_KV_DIM = ATT_KV_HEADS * HEAD_DIM
ATT_BLOCK = 128
GRID_W = 64
ROPE_BASE = 10000.0
NORM_EPS = 1e-6
RWKV_GN_EPS = 64e-5

OFF_K = D_RWKV
OFF_V = 2 * D_RWKV
OFF_WD = 3 * D_RWKV
OFF_AD = OFF_WD + 2 * LORA_RANK
OFF_GD = OFF_AD + 2 * LORA_RANK
RWKV_COLS = OFF_GD + GATE_RANK
OFF_POOL = RWKV_COLS
OFF_AQ = OFF_POOL + D_POOL
OFF_AK = OFF_AQ + D_ATT
OFF_AV = OFF_AK + ATT_KV_DIM
D_IN = OFF_AV + ATT_KV_DIM

LANES = 128
SUBLANES = 8
MXU_DIM = 256
HEADS_PER_GROUP = MXU_DIM // HEAD_DIM
N_GROUPS = D_RWKV // MXU_DIM
CHUNK = MXU_DIM // HEADS_PER_GROUP
INV_BASE = 8
N_INV_MASKS = (CHUNK // INV_BASE).bit_length() - 1
VMEM_LIMIT = 56 * 1024 * 1024


def _cparams(sem, vmem=VMEM_LIMIT):
    return pltpu.CompilerParams(dimension_semantics=sem, vmem_limit_bytes=vmem)


def _dot(a, b):
    return jnp.dot(a, b, preferred_element_type=F32)


def _dot_nt(a, b):
    return lax.dot_general(a, b, (((1,), (1,)), ((), ())), preferred_element_type=F32)


def _dot_tn(a, b):
    return lax.dot_general(a, b, (((0,), (0,)), ((), ())), preferred_element_type=F32)


def _split3(x):
    x1 = x.astype(BF16)
    r1 = x - x1.astype(F32)
    x2 = r1.astype(BF16)
    x3 = (r1 - x2.astype(F32)).astype(BF16)
    return x1, x2, x3


def _dot_exact_rhs(x, m_bf16):
    x1, x2, x3 = _split3(x)
    return _dot(x1, m_bf16) + _dot(x2, m_bf16) + _dot(x3, m_bf16)


def _dot_exact_lhs(m_bf16, x):
    x1, x2, x3 = _split3(x)
    return _dot(m_bf16, x1) + _dot(m_bf16, x2) + _dot(m_bf16, x3)


def _head_sum(x, hs):
    parts = [_dot_exact_rhs(x[:, MXU_DIM * g:MXU_DIM * (g + 1)], hs) for g in range(N_GROUPS)]
    return jnp.concatenate(parts, axis=1)


def _rms_mod(x, g, sc, sh):
    ms = jnp.mean(x * x, axis=-1, keepdims=True)
    return x * lax.rsqrt(ms + NORM_EPS) * g * (1.0 + sc) + sh


def _mod_kernel(c_ref, w_ref, b_ref, o_ref):
    c = c_ref[...]
    s = (c * jax.nn.sigmoid(c)).astype(BF16)
    o_ref[...] = _dot(s, w_ref[...].astype(BF16)) + b_ref[...]


def _modulation(cvec, ada_w, ada_b):
    n_layers, d, n6 = ada_w.shape
    tn = 1024
    return pl.pallas_call(
        _mod_kernel,
        grid=(n_layers, n6 // tn),
        in_specs=[
            pl.BlockSpec((SUBLANES, d), lambda l, j: (0, 0)),
            pl.BlockSpec((None, d, tn), lambda l, j: (l, 0, j)),
            pl.BlockSpec((None, 1, tn), lambda l, j: (l, 0, j)),
        ],
        out_specs=pl.BlockSpec((None, SUBLANES, tn), lambda l, j: (l, 0, j)),
        out_shape=jax.ShapeDtypeStruct((n_layers, SUBLANES, n6), F32),
        compiler_params=_cparams(("arbitrary", "arbitrary")),
        name="adaln_modulation",
    )(cvec, ada_w, ada_b.reshape(n_layers, 1, n6))


def _rope(z, cos, sin_a, sin_b):
    outs = []
    for j in range(z.shape[1] // LANES):
        zb = z[:, LANES * j:LANES * (j + 1)]
        outs.append(zb * cos + pltpu.roll(zb, LANES - 16, axis=1) * sin_a + pltpu.roll(zb, 16, axis=1) * sin_b)
    return jnp.concatenate(outs, axis=1)


def _two_source_specs(src_ctx, src_lat, lat_first_tile, tm, ctx_tiles, first_tile):
    width = src_ctx.shape[1]
    spec_c = pl.BlockSpec((tm, width), lambda i: (jnp.minimum(i + first_tile, ctx_tiles - 1), 0))
    spec_l = pl.BlockSpec((tm, width), lambda i: (jnp.maximum(i + first_tile - ctx_tiles, 0) + lat_first_tile, 0))
    return [spec_c, spec_l], [src_ctx, src_lat]


def _pick_source(c_ref, l_ref, ctx_tiles, first_tile):
    return jnp.where(pl.program_id(0) + first_tile < ctx_tiles, c_ref[...], l_ref[...])


def _inproj_kernel(xc_ref, xl_ref, g_ref, sc_ref, sh_ref, w_ref, wq_ref, cos_ref, sa_ref, sb_ref,
                   zr_ref, zp_ref, q_ref, kv_ref, *, ctx_tiles):
    x = _pick_source(xc_ref, xl_ref, ctx_tiles, 0)
    h = _rms_mod(x, g_ref[...], sc_ref[...], sh_ref[...]).astype(BF16)
    zr_ref[...] = _dot(h, w_ref[:, 0:RWKV_COLS])
    zp_ref[...] = _dot(h, w_ref[:, OFF_POOL:OFF_AQ])
    cos, sa, sb = cos_ref[...], sa_ref[...], sb_ref[...]
    zq = _dot(h, wq_ref[...])
    q_ref[...] = (_rope(zq, cos, sa, sb) * (HEAD_DIM ** -0.5)).astype(BF16)
    zk = _dot(h, w_ref[:, OFF_AK:OFF_AV])
    zv = _dot(h, w_ref[:, OFF_AV:D_IN])
    kv_ref[...] = jnp.concatenate([_rope(zk, cos, sa, sb), zv], axis=1).astype(BF16)


def _in_projection(x_src, mod3, mod_base, norm_g, w_in_b, wq_b, layer, rope, dims):
    tm = dims["tm"]
    rows, d = dims["rows"], x_src[0].shape[1]
    ctx_tiles = dims["nc"] // tm
    x_specs, x_args = _two_source_specs(*x_src, tm, ctx_tiles, 0)
    grp = dims["grp"]
    cos, sin_a, sin_b = rope

    def mod_spec(k):
        return pl.BlockSpec((None, 1, d), lambda i: (mod_base + grp(i, tm) * 6 + k, 0, 0))

    row_spec = lambda w: pl.BlockSpec((tm, w), lambda i: (i, 0))
    return pl.pallas_call(
        functools.partial(_inproj_kernel, ctx_tiles=ctx_tiles),
        grid=(rows // tm,),
        in_specs=x_specs + [
            pl.BlockSpec((1, d), lambda i: (0, 0)),
            mod_spec(1), mod_spec(0),
            pl.BlockSpec((None, d, D_IN), lambda i: (layer, 0, 0), pipeline_mode=pl.Buffered(1)),
            pl.BlockSpec((None, d, D_ATT), lambda i: (layer, 0, 0), pipeline_mode=pl.Buffered(1)),
            row_spec(LANES), row_spec(LANES), row_spec(LANES),
        ],
        out_specs=[row_spec(RWKV_COLS), row_spec(D_POOL), row_spec(D_ATT), row_spec(2 * ATT_KV_DIM)],
        out_shape=[
            jax.ShapeDtypeStruct((rows, RWKV_COLS), F32),
            jax.ShapeDtypeStruct((rows, D_POOL), F32),
            jax.ShapeDtypeStruct((rows, D_ATT), BF16),
            jax.ShapeDtypeStruct((rows, 2 * ATT_KV_DIM), BF16),
        ],
        compiler_params=_cparams(("arbitrary",)),
        name="norm_inproj_rope",
    )(*x_args, norm_g.reshape(1, d), mod3, mod3, w_in_b, wq_b, cos, sin_a, sin_b)


def _seq_edges(i, tm, dims):
    nc, c_len, t_len = dims["nc"], dims["c_len"], dims["t_len"]
    r0 = i * tm
    is_ctx = r0 < nc
    seq_len = jnp.where(is_ctx, c_len, t_len)
    pos0 = jnp.where(is_ctx, lax.rem(r0, c_len), lax.rem(jnp.maximum(r0 - nc, 0), t_len))
    return pos0, seq_len, pos0 == 0, pos0 + tm == seq_len


def _prep_kernel(z_ref, zp_ref, zn_ref, mu_ref, w0_ref, wup_ref, a0_ref, aup_ref, gup_ref,
                 kk_ref, ka_ref, rk_ref, hs_ref,
                 r_ref, v_ref, av_ref, g_ref, bonus_ref, kd_ref, b_ref, lw_ref, *, tm, dims):
    i = pl.program_id(0)
    _, _, first, last = _seq_edges(i, tm, dims)
    z = z_ref[...]
    prev_row = jnp.where(first, 0.0, zp_ref[SUBLANES - 1:SUBLANES, :])
    next_row = jnp.where(last, 0.0, zn_ref[0:1, :])
    row = lax.broadcasted_iota(jnp.int32, (tm, 1), 0)
    z_m1 = jnp.where(row == 0, prev_row, pltpu.roll(z, 1, axis=0))
    z_p1 = jnp.where(row == tm - 1, next_row, pltpu.roll(z, tm - 1, axis=0))
    zs = z + mu_ref[...] * (0.5 * (z_m1 + z_p1) - z)

    r = zs[:, 0:OFF_K]
    k = zs[:, OFF_K:OFF_V]
    v = zs[:, OFF_V:OFF_WD]
    wd = zs[:, OFF_WD:OFF_AD]
    ad = zs[:, OFF_AD:OFF_GD]
    gd = zs[:, OFF_GD:RWKV_COLS]

    w_pre = w0_ref[...] + _dot(jnp.tanh(wd).astype(BF16), wup_ref[...])
    lw = -float(np.exp(-0.5)) * jax.nn.sigmoid(w_pre)
    a = jax.nn.sigmoid(a0_ref[...] + _dot(ad.astype(BF16), aup_ref[...]))
    g_ref[...] = _dot(jax.nn.sigmoid(gd).astype(BF16), gup_ref[...])

    hs = hs_ref[...]
    kx = k * kk_ref[...]
    kn = kx / jnp.maximum(jnp.sqrt(_head_sum(kx * kx, hs)), 1e-12)
    rk = r * rk_ref[...]
    ka = ka_ref[...]
    dot_rk = None
    for d in range(2):
        a_d = a[:, D_RWKV * d:D_RWKV * (d + 1)]
        kd = k * (1.0 + (a_d - 1.0) * ka)
        kd_ref[d] = kd.astype(BF16)
        b_ref[d] = (kn * a_d).astype(BF16)
        lw_ref[d] = lw[:, D_RWKV * d:D_RWKV * (d + 1)]
        dot_rk = rk * kd if dot_rk is None else dot_rk + rk * kd
    r_ref[...] = r.astype(BF16)
    v_ref[...] = v.astype(BF16)
    av_ref[...] = (-kn).astype(BF16)
    bonus_ref[...] = _head_sum(dot_rk, hs) * v


def _rwkv_prepare(z_r, p, hs, dims):
    rows = z_r.shape[0]
    tm = dims["tm"]
    nblk8 = rows // SUBLANES
    per = tm // SUBLANES
    full = lambda shape: pl.BlockSpec(shape, lambda i: (0,) * len(shape))
    row_spec = pl.BlockSpec((tm, D_RWKV), lambda i: (i, 0))
    dir_spec = pl.BlockSpec((2, tm, D_RWKV), lambda i: (0, i, 0))
    one = lambda dt: jax.ShapeDtypeStruct((rows, D_RWKV), dt)
    two = lambda dt: jax.ShapeDtypeStruct((2, rows, D_RWKV), dt)
    return pl.pallas_call(
        functools.partial(_prep_kernel, tm=tm, dims=dims),
        grid=(rows // tm,),
        in_specs=[
            pl.BlockSpec((tm, RWKV_COLS), lambda i: (i, 0)),
            pl.BlockSpec((SUBLANES, RWKV_COLS), lambda i: (jnp.maximum(i * per - 1, 0), 0)),
            pl.BlockSpec((SUBLANES, RWKV_COLS), lambda i: (jnp.minimum((i + 1) * per, nblk8 - 1), 0)),
            full((1, RWKV_COLS)),
            full((1, 2 * D_RWKV)), full((2 * LORA_RANK, 2 * D_RWKV)),
            full((1, 2 * D_RWKV)), full((2 * LORA_RANK, 2 * D_RWKV)),
            full((GATE_RANK, D_RWKV)),
            full((1, D_RWKV)), full((1, D_RWKV)), full((1, D_RWKV)),
            full((MXU_DIM, MXU_DIM)),
        ],
        out_specs=[row_spec, row_spec, row_spec, row_spec, row_spec, dir_spec, dir_spec, dir_spec],
        out_shape=[one(BF16), one(BF16), one(BF16), one(F32), one(F32), two(BF16), two(BF16), two(F32)],
        compiler_params=_cparams(("arbitrary",)),
        name="rwkv_prepare",
    )(z_r, z_r, z_r, p["mu"], p["w0"], p["w_up"], p["a0"], p["a_up"], p["g_up"],
      p["k_k"], p["k_a"], p["r_k"], hs)


def _take_rows(x, blk, phase):
    return jnp.concatenate([x[s:s + blk] for s in range(phase * blk, x.shape[0], 2 * blk)], axis=0)


def _add_rows(full, part, blk, phase):
    pieces = []
    for i, s in enumerate(range(0, full.shape[0], blk)):
        piece = full[s:s + blk]
        if i % 2 == phase:
            piece = piece + part[(i // 2) * blk:(i // 2 + 1) * blk]
        pieces.append(piece)
    return jnp.concatenate(pieces, axis=0)


def _scatter_rows(part, blk, phase):
    zero = jnp.zeros((blk, part.shape[1]), part.dtype)
    pieces = [part[(i // 2) * blk:(i // 2 + 1) * blk] if i % 2 == phase else zero
              for i in range(2 * part.shape[0] // blk)]
    return jnp.concatenate(pieces, axis=0)


def _block_rows_sum(x):
    parts = [x[s:s + INV_BASE] for s in range(0, x.shape[0], INV_BASE)]
    while len(parts) > 1:
        parts = [parts[i] + parts[i + 1] for i in range(0, len(parts), 2)]
    return parts[0]


def _group_bcast(a, m, lane0):
    halves = []
    for h in range(a.shape[1] // LANES):
        z = a[:, LANES * h:LANES * (h + 1)]
        if m:
            z = pltpu.roll(z, LANES - m, axis=1)
        z = z * lane0
        for shift in (1, 2, 4):
            z = z + pltpu.roll(z, shift, axis=1)
        halves.append(z)
    return jnp.concatenate(halves, axis=1)


def _scan_unit(r, v, av, kd, bd, e_c, e_cx, e_nc, e_rt, e_tot, masks, st_ref, d, g):
    m_strict, mt_strict, mt_incl, bm, bm_b, inv_masks, m8, m8_strict, m8_sel, eye_c, lane0 = masks
    sl = slice(MXU_DIM * g, MXU_DIM * (g + 1))
    phase = 1 - d

    def expand(xb):
        return jnp.tile(xb, (HEADS_PER_GROUP, 1)) * bm_b

    def collapse(x):
        out = x[0:CHUNK]
        for h in range(1, HEADS_PER_GROUP):
            out = out + x[CHUNK * h:CHUNK * (h + 1)]
        return out

    r_t = (r[:, sl] * e_c[:, sl]).astype(BF16)
    a_t = (av[:, sl] * e_cx[:, sl]).astype(BF16)
    k_t = (kd[:, sl] * e_nc[:, sl]).astype(BF16)
    b_t = (bd[:, sl] * e_nc[:, sl]).astype(BF16)
    k_h = (kd[:, sl] * e_rt[:, sl]).astype(BF16)
    b_h = (bd[:, sl] * e_rt[:, sl]).astype(BF16)
    v_b = v[:, sl].astype(BF16)
    ea, er, eb = expand(a_t), expand(r_t), expand(b_t)
    yield

    g_f = _dot_nt(ea, eb)
    yield
    g_ab = g_f.astype(BF16) * m_strict
    d_c = _block_rows_sum(g_f * m8_strict)
    a_ak = _dot_nt(ea, k_t).astype(BF16) * mt_strict
    yield
    d_cols = [(m, _group_bcast(d_c, m, lane0)) for m in range(d, INV_BASE - 1 + d)]
    a_r = _dot_nt(er, jnp.concatenate([b_t, k_t], axis=0)).astype(BF16) * mt_incl
    yield

    def horner(x):
        out = eye_c
        for m, col in d_cols:
            out = out + col * jnp.broadcast_to(x[m:m + 1, :], x.shape)
        return out

    t8_c = eye_c + d_c
    for _ in range((INV_BASE - 2) // 2):
        t8_c = horner(t8_c)
    x1 = collapse(_dot(a_ak, v_b) * bm)
    yield
    for _ in range(INV_BASE - 2 - (INV_BASE - 2) // 2):
        t8_c = horner(t8_c)
    s0 = st_ref[d, g]
    s0_b = s0.astype(BF16)
    rhs = _dot_nt(a_t, s0_b) + x1
    y_st = _dot_nt(r_t, s0_b)
    yield

    t8 = jnp.tile(t8_c, (MXU_DIM // INV_BASE, 1)) * m8
    inv = t8.astype(BF16)
    t8_sel = (jnp.tile(t8_c, (MXU_DIM // (2 * INV_BASE), 1)) * m8_sel).astype(BF16)
    t_off = _dot(t8_sel, g_ab * inv_masks[0]).astype(BF16)
    yield
    inv = _add_rows(t8, _dot(t_off, inv), INV_BASE, phase).astype(BF16)
    yield
    blk = 2 * INV_BASE
    t_off = _dot(_take_rows(inv, blk, phase), g_ab * inv_masks[1]).astype(BF16)
    yield
    inv = _add_rows(inv, _dot(t_off, inv).astype(BF16), blk, phase)
    yield
    blk = 4 * INV_BASE
    u32 = _dot(inv, expand(rhs.astype(BF16)))
    yield
    o_u = _dot(_take_rows(g_ab * inv_masks[2], blk, phase), u32.astype(BF16)).astype(BF16)
    yield
    e_u = _add_rows(u32, _dot(_take_rows(inv, blk, phase), _scatter_rows(o_u, blk, phase)), blk, phase)
    yield
    u_c = collapse(e_u).astype(BF16)
    y = y_st + collapse(_dot(a_r, jnp.concatenate([u_c, v_b], axis=0)) * bm)
    yield
    upd = _dot_tn(jnp.concatenate([u_c, v_b], axis=0), jnp.concatenate([b_h, k_h], axis=0))
    st_ref[d, g] = s0 * e_tot[:, sl] + upd * bm
    return y


def _run_interleaved(gens):
    results = [None] * len(gens)
    active = list(range(len(gens)))
    while active:
        for i in list(active):
            try:
                next(gens[i])
            except StopIteration as stop:
                results[i] = stop.value
                active.remove(i)
    return results


N_SCAN_CONSTS = 11
N_SCAN_STREAMS = 6


def _scan_kernel(*refs, n_b):
    n_in = n_b * 2 * N_SCAN_STREAMS
    streams, consts = refs[:n_in], refs[n_in:n_in + N_SCAN_CONSTS]
    y_refs, st_ref = refs[n_in + N_SCAN_CONSTS:-1], refs[-1]
    tri_ref, ms_ref, mts_ref, mti_ref, bm_ref, im_ref, m8_ref, m8s_ref, m8sel_ref, eyec_ref, lane0_ref = consts

    @pl.when(pl.program_id(0) == 0)
    def _():
        st_ref[...] = jnp.zeros(st_ref.shape, F32)

    bm = bm_ref[...]
    bm_b = bm.astype(BF16)
    inv_masks = [im_ref[lvl] for lvl in range(N_INV_MASKS)]
    m8, eye_c, lane0 = m8_ref[...], eyec_ref[...], lane0_ref[...]
    units = []
    for bi in range(n_b):
        for d in range(2):
            base = (bi * 2 + d) * N_SCAN_STREAMS
            r_ref, v_ref, av_ref, kd_ref, b_ref, lw_ref = streams[base:base + N_SCAN_STREAMS]
            lw = lw_ref[...]
            c = _dot_exact_lhs(tri_ref[d], lw)
            c_tot = jnp.sum(lw, axis=0, keepdims=True)
            e_c, e_cx, e_nc = jnp.exp(c), jnp.exp(c - lw), jnp.exp(-c)
            e_rt, e_tot = jnp.exp(c_tot - c), jnp.exp(c_tot)
            masks = (ms_ref[d], mts_ref[d], mti_ref[d], bm, bm_b, inv_masks, m8, m8s_ref[d], m8sel_ref[1 - d],
                     eye_c, lane0)
            r, v, av, kd, bd = r_ref[...], v_ref[...], av_ref[...], kd_ref[...], b_ref[...]
            units += [_scan_unit(r, v, av, kd, bd, e_c, e_cx, e_nc, e_rt, e_tot, masks, st_ref.at[bi], d, g)
                      for g in range(N_GROUPS)]
    ys = _run_interleaved(units)
    for bi in range(n_b):
        for d in range(2):
            first = (bi * 2 + d) * N_GROUPS
            y_refs[d][bi] = jnp.concatenate(ys[first:first + N_GROUPS], axis=1)


def _scan_constants():
    i = np.arange(MXU_DIM)
    blk = (i[:, None] // CHUNK) == (i[None, :] // CHUNK)
    t, s = i[:, None] % CHUNK, i[None, :] % CHUNK
    strict = np.stack([blk & (t > s), blk & (t < s)])
    j = np.arange(CHUNK)
    tall_strict = np.stack([t > j[None, :], t < j[None, :]])
    j2 = np.tile(j, 2)
    tall_incl = np.stack([t >= j2[None, :], t <= j2[None, :]])
    tri = np.stack([j[:, None] >= j[None, :], j[:, None] <= j[None, :]])
    inv_masks = []
    n = INV_BASE
    while n < CHUNK:
        inv_masks.append(((i[:, None] // (2 * n)) == (i[None, :] // (2 * n))) & ((i[:, None] // n) != (i[None, :] // n)))
        n *= 2
    assert len(inv_masks) == N_INV_MASKS
    m8 = (i[:, None] // INV_BASE) == (i[None, :] // INV_BASE)
    m8_strict = m8[None] & strict
    row_blk = np.arange(MXU_DIM // 2) // INV_BASE
    m8_sel = np.stack([m8[(2 * row_blk + p) * INV_BASE + np.arange(MXU_DIM // 2) % INV_BASE] for p in (0, 1)])
    eye_c = np.arange(INV_BASE)[:, None] == (i[None, :] % INV_BASE)
    lane0 = np.broadcast_to((np.arange(LANES) % INV_BASE) == 0, (INV_BASE, LANES))
    f32 = lambda a: jnp.asarray(a, F32)
    b16 = lambda a: jnp.asarray(a, BF16)
    return (b16(tri), b16(strict), b16(tall_strict), b16(tall_incl), f32(blk), b16(np.stack(inv_masks)),
            f32(m8), f32(m8_strict), f32(m8_sel), f32(eye_c), f32(lane0))


def _rwkv_scan(r, v, av, kd, b, lw, dims):
    rows = r.shape[0]
    n_b, ncc, nct = dims["batch"], dims["c_len"] // CHUNK, dims["t_len"] // CHUNK
    consts = _scan_constants()

    assert len(consts) == N_SCAN_CONSTS

    def chunk(s, d):
        j_ctx = s if d == 0 else ncc - 1 - s
        j_lat = s - ncc if d == 0 else nct - 1 - (s - ncc)
        return j_ctx, j_lat

    def flat_chunk(bi, d):
        def index(s):
            j_ctx, j_lat = chunk(s, d)
            return jnp.where(s < ncc, bi * ncc + j_ctx, n_b * ncc + bi * nct + j_lat)
        return index

    in_specs, args = [], []
    for bi in range(n_b):
        for d in range(2):
            idx = flat_chunk(bi, d)
            shared = pl.BlockSpec((CHUNK, D_RWKV), lambda s, idx=idx: (idx(s), 0))
            per_dir = pl.BlockSpec((None, CHUNK, D_RWKV), lambda s, idx=idx, d=d: (d, idx(s), 0))
            in_specs += [shared] * 3 + [per_dir] * 3
            args += [r, v, av, kd, b, lw]
    in_specs += [pl.BlockSpec(a.shape, lambda s, nd=a.ndim: (0,) * nd) for a in consts]

    def y_spec(d):
        def index(s):
            j_ctx, j_lat = chunk(s, d)
            return (0, jnp.where(s < ncc, j_ctx, ncc + j_lat), 0)
        return pl.BlockSpec((n_b, CHUNK, D_RWKV), index)

    y_shape = jax.ShapeDtypeStruct((n_b, rows // n_b, D_RWKV), F32)
    return pl.pallas_call(
        functools.partial(_scan_kernel, n_b=n_b),
        grid=(ncc + nct,),
        in_specs=in_specs,
        out_specs=[y_spec(0), y_spec(1)],
        out_shape=[y_shape, y_shape],
        scratch_shapes=[pltpu.VMEM((n_b, 2, N_GROUPS, MXU_DIM, MXU_DIM), F32)],
        compiler_params=_cparams(("arbitrary",)),
        name="rwkv_chunk_scan",
    )(*args, *consts)


def _pool_kernel(z_ref, zp_ref, zn_ref, w_ref, sc_ref, o_ref, *, tm, dims):
    i = pl.program_id(0)
    pos0, seq_len, first, last = _seq_edges(i, tm, dims)
    prev = jnp.where(first, 0.0, zp_ref[...])
    nxt = jnp.where(last, 0.0, zn_ref[...])
    ext = jnp.concatenate([prev, z_ref[...], nxt], axis=0)
    n = tm + 2 * POOL_HALO
    t = pos0 + lax.broadcasted_iota(jnp.int32, (tm, 1), 0)
    outs = []
    for gi, w in enumerate(POOL_WINDOWS):
        e = ext[:, POOL_GROUP_DIM * gi:POOL_GROUP_DIM * (gi + 1)]
        acc, width = e, 1
        while width < w:
            acc = acc + pltpu.roll(acc, n - width, axis=0)
            width *= 2
        win = pltpu.roll(acc, w // 2, axis=0)[POOL_HALO:POOL_HALO + tm]
        lo = jnp.clip(t - w // 2, 0, seq_len)
        hi = jnp.clip(t - w // 2 + w, 0, seq_len)
        y = win / (hi - lo).astype(F32) - e[POOL_HALO:POOL_HALO + tm]
        outs.append(_dot(y.astype(BF16), w_ref[gi].astype(BF16)))
    o_ref[...] = (jnp.concatenate(outs, axis=1) * sc_ref[...]).astype(BF16)


def _pool(z_p, pool_w, pool_scale, dims):
    rows = z_p.shape[0]
    tm = dims["tm"]
    nblk8 = rows // SUBLANES
    per = tm // SUBLANES
    return pl.pallas_call(
        functools.partial(_pool_kernel, tm=tm, dims=dims),
        grid=(rows // tm,),
        in_specs=[
            pl.BlockSpec((tm, D_POOL), lambda i: (i, 0)),
            pl.BlockSpec((SUBLANES, D_POOL), lambda i: (jnp.maximum(i * per - 1, 0), 0)),
            pl.BlockSpec((SUBLANES, D_POOL), lambda i: (jnp.minimum((i + 1) * per, nblk8 - 1), 0)),
            pl.BlockSpec(pool_w.shape, lambda i: (0, 0, 0)),
            pl.BlockSpec((1, D_POOL), lambda i: (0, 0)),
        ],
        out_specs=pl.BlockSpec((tm, D_POOL), lambda i: (i, 0)),
        out_shape=jax.ShapeDtypeStruct((rows, D_POOL), BF16),
        compiler_params=_cparams(("arbitrary",)),
        name="multiscale_pool",
    )(z_p, z_p, z_p, pool_w, pool_scale.reshape(1, D_POOL))


def _attn_kernel(sink_ref, q_ref, *rest, windowed, nblk):
    if windowed:
        kp_ref, kc_ref, kn_ref, kx_ref, o_ref = rest
        kv = jnp.concatenate([kp_ref[...], kc_ref[...], kn_ref[...], kx_ref[...]], axis=0)
    else:
        kx_ref, o_ref = rest
        kv = kx_ref[...]
    n = pl.program_id(1)
    q = q_ref[...]
    k_all, v_all = kv[:, 0:ATT_KV_DIM], kv[:, ATT_KV_DIM:2 * ATT_KV_DIM]
    nk = kv.shape[0]
    rows = ATT_GROUP * ATT_BLOCK
    if windowed:
        qi = lax.broadcasted_iota(jnp.int32, (ATT_BLOCK, nk), 0)
        kj = lax.broadcasted_iota(jnp.int32, (ATT_BLOCK, nk), 1)
        rel = kj - ATT_BLOCK - qi
        k_lo = jnp.where(n == 0, ATT_BLOCK, 0)
        k_hi = jnp.where(n == nblk - 1, 2 * ATT_BLOCK, 3 * ATT_BLOCK)
        bad = jnp.where(rel > ATT_BLOCK, 1, 0) + jnp.where(rel < -ATT_BLOCK, 1, 0) \
            + jnp.where(kj < k_lo, 1, 0) + jnp.where(kj >= k_hi, 1, 0)
        bad = jnp.where(kj >= 3 * ATT_BLOCK, 0, bad)
        neg = jnp.tile(jnp.where(bad > 0, -1e30, 0.0).astype(F32), (ATT_GROUP, 1))
    lane = lax.shift_right_logical(lax.broadcasted_iota(jnp.int32, (1, ATT_KV_DIM), 1), HEAD_DIM.bit_length() - 1)
    grow = lax.shift_right_logical(lax.broadcasted_iota(jnp.int32, (rows, 1), 0), ATT_BLOCK.bit_length() - 1)
    def kv_head(hk):
        lm_f = (lane == hk).astype(F32)
        lm_b = lm_f.astype(BF16)
        lhs = jnp.concatenate([q[:, ATT_KV_DIM * g:ATT_KV_DIM * (g + 1)] * lm_b for g in range(ATT_GROUP)], axis=0)
        s = _dot_nt(lhs, k_all)
        yield
        if windowed:
            s = s + neg
        sk = jnp.where(grow == 0, sink_ref[ATT_GROUP * hk],
                       jnp.where(grow == 1, sink_ref[ATT_GROUP * hk + 1], sink_ref[ATT_GROUP * hk + 2]))
        m = jnp.maximum(jnp.max(s, axis=1, keepdims=True), sk)
        p = jnp.exp(s - m)
        den = jnp.sum(p, axis=1, keepdims=True) + jnp.exp(sk - m)
        yield
        pv = _dot(p.astype(BF16), v_all) / den
        return [pv[ATT_BLOCK * g:ATT_BLOCK * (g + 1)] * lm_f for g in range(ATT_GROUP)]

    heads = _run_interleaved([kv_head(hk) for hk in range(ATT_KV_HEADS)])
    outs = [sum((head[g] for head in heads[1:]), heads[0][g]) for g in range(ATT_GROUP)]
    o_ref[...] = jnp.concatenate(outs, axis=1).astype(BF16)


def _attention(q, kv, sink, dims, windowed):
    rows = q.shape[0]
    n_b, c_len, t_len, nc = dims["batch"], dims["c_len"], dims["t_len"], dims["nc"]
    seq = t_len if windowed else c_len
    nblk = seq // ATT_BLOCK
    base = (nc // ATT_BLOCK) if windowed else 0
    qrow = lambda bi, n: (base + bi * nblk + n, 0)
    ctx_spec = pl.BlockSpec((c_len, 2 * ATT_KV_DIM), lambda bi, n: (bi, 0))
    blk = lambda f: pl.BlockSpec((ATT_BLOCK, 2 * ATT_KV_DIM), f)
    in_specs = [pl.BlockSpec(memory_space=pltpu.SMEM), pl.BlockSpec((ATT_BLOCK, D_ATT), qrow)]
    args = [sink, q]
    if windowed:
        in_specs += [
            blk(lambda bi, n: (base + bi * nblk + jnp.maximum(n - 1, 0), 0)),
            blk(qrow),
            blk(lambda bi, n: (base + bi * nblk + jnp.minimum(n + 1, nblk - 1), 0)),
        ]
        args += [kv, kv, kv]
    in_specs.append(ctx_spec)
    args.append(kv)
    out_rows = n_b * seq
    return pl.pallas_call(
        functools.partial(_attn_kernel, windowed=windowed, nblk=nblk),
        grid=(n_b, nblk),
        in_specs=in_specs,
        out_specs=pl.BlockSpec((ATT_BLOCK, D_ATT), lambda bi, n: (bi * nblk + n, 0)),
        out_shape=jax.ShapeDtypeStruct((out_rows, D_ATT), BF16),
        compiler_params=_cparams(("arbitrary", "arbitrary")),
        name="window_attention" if windowed else "context_attention",
    )(*args)


def _outproj_kernel(yf_ref, yb_ref, bonus_ref, g_ref, yp_ref, yac_ref, yal_ref, xc_ref, xl_ref, w_ref, wa_ref,
                    lng_ref, lnb_ref, hs_ref, gt_ref, ng_ref, sc_ref, sh_ref, xo_ref, ho_ref, *, ctx_tiles, first_tile):
    hs = hs_ref[...]
    y = yf_ref[...] + yb_ref[...]
    inv_n = 1.0 / HEAD_DIM
    dev = y - _head_sum(y, hs) * inv_n
    var = _head_sum(dev * dev, hs) * inv_n
    yn = dev * lax.rsqrt(var + RWKV_GN_EPS) * lng_ref[...] + lnb_ref[...]
    y_r = ((yn + bonus_ref[...]) * g_ref[...]).astype(BF16)
    acc = (_dot(y_r, w_ref[0:D_RWKV])
           + _dot(yp_ref[...], w_ref[D_RWKV:D_RWKV + D_POOL])
           + _dot(_pick_source(yac_ref, yal_ref, ctx_tiles, first_tile), wa_ref[...]))
    x_new = _pick_source(xc_ref, xl_ref, ctx_tiles, first_tile) + gt_ref[...] * acc
    xo_ref[...] = x_new
    ho_ref[...] = _rms_mod(x_new, ng_ref[...], sc_ref[...], sh_ref[...]).astype(BF16)


def _out_projection(yf, yb, bonus, g, y_p, ya_src, x_src, w_out_b, wa_b, layer, ln_g, ln_b, hs, mod3, mod_base, norm_g,
                    dims, first_tile):
    tm = dims["tm"]
    rows, d = dims["rows"], x_src[0].shape[1]
    ctx_tiles = dims["nc"] // tm
    ya_specs, ya_args = _two_source_specs(*ya_src, tm, ctx_tiles, first_tile)
    x_specs, x_args = _two_source_specs(*x_src, tm, ctx_tiles, first_tile)
    grp = dims["grp"]
    n_tiles = rows // tm - first_tile
    off = lambda w: pl.BlockSpec((tm, w), lambda i: (i + first_tile, 0))

    def mod_spec(k):
        return pl.BlockSpec((None, 1, d), lambda i: (mod_base + grp(i + first_tile, tm) * 6 + k, 0, 0))

    def y_index(i):
        ft = i + first_tile
        ctx_tiles, per_ctx, per_lat = dims["nc"] // tm, dims["c_len"] // tm, dims["t_len"] // tm
        lt = jnp.maximum(ft - ctx_tiles, 0)
        in_ctx = ft < ctx_tiles
        return (jnp.where(in_ctx, ft // per_ctx, lt // per_lat),
                jnp.where(in_ctx, ft % per_ctx, per_ctx + lt % per_lat), 0)

    y_spec = pl.BlockSpec((None, tm, D_RWKV), y_index)
    vec = lambda w: pl.BlockSpec((1, w), lambda i: (0, 0))
    out_rows = n_tiles * tm
    return pl.pallas_call(
        functools.partial(_outproj_kernel, ctx_tiles=ctx_tiles, first_tile=first_tile),
        grid=(n_tiles,),
        in_specs=[y_spec, y_spec, off(D_RWKV), off(D_RWKV), off(D_POOL)] + ya_specs + x_specs + [
            pl.BlockSpec((None, d, d), lambda i: (layer, 0, 0), pipeline_mode=pl.Buffered(1)),
            pl.BlockSpec((None, D_ATT, d), lambda i: (layer, 0, 0), pipeline_mode=pl.Buffered(1)),
            vec(D_RWKV), vec(D_RWKV),
            pl.BlockSpec((MXU_DIM, MXU_DIM), lambda i: (0, 0)),
            mod_spec(2), vec(d), mod_spec(4), mod_spec(3),
        ],
        out_specs=[pl.BlockSpec((tm, d), lambda i: (i, 0)), pl.BlockSpec((tm, d), lambda i: (i, 0))],
        out_shape=[jax.ShapeDtypeStruct((out_rows, d), F32), jax.ShapeDtypeStruct((out_rows, d), BF16)],
        compiler_params=_cparams(("arbitrary",)),
        name="readout_outproj_norm",
    )(yf, yb, bonus, g, y_p, *ya_args, *x_args, w_out_b, wa_b, ln_g.reshape(1, -1), ln_b.reshape(1, -1), hs,
      mod3, norm_g.reshape(1, d), mod3, mod3)


def _ffn_kernel(h_ref, x_ref, wg_ref, wu_ref, wd_ref, gt_ref, fg_ref, o_ref, acc_ref, *, final):
    j = pl.program_id(1)

    @pl.when(j == 0)
    def _():
        acc_ref[...] = jnp.zeros(acc_ref.shape, F32)

    h = h_ref[...]
    n_sub = wg_ref.shape[1] // MXU_DIM
    slabs = [slice(MXU_DIM * q, MXU_DIM * (q + 1)) for q in range(n_sub)]
    gate_up = [(_dot(h, wg_ref[:, sl]), _dot(h, wu_ref[:, sl])) for sl in slabs]
    part = None
    for sl, (gate, up) in zip(slabs, gate_up):
        act = (gate * jax.nn.sigmoid(gate) * up).astype(BF16)
        p = _dot(act, wd_ref[sl, :])
        part = p if part is None else part + p
    acc_ref[...] += part

    @pl.when(j == pl.num_programs(1) - 1)
    def _():
        x_new = x_ref[...] + gt_ref[...] * acc_ref[...]
        if final:
            ms = jnp.mean(x_new * x_new, axis=-1, keepdims=True)
            x_new = x_new * lax.rsqrt(ms + NORM_EPS) * fg_ref[...]
        o_ref[...] = x_new


def _ffn(h2, x_new, w_gu_b, w_down_b, layer, mod3, mod_base, final_g, dims, first_tile, final):
    rows, d = x_new.shape
    tmf, th = dims["tmf"], dims["th"]
    hidden = w_down_b.shape[1]
    nh = hidden // th
    grp = dims["grp"]
    return pl.pallas_call(
        functools.partial(_ffn_kernel, final=final),
        grid=(rows // tmf, nh),
        in_specs=[
            pl.BlockSpec((tmf, d), lambda i, j: (i, 0)),
            pl.BlockSpec((tmf, d), lambda i, j: (i, 0)),
            pl.BlockSpec((None, d, th), lambda i, j: (layer, 0, j)),
            pl.BlockSpec((None, d, th), lambda i, j: (layer, 0, nh + j)),
            pl.BlockSpec((None, th, d), lambda i, j: (layer, j, 0)),
            pl.BlockSpec((None, 1, d), lambda i, j: (mod_base + grp(i + first_tile, tmf) * 6 + 5, 0, 0)),
            pl.BlockSpec((1, d), lambda i, j: (0, 0)),
        ],
        out_specs=pl.BlockSpec((tmf, d), lambda i, j: (i, 0)),
        out_shape=jax.ShapeDtypeStruct((rows, d), F32),
        scratch_shapes=[pltpu.VMEM((tmf, d), F32)],
        compiler_params=_cparams(("arbitrary", "arbitrary")),
        name="swiglu_ffn_final" if final else "swiglu_ffn",
    )(h2, x_new, w_gu_b, w_gu_b, w_down_b, mod3, final_g.reshape(1, d))


def _rope_tables(dims):
    t_len, nc, n_b = dims["t_len"], dims["nc"], dims["batch"]
    grid_rows = t_len // GRID_W
    n_freq = HEAD_DIM // 4
    inv = ROPE_BASE ** (-jnp.arange(n_freq, dtype=F32) / n_freq)
    ang_r = jnp.arange(grid_rows).astype(F32)[:, None] * inv
    ang_c = jnp.arange(GRID_W).astype(F32)[:, None] * inv

    def per_token(fn):
        by_row = jnp.repeat(fn(ang_r), GRID_W, axis=0)
        by_col = jnp.tile(fn(ang_c), (grid_rows, 1))
        return jnp.tile(jnp.concatenate([by_row, by_row, by_col, by_col], axis=-1), (1, LANES // HEAD_DIM))

    cos, sin = per_token(jnp.cos), per_token(jnp.sin)
    first_half = (jnp.arange(LANES) % 32) < 16
    sin_a = jnp.where(first_half, -sin, 0.0)
    sin_b = jnp.where(first_half, 0.0, sin)
    flat = lambda tab, fill: jnp.concatenate([jnp.full((nc, LANES), fill, F32), jnp.tile(tab, (n_b, 1))], axis=0)
    return flat(cos, 1.0), flat(sin_a, 0.0), flat(sin_b, 0.0)


def _block_diag2(w):
    z = jnp.zeros_like(w[0])
    return jnp.concatenate([jnp.concatenate([w[0], z], axis=1), jnp.concatenate([z, w[1]], axis=1)], axis=0)


def kernel(x, c, ctx, c_ctx, ada_w, ada_b, norm_mix_g, norm_ffn_g, w_in, rwkv_mu, rwkv_w0, rwkv_w_up, rwkv_a0,
           rwkv_a_up, rwkv_g_up, rwkv_k_k, rwkv_k_a, rwkv_r_k, rwkv_ln_g, rwkv_ln_b, pool_w, pool_scale,
           attn_sink, w_out, ffn_w_gu, ffn_w_down, final_norm_g):
    n_b, t_len, d = x.shape
    c_len = ctx.shape[1]
    depth = w_in.shape[0]
    nc = n_b * c_len
    rows = nc + n_b * t_len
    tm = 256 if c_len % 256 == 0 else c_len
    tmf = nc
    assert n_b + 1 <= SUBLANES and c_len % tm == 0 and t_len % tm == 0 and t_len % tmf == 0
    assert tm % ATT_BLOCK == 0 and c_len % CHUNK == 0 and t_len % GRID_W == 0 and d == w_out.shape[1]

    def grp(i, tile):
        r0 = i * tile
        return jnp.where(r0 < nc, 0, 1 + jnp.maximum(r0 - nc, 0) // t_len)

    hidden = ffn_w_down.shape[1]
    th = 512 if hidden % 512 == 0 else hidden
    dims = dict(batch=n_b, c_len=c_len, t_len=t_len, nc=nc, rows=rows, tm=tm, tmf=tmf, th=th, grp=grp)

    cvec = jnp.concatenate([c_ctx[None, :], c, jnp.zeros((SUBLANES - 1 - n_b, d), F32)], axis=0)
    mod = _modulation(cvec, ada_w, ada_b)
    mod3 = mod.reshape(depth * SUBLANES * 6, 1, d)

    x_src = (ctx.reshape(nc, d), x.reshape(n_b * t_len, d), 0)
    rope = _rope_tables(dims)
    hs = jnp.asarray(np.kron(np.eye(HEADS_PER_GROUP), np.ones((HEAD_DIM, HEAD_DIM))), BF16)

    w_in_b = w_in.astype(BF16)
    wq_b = w_in_b[:, :, OFF_AQ:OFF_AK].reshape(depth, d, ATT_KV_HEADS, ATT_GROUP, HEAD_DIM)
    wq_b = wq_b.transpose(0, 1, 3, 2, 4).reshape(depth, d, D_ATT)
    w_out_b = w_out.astype(BF16)
    wa_b = w_out_b[:, D_RWKV + D_POOL:].reshape(depth, ATT_KV_HEADS, ATT_GROUP, HEAD_DIM, d)
    wa_b = wa_b.transpose(0, 2, 1, 3, 4).reshape(depth, D_ATT, d)
    w_gu_b = ffn_w_gu.astype(BF16)
    w_down_b = ffn_w_down.astype(BF16)

    out = None
    for l in range(depth):
        last = l == depth - 1
        mod_base = l * SUBLANES * 6
        rw = dict(
            mu=rwkv_mu[l].reshape(1, RWKV_COLS),
            w0=rwkv_w0[l].reshape(1, 2 * D_RWKV),
            w_up=_block_diag2(rwkv_w_up[l]).astype(BF16),
            a0=rwkv_a0[l].reshape(1, 2 * D_RWKV),
            a_up=_block_diag2(rwkv_a_up[l]).astype(BF16),
            g_up=rwkv_g_up[l].astype(BF16),
            k_k=rwkv_k_k[l].reshape(1, D_RWKV),
            k_a=rwkv_k_a[l].reshape(1, D_RWKV),
            r_k=rwkv_r_k[l].reshape(1, D_RWKV),
        )

        z_r, z_p, q, kv = _in_projection(x_src, mod3, mod_base, norm_mix_g[l], w_in_b, wq_b, l, rope, dims)
        r, v, av, g, bonus, kd, b, lw = _rwkv_prepare(z_r, rw, hs, dims)
        yf, yb = _rwkv_scan(r, v, av, kd, b, lw, dims)
        y_p = _pool(z_p, pool_w[l], pool_scale[l], dims)
        ya_lat = _attention(q, kv, attn_sink[l], dims, windowed=True)
        ya_ctx = ya_lat if last else _attention(q, kv, attn_sink[l], dims, windowed=False)
        first_tile = nc // tm if last else 0
        x_new, h2 = _out_projection(yf, yb, bonus, g, y_p, (ya_ctx, ya_lat, 0), x_src, w_out_b, wa_b, l,
                                    rwkv_ln_g[l], rwkv_ln_b[l], hs, mod3, mod_base, norm_ffn_g[l], dims, first_tile)
        x_next = _ffn(h2, x_new, w_gu_b, w_down_b, l, mod3, mod_base, final_norm_g, dims,
                      first_tile * tm // tmf, final=last)
        if last:
            out = x_next.reshape(n_b, t_len, d)
        else:
            x_src = (x_next, x_next, nc // tm)
    return out
```

```python
import functools

import numpy as np
import jax
import jax.numpy as jnp
from jax import lax
from jax.experimental import pallas as pl
from jax.experimental.pallas import tpu as pltpu

F32 = jnp.float32
BF16 = jnp.bfloat16

HEAD_DIM = 64
RWKV_HEADS = 12
D_RWKV = RWKV_HEADS * HEAD_DIM
LORA_RANK = 64
GATE_RANK = 128
D_POOL = 512
POOL_WINDOWS = (2, 4, 8, 16)
POOL_GROUP_DIM = 128
POOL_HALO = 8
ATT_Q_HEADS = 12
ATT_KV_HEADS = 4
ATT_GROUP = ATT_Q_HEADS // ATT_KV_HEADS
D_ATT = ATT_Q_HEADS * HEAD_DIM
ATT_KV_DIM = ATT_KV_HEADS * HEAD_DIM
ATT_BLOCK = 128
GRID_W = 64
ROPE_BASE = 10000.0
NORM_EPS = 1e-6
RWKV_GN_EPS = 64e-5

OFF_K = D_RWKV
OFF_V = 2 * D_RWKV
OFF_WD = 3 * D_RWKV
OFF_AD = OFF_WD + 2 * LORA_RANK
OFF_GD = OFF_AD + 2 * LORA_RANK
RWKV_COLS = OFF_GD + GATE_RANK
OFF_POOL = RWKV_COLS
OFF_AQ = OFF_POOL + D_POOL
OFF_AK = OFF_AQ + D_ATT
OFF_AV = OFF_AK + ATT_KV_DIM
D_IN = OFF_AV + ATT_KV_DIM

LANES = 128
SUBLANES = 8
MXU_DIM = 256
HEADS_PER_GROUP = MXU_DIM // HEAD_DIM
N_GROUPS = D_RWKV // MXU_DIM
CHUNK = MXU_DIM // HEADS_PER_GROUP
INV_BASE = 8
N_INV_MASKS = (CHUNK // INV_BASE).bit_length() - 2
VMEM_LIMIT = 56 * 1024 * 1024


def _cparams(sem, vmem=VMEM_LIMIT):
    return pltpu.CompilerParams(dimension_semantics=sem, vmem_limit_bytes=vmem)


def _dot(a, b):
    return jnp.dot(a, b, preferred_element_type=F32)


def _dot_nt(a, b):
    return lax.dot_general(a, b, (((1,), (1,)), ((), ())), preferred_element_type=F32)


def _dot_tn(a, b):
    return lax.dot_general(a, b, (((0,), (0,)), ((), ())), preferred_element_type=F32)


def _split3(x):
    x1 = x.astype(BF16)
    r1 = x - x1.astype(F32)
    x2 = r1.astype(BF16)
    x3 = (r1 - x2.astype(F32)).astype(BF16)
    return x1, x2, x3


def _dot_exact_rhs(x, m_bf16):
    x1, x2, x3 = _split3(x)
    return _dot(x1, m_bf16) + _dot(x2, m_bf16) + _dot(x3, m_bf16)


def _dot_exact_lhs(m_bf16, x):
    x1, x2, x3 = _split3(x)
    return _dot(m_bf16, x1) + _dot(m_bf16, x2) + _dot(m_bf16, x3)


def _head_sum(x, hs):
    parts = [_dot_exact_rhs(x[:, MXU_DIM * g:MXU_DIM * (g + 1)], hs) for g in range(N_GROUPS)]
    return jnp.concatenate(parts, axis=1)


def _rms_mod(x, g, sc, sh):
    ms = jnp.mean(x * x, axis=-1, keepdims=True)
    return x * lax.rsqrt(ms + NORM_EPS) * g * (1.0 + sc) + sh


def _mod_kernel(c_ref, w_ref, b_ref, o_ref):
    c = c_ref[...]
    s = (c * jax.nn.sigmoid(c)).astype(BF16)
    o_ref[...] = _dot(s, w_ref[...].astype(BF16)) + b_ref[...]


def _modulation(cvec, ada_w, ada_b):
    n_layers, d, n6 = ada_w.shape
    tn = 1024
    return pl.pallas_call(
        _mod_kernel,
        grid=(n_layers, n6 // tn),
        in_specs=[
            pl.BlockSpec((SUBLANES, d), lambda l, j: (0, 0)),
            pl.BlockSpec((None, d, tn), lambda l, j: (l, 0, j)),
            pl.BlockSpec((None, 1, tn), lambda l, j: (l, 0, j)),
        ],
        out_specs=pl.BlockSpec((None, SUBLANES, tn), lambda l, j: (l, 0, j)),
        out_shape=jax.ShapeDtypeStruct((n_layers, SUBLANES, n6), F32),
        compiler_params=_cparams(("arbitrary", "arbitrary")),
        name="adaln_modulation",
    )(cvec, ada_w, ada_b.reshape(n_layers, 1, n6))


def _rope(z, cos, sin_a, sin_b):
    outs = []
    for j in range(z.shape[1] // LANES):
        zb = z[:, LANES * j:LANES * (j + 1)]
        outs.append(zb * cos + pltpu.roll(zb, LANES - 16, axis=1) * sin_a + pltpu.roll(zb, 16, axis=1) * sin_b)
    return jnp.concatenate(outs, axis=1)


def _two_source_specs(src_ctx, src_lat, lat_first_tile, tm, ctx_tiles, first_tile):
    width = src_ctx.shape[1]
    spec_c = pl.BlockSpec((tm, width), lambda i: (jnp.minimum(i + first_tile, ctx_tiles - 1), 0))
    spec_l = pl.BlockSpec((tm, width), lambda i: (jnp.maximum(i + first_tile - ctx_tiles, 0) + lat_first_tile, 0))
    return [spec_c, spec_l], [src_ctx, src_lat]


def _pick_source(c_ref, l_ref, ctx_tiles, first_tile):
    return jnp.where(pl.program_id(0) + first_tile < ctx_tiles, c_ref[...], l_ref[...])


def _inproj_kernel(xc_ref, xl_ref, g_ref, sc_ref, sh_ref, w_ref, wq_ref, cos_ref, sa_ref, sb_ref,
                   zr_ref, zp_ref, q_ref, kv_ref, *, ctx_tiles):
    x = _pick_source(xc_ref, xl_ref, ctx_tiles, 0)
    h = _rms_mod(x, g_ref[...], sc_ref[...], sh_ref[...]).astype(BF16)
    zr_ref[...] = _dot(h, w_ref[:, 0:RWKV_COLS])
    zp_ref[...] = _dot(h, w_ref[:, OFF_POOL:OFF_AQ])
    cos, sa, sb = cos_ref[...], sa_ref[...], sb_ref[...]
    zq = _dot(h, wq_ref[...])
    q_ref[...] = (_rope(zq, cos, sa, sb) * (HEAD_DIM ** -0.5)).astype(BF16)
    zk = _dot(h, w_ref[:, OFF_AK:OFF_AV])
    zv = _dot(h, w_ref[:, OFF_AV:D_IN])
    kv_ref[...] = jnp.concatenate([_rope(zk, cos, sa, sb), zv], axis=1).astype(BF16)


def _in_projection(x_src, mod3, mod_base, norm_g, w_in_b, wq_b, layer, rope, dims):
    tm = dims["tm"]
    rows, d = dims["rows"], x_src[0].shape[1]
    ctx_tiles = dims["nc"] // tm
    x_specs, x_args = _two_source_specs(*x_src, tm, ctx_tiles, 0)
    grp = dims["grp"]
    cos, sin_a, sin_b = rope

    def mod_spec(k):
        return pl.BlockSpec((None, 1, d), lambda i: (mod_base + grp(i, tm) * 6 + k, 0, 0))

    row_spec = lambda w: pl.BlockSpec((tm, w), lambda i: (i, 0))
    return pl.pallas_call(
        functools.partial(_inproj_kernel, ctx_tiles=ctx_tiles),
        grid=(rows // tm,),
        in_specs=x_specs + [
            pl.BlockSpec((1, d), lambda i: (0, 0)),
            mod_spec(1), mod_spec(0),
            pl.BlockSpec((None, d, D_IN), lambda i: (layer, 0, 0), pipeline_mode=pl.Buffered(1)),
            pl.BlockSpec((None, d, D_ATT), lambda i: (layer, 0, 0), pipeline_mode=pl.Buffered(1)),
            row_spec(LANES), row_spec(LANES), row_spec(LANES),
        ],
        out_specs=[row_spec(RWKV_COLS), row_spec(D_POOL), row_spec(D_ATT), row_spec(2 * ATT_KV_DIM)],
        out_shape=[
            jax.ShapeDtypeStruct((rows, RWKV_COLS), F32),
            jax.ShapeDtypeStruct((rows, D_POOL), F32),
            jax.ShapeDtypeStruct((rows, D_ATT), BF16),
            jax.ShapeDtypeStruct((rows, 2 * ATT_KV_DIM), BF16),
        ],
        compiler_params=_cparams(("arbitrary",)),
        name="norm_inproj_rope",
    )(*x_args, norm_g.reshape(1, d), mod3, mod3, w_in_b, wq_b, cos, sin_a, sin_b)


def _seq_edges(i, tm, dims):
    nc, c_len, t_len = dims["nc"], dims["c_len"], dims["t_len"]
    r0 = i * tm
    is_ctx = r0 < nc
    seq_len = jnp.where(is_ctx, c_len, t_len)
    pos0 = jnp.where(is_ctx, lax.rem(r0, c_len), lax.rem(jnp.maximum(r0 - nc, 0), t_len))
    return pos0, seq_len, pos0 == 0, pos0 + tm == seq_len


def _prep_kernel(z_ref, zp_ref, zn_ref, mu_ref, w0_ref, wup_ref, a0_ref, aup_ref, gup_ref,
                 kk_ref, ka_ref, rk_ref, hs_ref,
                 r_ref, v_ref, av_ref, g_ref, bonus_ref, kd_ref, b_ref, lw_ref, *, tm, dims):
    i = pl.program_id(0)
    _, _, first, last = _seq_edges(i, tm, dims)
    z = z_ref[...]
    prev_row = jnp.where(first, 0.0, zp_ref[SUBLANES - 1:SUBLANES, :])
    next_row = jnp.where(last, 0.0, zn_ref[0:1, :])
    row = lax.broadcasted_iota(jnp.int32, (tm, 1), 0)
    z_m1 = jnp.where(row == 0, prev_row, pltpu.roll(z, 1, axis=0))
    z_p1 = jnp.where(row == tm - 1, next_row, pltpu.roll(z, tm - 1, axis=0))
    zs = z + mu_ref[...] * (0.5 * (z_m1 + z_p1) - z)

    r = zs[:, 0:OFF_K]
    k = zs[:, OFF_K:OFF_V]
    v = zs[:, OFF_V:OFF_WD]
    wd = zs[:, OFF_WD:OFF_AD]
    ad = zs[:, OFF_AD:OFF_GD]
    gd = zs[:, OFF_GD:RWKV_COLS]

    w_pre = w0_ref[...] + _dot(jnp.tanh(wd).astype(BF16), wup_ref[...])
    lw = -float(np.exp(-0.5)) * jax.nn.sigmoid(w_pre)
    a = jax.nn.sigmoid(a0_ref[...] + _dot(ad.astype(BF16), aup_ref[...]))
    g_ref[...] = _dot(jax.nn.sigmoid(gd).astype(BF16), gup_ref[...])

    hs = hs_ref[...]
    kx = k * kk_ref[...]
    kn = kx / jnp.maximum(jnp.sqrt(_head_sum(kx * kx, hs)), 1e-12)
    rk = r * rk_ref[...]
    ka = ka_ref[...]
    dot_rk = None
    for d in range(2):
        a_d = a[:, D_RWKV * d:D_RWKV * (d + 1)]
        kd = k * (1.0 + (a_d - 1.0) * ka)
        kd_ref[d] = kd.astype(BF16)
        b_ref[d] = (kn * a_d).astype(BF16)
        lw_ref[d] = lw[:, D_RWKV * d:D_RWKV * (d + 1)]
        dot_rk = rk * kd if dot_rk is None else dot_rk + rk * kd
    r_ref[...] = r.astype(BF16)
    v_ref[...] = v.astype(BF16)
    av_ref[...] = (-kn).astype(BF16)
    bonus_ref[...] = _head_sum(dot_rk, hs) * v


def _rwkv_prepare(z_r, p, hs, dims):
    rows = z_r.shape[0]
    tm = dims["tm"]
    nblk8 = rows // SUBLANES
    per = tm // SUBLANES
    full = lambda shape: pl.BlockSpec(shape, lambda i: (0,) * len(shape))
    row_spec = pl.BlockSpec((tm, D_RWKV), lambda i: (i, 0))
    dir_spec = pl.BlockSpec((2, tm, D_RWKV), lambda i: (0, i, 0))
    one = lambda dt: jax.ShapeDtypeStruct((rows, D_RWKV), dt)
    two = lambda dt: jax.ShapeDtypeStruct((2, rows, D_RWKV), dt)
    return pl.pallas_call(
        functools.partial(_prep_kernel, tm=tm, dims=dims),
        grid=(rows // tm,),
        in_specs=[
            pl.BlockSpec((tm, RWKV_COLS), lambda i: (i, 0)),
            pl.BlockSpec((SUBLANES, RWKV_COLS), lambda i: (jnp.maximum(i * per - 1, 0), 0)),
            pl.BlockSpec((SUBLANES, RWKV_COLS), lambda i: (jnp.minimum((i + 1) * per, nblk8 - 1), 0)),
            full((1, RWKV_COLS)),
            full((1, 2 * D_RWKV)), full((2 * LORA_RANK, 2 * D_RWKV)),
            full((1, 2 * D_RWKV)), full((2 * LORA_RANK, 2 * D_RWKV)),
            full((GATE_RANK, D_RWKV)),
            full((1, D_RWKV)), full((1, D_RWKV)), full((1, D_RWKV)),
            full((MXU_DIM, MXU_DIM)),
        ],
        out_specs=[row_spec, row_spec, row_spec, row_spec, row_spec, dir_spec, dir_spec, dir_spec],
        out_shape=[one(BF16), one(BF16), one(BF16), one(F32), one(F32), two(BF16), two(BF16), two(F32)],
        compiler_params=_cparams(("arbitrary",)),
        name="rwkv_prepare",
    )(z_r, z_r, z_r, p["mu"], p["w0"], p["w_up"], p["a0"], p["a_up"], p["g_up"],
      p["k_k"], p["k_a"], p["r_k"], hs)


def _block_rows_sum(x):
    parts = [x[s:s + INV_BASE] for s in range(0, x.shape[0], INV_BASE)]
    while len(parts) > 1:
        parts = [parts[i] + parts[i + 1] for i in range(0, len(parts), 2)]
    return parts[0]


def _group_bcast(a, m, lane0):
    halves = []
    for h in range(a.shape[1] // LANES):
        z = a[:, LANES * h:LANES * (h + 1)]
        if m:
            z = pltpu.roll(z, LANES - m, axis=1)
        z = z * lane0
        for shift in (1, 2, 4):
            z = z + pltpu.roll(z, shift, axis=1)
        halves.append(z)
    return jnp.concatenate(halves, axis=1)


def _scan_unit(r, v, av, kd, bd, e_c, e_cx, e_nc, e_rt, e_tot, masks, st_ref, d, g):
    w_strict, w_incl, w8_strict, w8, w_off32, bm, bm_b, inv_masks, eye_c, lane0 = masks
    sl = slice(MXU_DIM * g, MXU_DIM * (g + 1))

    def expand(xb):
        return jnp.tile(xb, (HEADS_PER_GROUP, 1)) * bm_b

    r_t = (r[:, sl] * e_c[:, sl]).astype(BF16)
    a_t = (av[:, sl] * e_cx[:, sl]).astype(BF16)
    k_t = (kd[:, sl] * e_nc[:, sl]).astype(BF16)
    b_t = (bd[:, sl] * e_nc[:, sl]).astype(BF16)
    k_h = (kd[:, sl] * e_rt[:, sl]).astype(BF16)
    b_h = (bd[:, sl] * e_rt[:, sl]).astype(BF16)
    v_b = v[:, sl].astype(BF16)
    a_r = jnp.concatenate([a_t, r_t], axis=0)
    eb, ek, ev = expand(b_t), expand(k_t), expand(v_b)
    yield

    g_b = _dot_nt(a_r, eb)
    yield
    g_k = _dot_nt(a_r, ek)
    yield
    a_ab_f = g_b[0:CHUNK]
    a_ab = a_ab_f.astype(BF16) * w_strict
    a_rb = g_b[CHUNK:].astype(BF16) * w_incl
    a_ak = g_k[0:CHUNK].astype(BF16) * w_strict
    a_rk = g_k[CHUNK:].astype(BF16) * w_incl
    g_ab = expand(a_ab)
    d_c = _block_rows_sum(a_ab_f * w8_strict)
    d_cols = [(m, _group_bcast(d_c, m, lane0)) for m in range(d, INV_BASE - 1 + d)]
    from_v = _dot(jnp.concatenate([a_ak, a_rk], axis=0), ev)
    yield

    def horner(x):
        out = eye_c
        for m, col in d_cols:
            out = out + col * jnp.broadcast_to(x[m:m + 1, :], x.shape)
        return out

    t8_c = eye_c + d_c
    for _ in range((INV_BASE - 2) // 2):
        t8_c = horner(t8_c)
    s0 = st_ref[d, g]
    s0_b = s0.astype(BF16)
    from_state = _dot_nt(a_r, s0_b)
    yield
    for _ in range(INV_BASE - 2 - (INV_BASE - 2) // 2):
        t8_c = horner(t8_c)
    rhs = from_state[0:CHUNK] + from_v[0:CHUNK]
    y_known = from_state[CHUNK:] + from_v[CHUNK:]

    t32 = (jnp.tile(t8_c, (CHUNK // INV_BASE, 1)) * w8).astype(BF16)
    for lvl in range(N_INV_MASKS):
        t_off = _dot(t32, g_ab * inv_masks[lvl]).astype(BF16)
        yield
        t32 = t32 + _dot(t_off, expand(t32)).astype(BF16)
        yield
    u32 = _dot(t32, expand(rhs.astype(BF16)))
    yield
    o_u = _dot(a_ab * w_off32, expand(u32.astype(BF16)))
    yield
    u_c = (u32 + _dot(t32, expand(o_u.astype(BF16)))).astype(BF16)
    yield
    y = y_known + _dot(a_rb, expand(u_c))
    yield
    upd = _dot_tn(jnp.concatenate([u_c, v_b], axis=0), jnp.concatenate([b_h, k_h], axis=0))
    st_ref[d, g] = s0 * e_tot[:, sl] + upd * bm
    return y


def _run_interleaved(gens):
    results = [None] * len(gens)
    active = list(range(len(gens)))
    while active:
        for i in list(active):
            try:
                next(gens[i])
            except StopIteration as stop:
                results[i] = stop.value
                active.remove(i)
    return results


N_SCAN_CONSTS = 10
N_SCAN_STREAMS = 6


def _scan_kernel(*refs, n_b):
    n_in = n_b * 2 * N_SCAN_STREAMS
    streams, consts = refs[:n_in], refs[n_in:n_in + N_SCAN_CONSTS]
    y_refs, st_ref = refs[n_in + N_SCAN_CONSTS:-1], refs[-1]
    tri_ref, ws_ref, wi_ref, w8s_ref, w8_ref, woff_ref, bm_ref, im_ref, eyec_ref, lane0_ref = consts

    @pl.when(pl.program_id(0) == 0)
    def _():
        st_ref[...] = jnp.zeros(st_ref.shape, F32)

    bm = bm_ref[...]
    bm_b = bm.astype(BF16)
    inv_masks = [im_ref[lvl] for lvl in range(N_INV_MASKS)]
    w8, eye_c, lane0, w_off32 = w8_ref[...], eyec_ref[...], lane0_ref[...], woff_ref[...]
    units = []
    for bi in range(n_b):
        for d in range(2):
            base = (bi * 2 + d) * N_SCAN_STREAMS
            r_ref, v_ref, av_ref, kd_ref, b_ref, lw_ref = streams[base:base + N_SCAN_STREAMS]
            lw = lw_ref[...]
            c = _dot_exact_lhs(tri_ref[d], lw)
            c_tot = jnp.sum(lw, axis=0, keepdims=True)
            e_c, e_cx, e_nc = jnp.exp(c), jnp.exp(c - lw), jnp.exp(-c)
            e_rt, e_tot = jnp.exp(c_tot - c), jnp.exp(c_tot)
            masks = (ws_ref[d], wi_ref[d], w8s_ref[d], w8, w_off32, bm, bm_b, inv_masks, eye_c, lane0)
            r, v, av, kd, bd = r_ref[...], v_ref[...], av_ref[...], kd_ref[...], b_ref[...]
            units += [_scan_unit(r, v, av, kd, bd, e_c, e_cx, e_nc, e_rt, e_tot, masks, st_ref.at[bi], d, g)
                      for g in range(N_GROUPS)]
    ys = _run_interleaved(units)
    for bi in range(n_b):
        for d in range(2):
            first = (bi * 2 + d) * N_GROUPS
            y_refs[d][bi] = jnp.concatenate(ys[first:first + N_GROUPS], axis=1)


def _scan_constants():
    i = np.arange(MXU_DIM)
    blk = (i[:, None] // CHUNK) == (i[None, :] // CHUNK)
    j = np.arange(CHUNK)
    tri = np.stack([j[:, None] >= j[None, :], j[:, None] <= j[None, :]])
    t, s = j[:, None], i[None, :] % CHUNK
    wide_strict = np.stack([t > s, t < s])
    wide_incl = np.stack([t >= s, t <= s])
    wide8 = (t // INV_BASE) == (s // INV_BASE)
    wide8_strict = wide_strict & wide8[None]
    wide_off32 = (t // (CHUNK // 2)) != (s // (CHUNK // 2))
    inv_masks = []
    n = INV_BASE
    while 2 * n < CHUNK:
        inv_masks.append(((i[:, None] // (2 * n)) == (i[None, :] // (2 * n))) & ((i[:, None] // n) != (i[None, :] // n)))
        n *= 2
    assert len(inv_masks) == N_INV_MASKS
    eye_c = np.arange(INV_BASE)[:, None] == (i[None, :] % INV_BASE)
    lane0 = np.broadcast_to((np.arange(LANES) % INV_BASE) == 0, (INV_BASE, LANES))
    f32 = lambda a: jnp.asarray(a, F32)
    b16 = lambda a: jnp.asarray(a, BF16)
    return (b16(tri), b16(wide_strict), b16(wide_incl), f32(wide8_strict), f32(wide8), b16(wide_off32), f32(blk),
            b16(np.stack(inv_masks)), f32(eye_c), f32(lane0))


def _rwkv_scan(r, v, av, kd, b, lw, dims):
    rows = r.shape[0]
    n_b, ncc, nct = dims["batch"], dims["c_len"] // CHUNK, dims["t_len"] // CHUNK
    consts = _scan_constants()

    assert len(consts) == N_SCAN_CONSTS

    def chunk(s, d):
        j_ctx = s if d == 0 else ncc - 1 - s
        j_lat = s - ncc if d == 0 else nct - 1 - (s - ncc)
        return j_ctx, j_lat

    def flat_chunk(bi, d):
        def index(s):
            j_ctx, j_lat = chunk(s, d)
            return jnp.where(s < ncc, bi * ncc + j_ctx, n_b * ncc + bi * nct + j_lat)
        return index

    in_specs, args = [], []
    for bi in range(n_b):
        for d in range(2):
            idx = flat_chunk(bi, d)
            shared = pl.BlockSpec((CHUNK, D_RWKV), lambda s, idx=idx: (idx(s), 0))
            per_dir = pl.BlockSpec((None, CHUNK, D_RWKV), lambda s, idx=idx, d=d: (d, idx(s), 0))
            in_specs += [shared] * 3 + [per_dir] * 3
            args += [r, v, av, kd, b, lw]
    in_specs += [pl.BlockSpec(a.shape, lambda s, nd=a.ndim: (0,) * nd) for a in consts]

    def y_spec(d):
        def index(s):
            j_ctx, j_lat = chunk(s, d)
            return (0, jnp.where(s < ncc, j_ctx, ncc + j_lat), 0)
        return pl.BlockSpec((n_b, CHUNK, D_RWKV), index)

    y_shape = jax.ShapeDtypeStruct((n_b, rows // n_b, D_RWKV), F32)
    return pl.pallas_call(
        functools.partial(_scan_kernel, n_b=n_b),
        grid=(ncc + nct,),
        in_specs=in_specs,
        out_specs=[y_spec(0), y_spec(1)],
        out_shape=[y_shape, y_shape],
        scratch_shapes=[pltpu.VMEM((n_b, 2, N_GROUPS, MXU_DIM, MXU_DIM), F32)],
        compiler_params=_cparams(("arbitrary",)),
        name="rwkv_chunk_scan",
    )(*args, *consts)


def _pool_kernel(z_ref, zp_ref, zn_ref, w_ref, sc_ref, o_ref, *, tm, dims):
    i = pl.program_id(0)
    pos0, seq_len, first, last = _seq_edges(i, tm, dims)
    prev = jnp.where(first, 0.0, zp_ref[...])
    nxt = jnp.where(last, 0.0, zn_ref[...])
    ext = jnp.concatenate([prev, z_ref[...], nxt], axis=0)
    n = tm + 2 * POOL_HALO
    t = pos0 + lax.broadcasted_iota(jnp.int32, (tm, 1), 0)
    outs = []
    for gi, w in enumerate(POOL_WINDOWS):
        e = ext[:, POOL_GROUP_DIM * gi:POOL_GROUP_DIM * (gi + 1)]
        acc, width = e, 1
        while width < w:
            acc = acc + pltpu.roll(acc, n - width, axis=0)
            width *= 2
        win = pltpu.roll(acc, w // 2, axis=0)[POOL_HALO:POOL_HALO + tm]
        lo = jnp.clip(t - w // 2, 0, seq_len)
        hi = jnp.clip(t - w // 2 + w, 0, seq_len)
        y = win / (hi - lo).astype(F32) - e[POOL_HALO:POOL_HALO + tm]
        outs.append(_dot(y.astype(BF16), w_ref[gi].astype(BF16)))
    o_ref[...] = (jnp.concatenate(outs, axis=1) * sc_ref[...]).astype(BF16)


def _pool(z_p, pool_w, pool_scale, dims):
    rows = z_p.shape[0]
    tm = dims["tm"]
    nblk8 = rows // SUBLANES
    per = tm // SUBLANES
    return pl.pallas_call(
        functools.partial(_pool_kernel, tm=tm, dims=dims),
        grid=(rows // tm,),
        in_specs=[
            pl.BlockSpec((tm, D_POOL), lambda i: (i, 0)),
            pl.BlockSpec((SUBLANES, D_POOL), lambda i: (jnp.maximum(i * per - 1, 0), 0)),
            pl.BlockSpec((SUBLANES, D_POOL), lambda i: (jnp.minimum((i + 1) * per, nblk8 - 1), 0)),
            pl.BlockSpec(pool_w.shape, lambda i: (0, 0, 0)),
            pl.BlockSpec((1, D_POOL), lambda i: (0, 0)),
        ],
        out_specs=pl.BlockSpec((tm, D_POOL), lambda i: (i, 0)),
        out_shape=jax.ShapeDtypeStruct((rows, D_POOL), BF16),
        compiler_params=_cparams(("arbitrary",)),
        name="multiscale_pool",
    )(z_p, z_p, z_p, pool_w, pool_scale.reshape(1, D_POOL))


def _attn_kernel(sink_ref, q_ref, *rest, windowed, nblk):
    if windowed:
        kp_ref, kc_ref, kn_ref, kx_ref, o_ref = rest
        kv = jnp.concatenate([kp_ref[...], kc_ref[...], kn_ref[...], kx_ref[...]], axis=0)
    else:
        kx_ref, o_ref = rest
        kv = kx_ref[...]
    n = pl.program_id(1)
    q = q_ref[...]
    k_all, v_all = kv[:, 0:ATT_KV_DIM], kv[:, ATT_KV_DIM:2 * ATT_KV_DIM]
    nk = kv.shape[0]
    rows = ATT_GROUP * ATT_BLOCK
    if windowed:
        qi = lax.broadcasted_iota(jnp.int32, (ATT_BLOCK, nk), 0)
        kj = lax.broadcasted_iota(jnp.int32, (ATT_BLOCK, nk), 1)
        rel = kj - ATT_BLOCK - qi
        k_lo = jnp.where(n == 0, ATT_BLOCK, 0)
        k_hi = jnp.where(n == nblk - 1, 2 * ATT_BLOCK, 3 * ATT_BLOCK)
        bad = jnp.where(rel > ATT_BLOCK, 1, 0) + jnp.where(rel < -ATT_BLOCK, 1, 0) \
            + jnp.where(kj < k_lo, 1, 0) + jnp.where(kj >= k_hi, 1, 0)
        bad = jnp.where(kj >= 3 * ATT_BLOCK, 0, bad)
        neg = jnp.tile(jnp.where(bad > 0, -1e30, 0.0).astype(F32), (ATT_GROUP, 1))
    lane = lax.shift_right_logical(lax.broadcasted_iota(jnp.int32, (1, ATT_KV_DIM), 1), HEAD_DIM.bit_length() - 1)
    grow = lax.shift_right_logical(lax.broadcasted_iota(jnp.int32, (rows, 1), 0), ATT_BLOCK.bit_length() - 1)
    def kv_head(hk):
        lm_f = (lane == hk).astype(F32)
        lm_b = lm_f.astype(BF16)
        lhs = jnp.concatenate([q[:, ATT_KV_DIM * g:ATT_KV_DIM * (g + 1)] * lm_b for g in range(ATT_GROUP)], axis=0)
        s = _dot_nt(lhs, k_all)
        yield
        if windowed:
            s = s + neg
        sk = jnp.where(grow == 0, sink_ref[ATT_GROUP * hk],
                       jnp.where(grow == 1, sink_ref[ATT_GROUP * hk + 1], sink_ref[ATT_GROUP * hk + 2]))
        m = jnp.maximum(jnp.max(s, axis=1, keepdims=True), sk)
        p = jnp.exp(s - m)
        den = jnp.sum(p, axis=1, keepdims=True) + jnp.exp(sk - m)
        yield
        pv = _dot(p.astype(BF16), v_all) / den
        return [pv[ATT_BLOCK * g:ATT_BLOCK * (g + 1)] * lm_f for g in range(ATT_GROUP)]

    heads = _run_interleaved([kv_head(hk) for hk in range(ATT_KV_HEADS)])
    outs = [sum((head[g] for head in heads[1:]), heads[0][g]) for g in range(ATT_GROUP)]
    o_ref[...] = jnp.concatenate(outs, axis=1).astype(BF16)


def _attention(q, kv, sink, dims, windowed):
    rows = q.shape[0]
    n_b, c_len, t_len, nc = dims["batch"], dims["c_len"], dims["t_len"], dims["nc"]
    seq = t_len if windowed else c_len
    nblk = seq // ATT_BLOCK
    base = (nc // ATT_BLOCK) if windowed else 0
    qrow = lambda bi, n: (base + bi * nblk + n, 0)
    ctx_spec = pl.BlockSpec((c_len, 2 * ATT_KV_DIM), lambda bi, n: (bi, 0))
    blk = lambda f: pl.BlockSpec((ATT_BLOCK, 2 * ATT_KV_DIM), f)
    in_specs = [pl.BlockSpec(memory_space=pltpu.SMEM), pl.BlockSpec((ATT_BLOCK, D_ATT), qrow)]
    args = [sink, q]
    if windowed:
        in_specs += [
            blk(lambda bi, n: (base + bi * nblk + jnp.maximum(n - 1, 0), 0)),
            blk(qrow),
            blk(lambda bi, n: (base + bi * nblk + jnp.minimum(n + 1, nblk - 1), 0)),
        ]
        args += [kv, kv, kv]
    in_specs.append(ctx_spec)
    args.append(kv)
    out_rows = n_b * seq
    return pl.pallas_call(
        functools.partial(_attn_kernel, windowed=windowed, nblk=nblk),
        grid=(n_b, nblk),
        in_specs=in_specs,
        out_specs=pl.BlockSpec((ATT_BLOCK, D_ATT), lambda bi, n: (bi * nblk + n, 0)),
        out_shape=jax.ShapeDtypeStruct((out_rows, D_ATT), BF16),
        compiler_params=_cparams(("arbitrary", "arbitrary")),
        name="window_attention" if windowed else "context_attention",
    )(*args)


def _outproj_kernel(yf_ref, yb_ref, bonus_ref, g_ref, yp_ref, yac_ref, yal_ref, xc_ref, xl_ref, w_ref, wa_ref,
                    lng_ref, lnb_ref, hs_ref, gt_ref, ng_ref, sc_ref, sh_ref, xo_ref, ho_ref, *, ctx_tiles, first_tile):
    hs = hs_ref[...]
    y = yf_ref[...] + yb_ref[...]
    inv_n = 1.0 / HEAD_DIM
    dev = y - _head_sum(y, hs) * inv_n
    var = _head_sum(dev * dev, hs) * inv_n
    yn = dev * lax.rsqrt(var + RWKV_GN_EPS) * lng_ref[...] + lnb_ref[...]
    y_r = ((yn + bonus_ref[...]) * g_ref[...]).astype(BF16)
    acc = (_dot(y_r, w_ref[0:D_RWKV])
           + _dot(yp_ref[...], w_ref[D_RWKV:D_RWKV + D_POOL])
           + _dot(_pick_source(yac_ref, yal_ref, ctx_tiles, first_tile), wa_ref[...]))
    x_new = _pick_source(xc_ref, xl_ref, ctx_tiles, first_tile) + gt_ref[...] * acc
    xo_ref[...] = x_new
    ho_ref[...] = _rms_mod(x_new, ng_ref[...], sc_ref[...], sh_ref[...]).astype(BF16)


def _out_projection(yf, yb, bonus, g, y_p, ya_src, x_src, w_out_b, wa_b, layer, ln_g, ln_b, hs, mod3, mod_base, norm_g,
                    dims, first_tile):
    tm = dims["tm"]
    rows, d = dims["rows"], x_src[0].shape[1]
    ctx_tiles = dims["nc"] // tm
    ya_specs, ya_args = _two_source_specs(*ya_src, tm, ctx_tiles, first_tile)
    x_specs, x_args = _two_source_specs(*x_src, tm, ctx_tiles, first_tile)
    grp = dims["grp"]
    n_tiles = rows // tm - first_tile
    off = lambda w: pl.BlockSpec((tm, w), lambda i: (i + first_tile, 0))

    def mod_spec(k):
        return pl.BlockSpec((None, 1, d), lambda i: (mod_base + grp(i + first_tile, tm) * 6 + k, 0, 0))

    def y_index(i):
        ft = i + first_tile
        ctx_tiles, per_ctx, per_lat = dims["nc"] // tm, dims["c_len"] // tm, dims["t_len"] // tm
        lt = jnp.maximum(ft - ctx_tiles, 0)
        in_ctx = ft < ctx_tiles
        return (jnp.where(in_ctx, ft // per_ctx, lt // per_lat),
                jnp.where(in_ctx, ft % per_ctx, per_ctx + lt % per_lat), 0)

    y_spec = pl.BlockSpec((None, tm, D_RWKV), y_index)
    vec = lambda w: pl.BlockSpec((1, w), lambda i: (0, 0))
    out_rows = n_tiles * tm
    return pl.pallas_call(
        functools.partial(_outproj_kernel, ctx_tiles=ctx_tiles, first_tile=first_tile),
        grid=(n_tiles,),
        in_specs=[y_spec, y_spec, off(D_RWKV), off(D_RWKV), off(D_POOL)] + ya_specs + x_specs + [
            pl.BlockSpec((None, d, d), lambda i: (layer, 0, 0), pipeline_mode=pl.Buffered(1)),
            pl.BlockSpec((None, D_ATT, d), lambda i: (layer, 0, 0), pipeline_mode=pl.Buffered(1)),
            vec(D_RWKV), vec(D_RWKV),
            pl.BlockSpec((MXU_DIM, MXU_DIM), lambda i: (0, 0)),
            mod_spec(2), vec(d), mod_spec(4), mod_spec(3),
        ],
        out_specs=[pl.BlockSpec((tm, d), lambda i: (i, 0)), pl.BlockSpec((tm, d), lambda i: (i, 0))],
        out_shape=[jax.ShapeDtypeStruct((out_rows, d), F32), jax.ShapeDtypeStruct((out_rows, d), BF16)],
        compiler_params=_cparams(("arbitrary",)),
        name="readout_outproj_norm",
    )(yf, yb, bonus, g, y_p, *ya_args, *x_args, w_out_b, wa_b, ln_g.reshape(1, -1), ln_b.reshape(1, -1), hs,
      mod3, norm_g.reshape(1, d), mod3, mod3)


def _ffn_kernel(h_ref, x_ref, wg_ref, wu_ref, wd_ref, gt_ref, fg_ref, o_ref, acc_ref, *, final):
    j = pl.program_id(1)

    @pl.when(j == 0)
    def _():
        acc_ref[...] = jnp.zeros(acc_ref.shape, F32)

    h = h_ref[...]
    n_sub = wg_ref.shape[1] // MXU_DIM
    slabs = [slice(MXU_DIM * q, MXU_DIM * (q + 1)) for q in range(n_sub)]
    gate_up = [(_dot(h, wg_ref[:, sl]), _dot(h, wu_ref[:, sl])) for sl in slabs]
    part = None
    for sl, (gate, up) in zip(slabs, gate_up):
        act = (gate * jax.nn.sigmoid(gate) * up).astype(BF16)
        p = _dot(act, wd_ref[sl, :])
        part = p if part is None else part + p
    acc_ref[...] += part

    @pl.when(j == pl.num_programs(1) - 1)
    def _():
        x_new = x_ref[...] + gt_ref[...] * acc_ref[...]
        if final:
            ms = jnp.mean(x_new * x_new, axis=-1, keepdims=True)
            x_new = x_new * lax.rsqrt(ms + NORM_EPS) * fg_ref[...]
        o_ref[...] = x_new


def _ffn(h2, x_new, w_gu_b, w_down_b, layer, mod3, mod_base, final_g, dims, first_tile, final):
    rows, d = x_new.shape
    tmf, th = dims["tmf"], dims["th"]
    hidden = w_down_b.shape[1]
    nh = hidden // th
    grp = dims["grp"]
    return pl.pallas_call(
        functools.partial(_ffn_kernel, final=final),
        grid=(rows // tmf, nh),
        in_specs=[
            pl.BlockSpec((tmf, d), lambda i, j: (i, 0)),
            pl.BlockSpec((tmf, d), lambda i, j: (i, 0)),
            pl.BlockSpec((None, d, th), lambda i, j: (layer, 0, j)),
            pl.BlockSpec((None, d, th), lambda i, j: (layer, 0, nh + j)),
            pl.BlockSpec((None, th, d), lambda i, j: (layer, j, 0)),
            pl.BlockSpec((None, 1, d), lambda i, j: (mod_base + grp(i + first_tile, tmf) * 6 + 5, 0, 0)),
            pl.BlockSpec((1, d), lambda i, j: (0, 0)),
        ],
        out_specs=pl.BlockSpec((tmf, d), lambda i, j: (i, 0)),
        out_shape=jax.ShapeDtypeStruct((rows, d), F32),
        scratch_shapes=[pltpu.VMEM((tmf, d), F32)],
        compiler_params=_cparams(("arbitrary", "arbitrary")),
        name="swiglu_ffn_final" if final else "swiglu_ffn",
    )(h2, x_new, w_gu_b, w_gu_b, w_down_b, mod3, final_g.reshape(1, d))


def _rope_tables(dims):
    t_len, nc, n_b = dims["t_len"], dims["nc"], dims["batch"]
    grid_rows = t_len // GRID_W
    n_freq = HEAD_DIM // 4
    inv = ROPE_BASE ** (-jnp.arange(n_freq, dtype=F32) / n_freq)
    ang_r = jnp.arange(grid_rows).astype(F32)[:, None] * inv
    ang_c = jnp.arange(GRID_W).astype(F32)[:, None] * inv

    def per_token(fn):
        by_row = jnp.repeat(fn(ang_r), GRID_W, axis=0)
        by_col = jnp.tile(fn(ang_c), (grid_rows, 1))
        return jnp.tile(jnp.concatenate([by_row, by_row, by_col, by_col], axis=-1), (1, LANES // HEAD_DIM))

    cos, sin = per_token(jnp.cos), per_token(jnp.sin)
    first_half = (jnp.arange(LANES) % 32) < 16
    sin_a = jnp.where(first_half, -sin, 0.0)
    sin_b = jnp.where(first_half, 0.0, sin)
    flat = lambda tab, fill: jnp.concatenate([jnp.full((nc, LANES), fill, F32), jnp.tile(tab, (n_b, 1))], axis=0)
    return flat(cos, 1.0), flat(sin_a, 0.0), flat(sin_b, 0.0)


def _block_diag2(w):
    z = jnp.zeros_like(w[0])
    return jnp.concatenate([jnp.concatenate([w[0], z], axis=1), jnp.concatenate([z, w[1]], axis=1)], axis=0)


def kernel(x, c, ctx, c_ctx, ada_w, ada_b, norm_mix_g, norm_ffn_g, w_in, rwkv_mu, rwkv_w0, rwkv_w_up, rwkv_a0,
           rwkv_a_up, rwkv_g_up, rwkv_k_k, rwkv_k_a, rwkv_r_k, rwkv_ln_g, rwkv_ln_b, pool_w, pool_scale,
           attn_sink, w_out, ffn_w_gu, ffn_w_down, final_norm_g):
    n_b, t_len, d = x.shape
    c_len = ctx.shape[1]
    depth = w_in.shape[0]
    nc = n_b * c_len
    rows = nc + n_b * t_len
    tm = 256 if c_len % 256 == 0 else c_len
    tmf = nc
    assert n_b + 1 <= SUBLANES and c_len % tm == 0 and t_len % tm == 0 and t_len % tmf == 0
    assert tm % ATT_BLOCK == 0 and c_len % CHUNK == 0 and t_len % GRID_W == 0 and d == w_out.shape[1]

    def grp(i, tile):
        r0 = i * tile
        return jnp.where(r0 < nc, 0, 1 + jnp.maximum(r0 - nc, 0) // t_len)

    hidden = ffn_w_down.shape[1]
    th = 512 if hidden % 512 == 0 else hidden
    dims = dict(batch=n_b, c_len=c_len, t_len=t_len, nc=nc, rows=rows, tm=tm, tmf=tmf, th=th, grp=grp)

    cvec = jnp.concatenate([c_ctx[None, :], c, jnp.zeros((SUBLANES - 1 - n_b, d), F32)], axis=0)
    mod = _modulation(cvec, ada_w, ada_b)
    mod3 = mod.reshape(depth * SUBLANES * 6, 1, d)

    x_src = (ctx.reshape(nc, d), x.reshape(n_b * t_len, d), 0)
    rope = _rope_tables(dims)
    hs = jnp.asarray(np.kron(np.eye(HEADS_PER_GROUP), np.ones((HEAD_DIM, HEAD_DIM))), BF16)

    w_in_b = w_in.astype(BF16)
    wq_b = w_in_b[:, :, OFF_AQ:OFF_AK].reshape(depth, d, ATT_KV_HEADS, ATT_GROUP, HEAD_DIM)
    wq_b = wq_b.transpose(0, 1, 3, 2, 4).reshape(depth, d, D_ATT)
    w_out_b = w_out.astype(BF16)
    wa_b = w_out_b[:, D_RWKV + D_POOL:].reshape(depth, ATT_KV_HEADS, ATT_GROUP, HEAD_DIM, d)
    wa_b = wa_b.transpose(0, 2, 1, 3, 4).reshape(depth, D_ATT, d)
    w_gu_b = ffn_w_gu.astype(BF16)
    w_down_b = ffn_w_down.astype(BF16)

    out = None
    for l in range(depth):
        last = l == depth - 1
        mod_base = l * SUBLANES * 6
        rw = dict(
            mu=rwkv_mu[l].reshape(1, RWKV_COLS),
            w0=rwkv_w0[l].reshape(1, 2 * D_RWKV),
            w_up=_block_diag2(rwkv_w_up[l]).astype(BF16),
            a0=rwkv_a0[l].reshape(1, 2 * D_RWKV),
            a_up=_block_diag2(rwkv_a_up[l]).astype(BF16),
            g_up=rwkv_g_up[l].astype(BF16),
            k_k=rwkv_k_k[l].reshape(1, D_RWKV),
            k_a=rwkv_k_a[l].reshape(1, D_RWKV),
            r_k=rwkv_r_k[l].reshape(1, D_RWKV),
        )

        z_r, z_p, q, kv = _in_projection(x_src, mod3, mod_base, norm_mix_g[l], w_in_b, wq_b, l, rope, dims)
        r, v, av, g, bonus, kd, b, lw = _rwkv_prepare(z_r, rw, hs, dims)
        yf, yb = _rwkv_scan(r, v, av, kd, b, lw, dims)
        y_p = _pool(z_p, pool_w[l], pool_scale[l], dims)
        ya_lat = _attention(q, kv, attn_sink[l], dims, windowed=True)
        ya_ctx = ya_lat if last else _attention(q, kv, attn_sink[l], dims, windowed=False)
        first_tile = nc // tm if last else 0
        x_new, h2 = _out_projection(yf, yb, bonus, g, y_p, (ya_ctx, ya_lat, 0), x_src, w_out_b, wa_b, l,
                                    rwkv_ln_g[l], rwkv_ln_b[l], hs, mod3, mod_base, norm_ffn_g[l], dims, first_tile)
        x_next = _ffn(h2, x_new, w_gu_b, w_down_b, l, mod3, mod_base, final_norm_g, dims,
                      first_tile * tm // tmf, final=last)
        if last:
            out = x_next.reshape(n_b, t_len, d)
        else:
            x_src = (x_next, x_next, nc // tm)
    return out
```

```python
import functools

import numpy as np
import jax
import jax.numpy as jnp
from jax import lax
from jax.experimental import pallas as pl
from jax.experimental.pallas import tpu as pltpu

F32 = jnp.float32
BF16 = jnp.bfloat16

HEAD_DIM = 64
RWKV_HEADS = 12
D_RWKV = RWKV_HEADS * HEAD_DIM
LORA_RANK = 64
GATE_RANK = 128
D_POOL = 512
POOL_WINDOWS = (2, 4, 8, 16)
POOL_GROUP_DIM = 128
POOL_HALO = 8
ATT_Q_HEADS = 12
ATT_KV_HEADS = 4
ATT_GROUP = ATT_Q_HEADS // ATT_KV_HEADS
D_ATT = ATT_Q_HEADS * HEAD_DIM
ATT_KV_DIM = ATT_KV_HEADS * HEAD_DIM
ATT_BLOCK = 128
GRID_W = 64
ROPE_BASE = 10000.0
NORM_EPS = 1e-6
RWKV_GN_EPS = 64e-5

OFF_K = D_RWKV
OFF_V = 2 * D_RWKV
OFF_WD = 3 * D_RWKV
OFF_AD = OFF_WD + 2 * LORA_RANK
OFF_GD = OFF_AD + 2 * LORA_RANK
RWKV_COLS = OFF_GD + GATE_RANK
OFF_POOL = RWKV_COLS
OFF_AQ = OFF_POOL + D_POOL
OFF_AK = OFF_AQ + D_ATT
OFF_AV = OFF_AK + ATT_KV_DIM
D_IN = OFF_AV + ATT_KV_DIM

LANES = 128
SUBLANES = 8
MXU_DIM = 256
HEADS_PER_GROUP = MXU_DIM // HEAD_DIM
N_GROUPS = D_RWKV // MXU_DIM
CHUNK = MXU_DIM // HEADS_PER_GROUP
INV_BASE = 8
N_INV_MASKS = (CHUNK // INV_BASE).bit_length() - 2
VMEM_LIMIT = 56 * 1024 * 1024


def _cparams(sem, vmem=VMEM_LIMIT):
    return pltpu.CompilerParams(dimension_semantics=sem, vmem_limit_bytes=vmem)


def _dot(a, b):
    return jnp.dot(a, b, preferred_element_type=F32)


def _dot_nt(a, b):
    return lax.dot_general(a, b, (((1,), (1,)), ((), ())), preferred_element_type=F32)


def _dot_tn(a, b):
    return lax.dot_general(a, b, (((0,), (0,)), ((), ())), preferred_element_type=F32)


def _split3(x):
    x1 = x.astype(BF16)
    r1 = x - x1.astype(F32)
    x2 = r1.astype(BF16)
    x3 = (r1 - x2.astype(F32)).astype(BF16)
    return x1, x2, x3


def _dot_exact_rhs(x, m_bf16):
    x1, x2, x3 = _split3(x)
    return _dot(x1, m_bf16) + _dot(x2, m_bf16) + _dot(x3, m_bf16)


def _dot_exact_lhs(m_bf16, x):
    x1, x2, x3 = _split3(x)
    return _dot(m_bf16, x1) + _dot(m_bf16, x2) + _dot(m_bf16, x3)


def _head_sum(x, hs):
    parts = [_dot_exact_rhs(x[:, MXU_DIM * g:MXU_DIM * (g + 1)], hs) for g in range(N_GROUPS)]
    return jnp.concatenate(parts, axis=1)


def _rms_mod(x, g, sc, sh):
    ms = jnp.mean(x * x, axis=-1, keepdims=True)
    return x * lax.rsqrt(ms + NORM_EPS) * g * (1.0 + sc) + sh


def _mod_kernel(c_ref, w_ref, b_ref, o_ref):
    c = c_ref[...]
    s = (c * jax.nn.sigmoid(c)).astype(BF16)
    o_ref[...] = _dot(s, w_ref[...].astype(BF16)) + b_ref[...]


def _modulation(cvec, ada_w, ada_b):
    n_layers, d, n6 = ada_w.shape
    tn = 1024
    return pl.pallas_call(
        _mod_kernel,
        grid=(n_layers, n6 // tn),
        in_specs=[
            pl.BlockSpec((SUBLANES, d), lambda l, j: (0, 0)),
            pl.BlockSpec((None, d, tn), lambda l, j: (l, 0, j)),
            pl.BlockSpec((None, 1, tn), lambda l, j: (l, 0, j)),
        ],
        out_specs=pl.BlockSpec((None, SUBLANES, tn), lambda l, j: (l, 0, j)),
        out_shape=jax.ShapeDtypeStruct((n_layers, SUBLANES, n6), F32),
        compiler_params=_cparams(("arbitrary", "arbitrary")),
        name="adaln_modulation",
    )(cvec, ada_w, ada_b.reshape(n_layers, 1, n6))


def _rope(z, cos, sin_a, sin_b):
    outs = []
    for j in range(z.shape[1] // LANES):
        zb = z[:, LANES * j:LANES * (j + 1)]
        outs.append(zb * cos + pltpu.roll(zb, LANES - 16, axis=1) * sin_a + pltpu.roll(zb, 16, axis=1) * sin_b)
    return jnp.concatenate(outs, axis=1)


def _two_source_specs(src_ctx, src_lat, lat_first_tile, tm, ctx_tiles, first_tile):
    width = src_ctx.shape[1]
    spec_c = pl.BlockSpec((tm, width), lambda i: (jnp.minimum(i + first_tile, ctx_tiles - 1), 0))
    spec_l = pl.BlockSpec((tm, width), lambda i: (jnp.maximum(i + first_tile - ctx_tiles, 0) + lat_first_tile, 0))
    return [spec_c, spec_l], [src_ctx, src_lat]


def _pick_source(c_ref, l_ref, ctx_tiles, first_tile):
    return jnp.where(pl.program_id(0) + first_tile < ctx_tiles, c_ref[...], l_ref[...])


def _inproj_kernel(xc_ref, xl_ref, g_ref, sc_ref, sh_ref, w_ref, wq_ref, cos_ref, sa_ref, sb_ref,
                   zr_ref, zp_ref, q_ref, kv_ref, *, ctx_tiles):
    x = _pick_source(xc_ref, xl_ref, ctx_tiles, 0)
    h = _rms_mod(x, g_ref[...], sc_ref[...], sh_ref[...]).astype(BF16)
    zr_ref[...] = _dot(h, w_ref[:, 0:RWKV_COLS])
    zp_ref[...] = _dot(h, w_ref[:, OFF_POOL:OFF_AQ])
    cos, sa, sb = cos_ref[...], sa_ref[...], sb_ref[...]
    zq = _dot(h, wq_ref[...])
    q_ref[...] = (_rope(zq, cos, sa, sb) * (HEAD_DIM ** -0.5)).astype(BF16)
    zk = _dot(h, w_ref[:, OFF_AK:OFF_AV])
    zv = _dot(h, w_ref[:, OFF_AV:D_IN])
    kv_ref[...] = jnp.concatenate([_rope(zk, cos, sa, sb), zv], axis=1).astype(BF16)


def _in_projection(x_src, mod3, mod_base, norm_g, w_in_b, wq_b, layer, rope, dims):
    tm = dims["tm"]
    rows, d = dims["rows"], x_src[0].shape[1]
    ctx_tiles = dims["nc"] // tm
    x_specs, x_args = _two_source_specs(*x_src, tm, ctx_tiles, 0)
    grp = dims["grp"]
    cos, sin_a, sin_b = rope

    def mod_spec(k):
        return pl.BlockSpec((None, 1, d), lambda i: (mod_base + grp(i, tm) * 6 + k, 0, 0))

    row_spec = lambda w: pl.BlockSpec((tm, w), lambda i: (i, 0))
    return pl.pallas_call(
        functools.partial(_inproj_kernel, ctx_tiles=ctx_tiles),
        grid=(rows // tm,),
        in_specs=x_specs + [
            pl.BlockSpec((1, d), lambda i: (0, 0)),
            mod_spec(1), mod_spec(0),
            pl.BlockSpec((None, d, D_IN), lambda i: (layer, 0, 0), pipeline_mode=pl.Buffered(1)),
            pl.BlockSpec((None, d, D_ATT), lambda i: (layer, 0, 0), pipeline_mode=pl.Buffered(1)),
            row_spec(LANES), row_spec(LANES), row_spec(LANES),
        ],
        out_specs=[row_spec(RWKV_COLS), row_spec(D_POOL), row_spec(D_ATT), row_spec(2 * ATT_KV_DIM)],
        out_shape=[
            jax.ShapeDtypeStruct((rows, RWKV_COLS), F32),
            jax.ShapeDtypeStruct((rows, D_POOL), F32),
            jax.ShapeDtypeStruct((rows, D_ATT), BF16),
            jax.ShapeDtypeStruct((rows, 2 * ATT_KV_DIM), BF16),
        ],
        compiler_params=_cparams(("arbitrary",)),
        name="norm_inproj_rope",
    )(*x_args, norm_g.reshape(1, d), mod3, mod3, w_in_b, wq_b, cos, sin_a, sin_b)


def _seq_edges(i, tm, dims):
    nc, c_len, t_len = dims["nc"], dims["c_len"], dims["t_len"]
    r0 = i * tm
    is_ctx = r0 < nc
    seq_len = jnp.where(is_ctx, c_len, t_len)
    pos0 = jnp.where(is_ctx, lax.rem(r0, c_len), lax.rem(jnp.maximum(r0 - nc, 0), t_len))
    return pos0, seq_len, pos0 == 0, pos0 + tm == seq_len


def _prep_kernel(z_ref, zp_ref, zn_ref, mu_ref, w0_ref, wup_ref, a0_ref, aup_ref, gup_ref,
                 kk_ref, ka_ref, rk_ref, hs_ref,
                 r_ref, v_ref, av_ref, g_ref, bonus_ref, kd_ref, b_ref, lw_ref, *, tm, dims):
    i = pl.program_id(0)
    _, _, first, last = _seq_edges(i, tm, dims)
    z = z_ref[...]
    prev_row = jnp.where(first, 0.0, zp_ref[SUBLANES - 1:SUBLANES, :])
    next_row = jnp.where(last, 0.0, zn_ref[0:1, :])
    row = lax.broadcasted_iota(jnp.int32, (tm, 1), 0)
    z_m1 = jnp.where(row == 0, prev_row, pltpu.roll(z, 1, axis=0))
    z_p1 = jnp.where(row == tm - 1, next_row, pltpu.roll(z, tm - 1, axis=0))
    zs = z + mu_ref[...] * (0.5 * (z_m1 + z_p1) - z)

    r = zs[:, 0:OFF_K]
    k = zs[:, OFF_K:OFF_V]
    v = zs[:, OFF_V:OFF_WD]
    wd = zs[:, OFF_WD:OFF_AD]
    ad = zs[:, OFF_AD:OFF_GD]
    gd = zs[:, OFF_GD:RWKV_COLS]

    w_pre = w0_ref[...] + _dot(jnp.tanh(wd).astype(BF16), wup_ref[...])
    lw = -float(np.exp(-0.5)) * jax.nn.sigmoid(w_pre)
    a = jax.nn.sigmoid(a0_ref[...] + _dot(ad.astype(BF16), aup_ref[...]))
    g_ref[...] = _dot(jax.nn.sigmoid(gd).astype(BF16), gup_ref[...])

    hs = hs_ref[...]
    kx = k * kk_ref[...]
    kn = kx / jnp.maximum(jnp.sqrt(_head_sum(kx * kx, hs)), 1e-12)
    rk = r * rk_ref[...]
    ka = ka_ref[...]
    dot_rk = None
    for d in range(2):
        a_d = a[:, D_RWKV * d:D_RWKV * (d + 1)]
        kd = k * (1.0 + (a_d - 1.0) * ka)
        kd_ref[d] = kd.astype(BF16)
        b_ref[d] = (kn * a_d).astype(BF16)
        lw_ref[d] = lw[:, D_RWKV * d:D_RWKV * (d + 1)]
        dot_rk = rk * kd if dot_rk is None else dot_rk + rk * kd
    r_ref[...] = r.astype(BF16)
    v_ref[...] = v.astype(BF16)
    av_ref[...] = (-kn).astype(BF16)
    bonus_ref[...] = _head_sum(dot_rk, hs) * v


def _rwkv_prepare(z_r, p, hs, dims):
    rows = z_r.shape[0]
    tm = dims["tm"]
    nblk8 = rows // SUBLANES
    per = tm // SUBLANES
    full = lambda shape: pl.BlockSpec(shape, lambda i: (0,) * len(shape))
    row_spec = pl.BlockSpec((tm, D_RWKV), lambda i: (i, 0))
    dir_spec = pl.BlockSpec((2, tm, D_RWKV), lambda i: (0, i, 0))
    one = lambda dt: jax.ShapeDtypeStruct((rows, D_RWKV), dt)
    two = lambda dt: jax.ShapeDtypeStruct((2, rows, D_RWKV), dt)
    return pl.pallas_call(
        functools.partial(_prep_kernel, tm=tm, dims=dims),
        grid=(rows // tm,),
        in_specs=[
            pl.BlockSpec((tm, RWKV_COLS), lambda i: (i, 0)),
            pl.BlockSpec((SUBLANES, RWKV_COLS), lambda i: (jnp.maximum(i * per - 1, 0), 0)),
            pl.BlockSpec((SUBLANES, RWKV_COLS), lambda i: (jnp.minimum((i + 1) * per, nblk8 - 1), 0)),
            full((1, RWKV_COLS)),
            full((1, 2 * D_RWKV)), full((2 * LORA_RANK, 2 * D_RWKV)),
            full((1, 2 * D_RWKV)), full((2 * LORA_RANK, 2 * D_RWKV)),
            full((GATE_RANK, D_RWKV)),
            full((1, D_RWKV)), full((1, D_RWKV)), full((1, D_RWKV)),
            full((MXU_DIM, MXU_DIM)),
        ],
        out_specs=[row_spec, row_spec, row_spec, row_spec, row_spec, dir_spec, dir_spec, dir_spec],
        out_shape=[one(BF16), one(BF16), one(BF16), one(F32), one(F32), two(BF16), two(BF16), two(F32)],
        compiler_params=_cparams(("arbitrary",)),
        name="rwkv_prepare",
    )(z_r, z_r, z_r, p["mu"], p["w0"], p["w_up"], p["a0"], p["a_up"], p["g_up"],
      p["k_k"], p["k_a"], p["r_k"], hs)


def _block_rows_sum(x):
    parts = [x[s:s + INV_BASE] for s in range(0, x.shape[0], INV_BASE)]
    while len(parts) > 1:
        parts = [parts[i] + parts[i + 1] for i in range(0, len(parts), 2)]
    return parts[0]


def _group_bcast(a, m, lane0):
    halves = []
    for h in range(a.shape[1] // LANES):
        z = a[:, LANES * h:LANES * (h + 1)]
        if m:
            z = pltpu.roll(z, LANES - m, axis=1)
        z = z * lane0
        for shift in (1, 2, 4):
            z = z + pltpu.roll(z, shift, axis=1)
        halves.append(z)
    return jnp.concatenate(halves, axis=1)


def _scan_unit(r, v, av, kd, bd, e_c, e_cx, e_nc, e_rt, e_tot, masks, st_ref, d, g):
    w_strict, w_incl, w8_strict, w8, w_off32, bm, bm_b, inv_masks, eye_c, lane0 = masks
    sl = slice(MXU_DIM * g, MXU_DIM * (g + 1))

    def expand(xb):
        return jnp.tile(xb, (HEADS_PER_GROUP, 1)) * bm_b

    r_t = (r[:, sl] * e_c[:, sl]).astype(BF16)
    a_t = (av[:, sl] * e_cx[:, sl]).astype(BF16)
    k_t = (kd[:, sl] * e_nc[:, sl]).astype(BF16)
    b_t = (bd[:, sl] * e_nc[:, sl]).astype(BF16)
    k_h = (kd[:, sl] * e_rt[:, sl]).astype(BF16)
    b_h = (bd[:, sl] * e_rt[:, sl]).astype(BF16)
    v_b = v[:, sl].astype(BF16)
    a_r = jnp.concatenate([a_t, r_t], axis=0)
    eb, ek, ev = expand(b_t), expand(k_t), expand(v_b)
    yield

    g_b = _dot_nt(a_r, eb)
    yield
    g_k = _dot_nt(a_r, ek)
    yield
    a_ab_f = g_b[0:CHUNK]
    a_ab = a_ab_f.astype(BF16) * w_strict
    a_rb = g_b[CHUNK:].astype(BF16) * w_incl
    a_ak = g_k[0:CHUNK].astype(BF16) * w_strict
    a_rk = g_k[CHUNK:].astype(BF16) * w_incl
    g_ab = expand(a_ab)
    d_c = _block_rows_sum(a_ab_f * w8_strict)
    d_cols = [(m, _group_bcast(d_c, m, lane0)) for m in range(d, INV_BASE - 1 + d)]
    from_v = _dot(jnp.concatenate([a_ak, a_rk], axis=0), ev)
    yield

    def horner(x):
        out = eye_c
        for m, col in d_cols:
            out = out + col * jnp.broadcast_to(x[m:m + 1, :], x.shape)
        return out

    t8_c = eye_c + d_c
    for _ in range((INV_BASE - 2) // 2):
        t8_c = horner(t8_c)
    s0 = st_ref[d, g]
    from_state = _dot_nt(a_r, expand(s0.astype(BF16)))
    yield
    for _ in range(INV_BASE - 2 - (INV_BASE - 2) // 2):
        t8_c = horner(t8_c)
    rhs = from_state[0:CHUNK] + from_v[0:CHUNK]
    y_known = from_state[CHUNK:] + from_v[CHUNK:]

    t32 = (jnp.tile(t8_c, (CHUNK // INV_BASE, 1)) * w8).astype(BF16)
    for lvl in range(N_INV_MASKS):
        t_off = _dot(t32, g_ab * inv_masks[lvl]).astype(BF16)
        yield
        t32 = t32 + _dot(t_off, expand(t32)).astype(BF16)
        yield
    u32 = _dot(t32, expand(rhs.astype(BF16)))
    yield
    o_u = _dot(a_ab * w_off32, expand(u32.astype(BF16)))
    yield
    u_c = (u32 + _dot(t32, expand(o_u.astype(BF16)))).astype(BF16)
    yield
    y = y_known + _dot(a_rb, expand(u_c))
    yield
    upd = _dot_tn(jnp.concatenate([u_c, v_b], axis=0), jnp.concatenate([b_h, k_h], axis=0)) * bm
    upd_w = upd[0:CHUNK]
    for h in range(1, HEADS_PER_GROUP):
        upd_w = upd_w + upd[CHUNK * h:CHUNK * (h + 1)]
    st_ref[d, g] = s0 * e_tot[:, sl] + upd_w
    return y


def _run_interleaved(gens):
    results = [None] * len(gens)
    active = list(range(len(gens)))
    while active:
        for i in list(active):
            try:
                next(gens[i])
            except StopIteration as stop:
                results[i] = stop.value
                active.remove(i)
    return results


N_SCAN_CONSTS = 10
N_SCAN_STREAMS = 6


def _scan_kernel(*refs, n_b):
    n_in = n_b * 2 * N_SCAN_STREAMS
    streams, consts = refs[:n_in], refs[n_in:n_in + N_SCAN_CONSTS]
    y_refs, st_ref = refs[n_in + N_SCAN_CONSTS:-1], refs[-1]
    tri_ref, ws_ref, wi_ref, w8s_ref, w8_ref, woff_ref, bm_ref, im_ref, eyec_ref, lane0_ref = consts

    @pl.when(pl.program_id(0) == 0)
    def _():
        st_ref[...] = jnp.zeros(st_ref.shape, F32)

    bm = bm_ref[...]
    bm_b = bm.astype(BF16)
    inv_masks = [im_ref[lvl] for lvl in range(N_INV_MASKS)]
    w8, eye_c, lane0, w_off32 = w8_ref[...], eyec_ref[...], lane0_ref[...], woff_ref[...]
    units = []
    for bi in range(n_b):
        for d in range(2):
            base = (bi * 2 + d) * N_SCAN_STREAMS
            r_ref, v_ref, av_ref, kd_ref, b_ref, lw_ref = streams[base:base + N_SCAN_STREAMS]
            lw = lw_ref[...]
            c = _dot_exact_lhs(tri_ref[d], lw)
            c_tot = jnp.sum(lw, axis=0, keepdims=True)
            e_c, e_cx, e_nc = jnp.exp(c), jnp.exp(c - lw), jnp.exp(-c)
            e_rt, e_tot = jnp.exp(c_tot - c), jnp.exp(c_tot)
            masks = (ws_ref[d], wi_ref[d], w8s_ref[d], w8, w_off32, bm, bm_b, inv_masks, eye_c, lane0)
            r, v, av, kd, bd = r_ref[...], v_ref[...], av_ref[...], kd_ref[...], b_ref[...]
            units += [_scan_unit(r, v, av, kd, bd, e_c, e_cx, e_nc, e_rt, e_tot, masks, st_ref.at[bi], d, g)
                      for g in range(N_GROUPS)]
    ys = _run_interleaved(units)
    for bi in range(n_b):
        for d in range(2):
            first = (bi * 2 + d) * N_GROUPS
            y_refs[d][bi] = jnp.concatenate(ys[first:first + N_GROUPS], axis=1)


def _scan_constants():
    i = np.arange(MXU_DIM)
    blk = (i[:, None] // CHUNK) == (i[None, :] // CHUNK)
    j = np.arange(CHUNK)
    tri = np.stack([j[:, None] >= j[None, :], j[:, None] <= j[None, :]])
    t, s = j[:, None], i[None, :] % CHUNK
    wide_strict = np.stack([t > s, t < s])
    wide_incl = np.stack([t >= s, t <= s])
    wide8 = (t // INV_BASE) == (s // INV_BASE)
    wide8_strict = wide_strict & wide8[None]
    wide_off32 = (t // (CHUNK // 2)) != (s // (CHUNK // 2))
    inv_masks = []
    n = INV_BASE
    while 2 * n < CHUNK:
        inv_masks.append(((i[:, None] // (2 * n)) == (i[None, :] // (2 * n))) & ((i[:, None] // n) != (i[None, :] // n)))
        n *= 2
    assert len(inv_masks) == N_INV_MASKS
    eye_c = np.arange(INV_BASE)[:, None] == (i[None, :] % INV_BASE)
    lane0 = np.broadcast_to((np.arange(LANES) % INV_BASE) == 0, (INV_BASE, LANES))
    f32 = lambda a: jnp.asarray(a, F32)
    b16 = lambda a: jnp.asarray(a, BF16)
    return (b16(tri), b16(wide_strict), b16(wide_incl), f32(wide8_strict), f32(wide8), b16(wide_off32), f32(blk),
            b16(np.stack(inv_masks)), f32(eye_c), f32(lane0))


def _rwkv_scan(r, v, av, kd, b, lw, dims):
    rows = r.shape[0]
    n_b, ncc, nct = dims["batch"], dims["c_len"] // CHUNK, dims["t_len"] // CHUNK
    consts = _scan_constants()

    assert len(consts) == N_SCAN_CONSTS

    def chunk(s, d):
        j_ctx = s if d == 0 else ncc - 1 - s
        j_lat = s - ncc if d == 0 else nct - 1 - (s - ncc)
        return j_ctx, j_lat

    def flat_chunk(bi, d):
        def index(s):
            j_ctx, j_lat = chunk(s, d)
            return jnp.where(s < ncc, bi * ncc + j_ctx, n_b * ncc + bi * nct + j_lat)
        return index

    in_specs, args = [], []
    for bi in range(n_b):
        for d in range(2):
            idx = flat_chunk(bi, d)
            shared = pl.BlockSpec((CHUNK, D_RWKV), lambda s, idx=idx: (idx(s), 0))
            per_dir = pl.BlockSpec((None, CHUNK, D_RWKV), lambda s, idx=idx, d=d: (d, idx(s), 0))
            in_specs += [shared] * 3 + [per_dir] * 3
            args += [r, v, av, kd, b, lw]
    in_specs += [pl.BlockSpec(a.shape, lambda s, nd=a.ndim: (0,) * nd) for a in consts]

    def y_spec(d):
        def index(s):
            j_ctx, j_lat = chunk(s, d)
            return (0, jnp.where(s < ncc, j_ctx, ncc + j_lat), 0)
        return pl.BlockSpec((n_b, CHUNK, D_RWKV), index)

    y_shape = jax.ShapeDtypeStruct((n_b, rows // n_b, D_RWKV), F32)
    return pl.pallas_call(
        functools.partial(_scan_kernel, n_b=n_b),
        grid=(ncc + nct,),
        in_specs=in_specs,
        out_specs=[y_spec(0), y_spec(1)],
        out_shape=[y_shape, y_shape],
        scratch_shapes=[pltpu.VMEM((n_b, 2, N_GROUPS, HEAD_DIM, MXU_DIM), F32)],
        compiler_params=_cparams(("arbitrary",)),
        name="rwkv_chunk_scan",
    )(*args, *consts)


def _pool_kernel(z_ref, zp_ref, zn_ref, w_ref, sc_ref, o_ref, *, tm, dims):
    i = pl.program_id(0)
    pos0, seq_len, first, last = _seq_edges(i, tm, dims)
    prev = jnp.where(first, 0.0, zp_ref[...])
    nxt = jnp.where(last, 0.0, zn_ref[...])
    ext = jnp.concatenate([prev, z_ref[...], nxt], axis=0)
    n = tm + 2 * POOL_HALO
    t = pos0 + lax.broadcasted_iota(jnp.int32, (tm, 1), 0)
    outs = []
    for gi, w in enumerate(POOL_WINDOWS):
        e = ext[:, POOL_GROUP_DIM * gi:POOL_GROUP_DIM * (gi + 1)]
        acc, width = e, 1
        while width < w:
            acc = acc + pltpu.roll(acc, n - width, axis=0)
            width *= 2
        win = pltpu.roll(acc, w // 2, axis=0)[POOL_HALO:POOL_HALO + tm]
        lo = jnp.clip(t - w // 2, 0, seq_len)
        hi = jnp.clip(t - w // 2 + w, 0, seq_len)
        y = win / (hi - lo).astype(F32) - e[POOL_HALO:POOL_HALO + tm]
        outs.append(_dot(y.astype(BF16), w_ref[gi].astype(BF16)))
    o_ref[...] = (jnp.concatenate(outs, axis=1) * sc_ref[...]).astype(BF16)


def _pool(z_p, pool_w, pool_scale, dims):
    rows = z_p.shape[0]
    tm = dims["tm"]
    nblk8 = rows // SUBLANES
    per = tm // SUBLANES
    return pl.pallas_call(
        functools.partial(_pool_kernel, tm=tm, dims=dims),
        grid=(rows // tm,),
        in_specs=[
            pl.BlockSpec((tm, D_POOL), lambda i: (i, 0)),
            pl.BlockSpec((SUBLANES, D_POOL), lambda i: (jnp.maximum(i * per - 1, 0), 0)),
            pl.BlockSpec((SUBLANES, D_POOL), lambda i: (jnp.minimum((i + 1) * per, nblk8 - 1), 0)),
            pl.BlockSpec(pool_w.shape, lambda i: (0, 0, 0)),
            pl.BlockSpec((1, D_POOL), lambda i: (0, 0)),
        ],
        out_specs=pl.BlockSpec((tm, D_POOL), lambda i: (i, 0)),
        out_shape=jax.ShapeDtypeStruct((rows, D_POOL), BF16),
        compiler_params=_cparams(("arbitrary",)),
        name="multiscale_pool",
    )(z_p, z_p, z_p, pool_w, pool_scale.reshape(1, D_POOL))


def _attn_kernel(sink_ref, q_ref, *rest, windowed, nblk):
    if windowed:
        kp_ref, kc_ref, kn_ref, kx_ref, o_ref = rest
        kv = jnp.concatenate([kp_ref[...], kc_ref[...], kn_ref[...], kx_ref[...]], axis=0)
    else:
        kx_ref, o_ref = rest
        kv = kx_ref[...]
    n = pl.program_id(1)
    q = q_ref[...]
    k_all, v_all = kv[:, 0:ATT_KV_DIM], kv[:, ATT_KV_DIM:2 * ATT_KV_DIM]
    nk = kv.shape[0]
    rows = ATT_GROUP * ATT_BLOCK
    if windowed:
        qi = lax.broadcasted_iota(jnp.int32, (ATT_BLOCK, nk), 0)
        kj = lax.broadcasted_iota(jnp.int32, (ATT_BLOCK, nk), 1)
        rel = kj - ATT_BLOCK - qi
        k_lo = jnp.where(n == 0, ATT_BLOCK, 0)
        k_hi = jnp.where(n == nblk - 1, 2 * ATT_BLOCK, 3 * ATT_BLOCK)
        bad = jnp.where(rel > ATT_BLOCK, 1, 0) + jnp.where(rel < -ATT_BLOCK, 1, 0) \
            + jnp.where(kj < k_lo, 1, 0) + jnp.where(kj >= k_hi, 1, 0)
        bad = jnp.where(kj >= 3 * ATT_BLOCK, 0, bad)
        neg = jnp.tile(jnp.where(bad > 0, -1e30, 0.0).astype(F32), (ATT_GROUP, 1))
    lane = lax.shift_right_logical(lax.broadcasted_iota(jnp.int32, (1, ATT_KV_DIM), 1), HEAD_DIM.bit_length() - 1)
    grow = lax.shift_right_logical(lax.broadcasted_iota(jnp.int32, (rows, 1), 0), ATT_BLOCK.bit_length() - 1)
    def kv_head(hk):
        lm_f = (lane == hk).astype(F32)
        lm_b = lm_f.astype(BF16)
        lhs = jnp.concatenate([q[:, ATT_KV_DIM * g:ATT_KV_DIM * (g + 1)] * lm_b for g in range(ATT_GROUP)], axis=0)
        s = _dot_nt(lhs, k_all)
        yield
        if windowed:
            s = s + neg
        sk = jnp.where(grow == 0, sink_ref[ATT_GROUP * hk],
                       jnp.where(grow == 1, sink_ref[ATT_GROUP * hk + 1], sink_ref[ATT_GROUP * hk + 2]))
        m = jnp.maximum(jnp.max(s, axis=1, keepdims=True), sk)
        p = jnp.exp(s - m)
        den = jnp.sum(p, axis=1, keepdims=True) + jnp.exp(sk - m)
        yield
        pv = _dot(p.astype(BF16), v_all) / den
        return [pv[ATT_BLOCK * g:ATT_BLOCK * (g + 1)] * lm_f for g in range(ATT_GROUP)]

    heads = _run_interleaved([kv_head(hk) for hk in range(ATT_KV_HEADS)])
    outs = [sum((head[g] for head in heads[1:]), heads[0][g]) for g in range(ATT_GROUP)]
    o_ref[...] = jnp.concatenate(outs, axis=1).astype(BF16)


def _attention(q, kv, sink, dims, windowed):
    rows = q.shape[0]
    n_b, c_len, t_len, nc = dims["batch"], dims["c_len"], dims["t_len"], dims["nc"]
    seq = t_len if windowed else c_len
    nblk = seq // ATT_BLOCK
    base = (nc // ATT_BLOCK) if windowed else 0
    qrow = lambda bi, n: (base + bi * nblk + n, 0)
    ctx_spec = pl.BlockSpec((c_len, 2 * ATT_KV_DIM), lambda bi, n: (bi, 0))
    blk = lambda f: pl.BlockSpec((ATT_BLOCK, 2 * ATT_KV_DIM), f)
    in_specs = [pl.BlockSpec(memory_space=pltpu.SMEM), pl.BlockSpec((ATT_BLOCK, D_ATT), qrow)]
    args = [sink, q]
    if windowed:
        in_specs += [
            blk(lambda bi, n: (base + bi * nblk + jnp.maximum(n - 1, 0), 0)),
            blk(qrow),
            blk(lambda bi, n: (base + bi * nblk + jnp.minimum(n + 1, nblk - 1), 0)),
        ]
        args += [kv, kv, kv]
    in_specs.append(ctx_spec)
    args.append(kv)
    out_rows = n_b * seq
    return pl.pallas_call(
        functools.partial(_attn_kernel, windowed=windowed, nblk=nblk),
        grid=(n_b, nblk),
        in_specs=in_specs,
        out_specs=pl.BlockSpec((ATT_BLOCK, D_ATT), lambda bi, n: (bi * nblk + n, 0)),
        out_shape=jax.ShapeDtypeStruct((out_rows, D_ATT), BF16),
        compiler_params=_cparams(("arbitrary", "arbitrary")),
        name="window_attention" if windowed else "context_attention",
    )(*args)


def _outproj_kernel(yf_ref, yb_ref, bonus_ref, g_ref, yp_ref, yac_ref, yal_ref, xc_ref, xl_ref, w_ref, wa_ref,
                    lng_ref, lnb_ref, hs_ref, gt_ref, ng_ref, sc_ref, sh_ref, xo_ref, ho_ref, *, ctx_tiles, first_tile):
    hs = hs_ref[...]
    y = yf_ref[...] + yb_ref[...]
    inv_n = 1.0 / HEAD_DIM
    dev = y - _head_sum(y, hs) * inv_n
    var = _head_sum(dev * dev, hs) * inv_n
    yn = dev * lax.rsqrt(var + RWKV_GN_EPS) * lng_ref[...] + lnb_ref[...]
    y_r = ((yn + bonus_ref[...]) * g_ref[...]).astype(BF16)
    acc = (_dot(y_r, w_ref[0:D_RWKV])
           + _dot(yp_ref[...], w_ref[D_RWKV:D_RWKV + D_POOL])
           + _dot(_pick_source(yac_ref, yal_ref, ctx_tiles, first_tile), wa_ref[...]))
    x_new = _pick_source(xc_ref, xl_ref, ctx_tiles, first_tile) + gt_ref[...] * acc
    xo_ref[...] = x_new
    ho_ref[...] = _rms_mod(x_new, ng_ref[...], sc_ref[...], sh_ref[...]).astype(BF16)


def _out_projection(yf, yb, bonus, g, y_p, ya_src, x_src, w_out_b, wa_b, layer, ln_g, ln_b, hs, mod3, mod_base, norm_g,
                    dims, first_tile):
    tm = dims["tm"]
    rows, d = dims["rows"], x_src[0].shape[1]
    ctx_tiles = dims["nc"] // tm
    ya_specs, ya_args = _two_source_specs(*ya_src, tm, ctx_tiles, first_tile)
    x_specs, x_args = _two_source_specs(*x_src, tm, ctx_tiles, first_tile)
    grp = dims["grp"]
    n_tiles = rows // tm - first_tile
    off = lambda w: pl.BlockSpec((tm, w), lambda i: (i + first_tile, 0))

    def mod_spec(k):
        return pl.BlockSpec((None, 1, d), lambda i: (mod_base + grp(i + first_tile, tm) * 6 + k, 0, 0))

    def y_index(ft):
        per_ctx, per_lat = dims["c_len"] // tm, dims["t_len"] // tm
        lt = jnp.maximum(ft - ctx_tiles, 0)
        in_ctx = ft < ctx_tiles
        return (jnp.where(in_ctx, ft // per_ctx, lt // per_lat),
                jnp.where(in_ctx, ft % per_ctx, per_ctx + lt % per_lat), 0)

    y_spec = pl.BlockSpec((None, tm, D_RWKV), lambda i: y_index(i + first_tile))
    vec = lambda w: pl.BlockSpec((1, w), lambda i: (0, 0))
    out_rows = n_tiles * tm
    return pl.pallas_call(
        functools.partial(_outproj_kernel, ctx_tiles=ctx_tiles, first_tile=first_tile),
        grid=(n_tiles,),
        in_specs=[y_spec, y_spec, off(D_RWKV), off(D_RWKV), off(D_POOL)] + ya_specs + x_specs + [
            pl.BlockSpec((None, d, d), lambda i: (layer, 0, 0), pipeline_mode=pl.Buffered(1)),
            pl.BlockSpec((None, D_ATT, d), lambda i: (layer, 0, 0), pipeline_mode=pl.Buffered(1)),
            vec(D_RWKV), vec(D_RWKV),
            pl.BlockSpec((MXU_DIM, MXU_DIM), lambda i: (0, 0)),
            mod_spec(2), vec(d), mod_spec(4), mod_spec(3),
        ],
        out_specs=[pl.BlockSpec((tm, d), lambda i: (i, 0)), pl.BlockSpec((tm, d), lambda i: (i, 0))],
        out_shape=[jax.ShapeDtypeStruct((out_rows, d), F32), jax.ShapeDtypeStruct((out_rows, d), BF16)],
        compiler_params=_cparams(("arbitrary",)),
        name="readout_outproj_norm",
    )(yf, yb, bonus, g, y_p, *ya_args, *x_args, w_out_b, wa_b, ln_g.reshape(1, -1), ln_b.reshape(1, -1), hs,
      mod3, norm_g.reshape(1, d), mod3, mod3)


def _ffn_kernel(h_ref, x_ref, wg_ref, wu_ref, wd_ref, gt_ref, fg_ref, o_ref, acc_ref, *, final):
    j = pl.program_id(1)

    @pl.when(j == 0)
    def _():
        acc_ref[...] = jnp.zeros(acc_ref.shape, F32)

    h = h_ref[...]
    n_sub = wg_ref.shape[1] // MXU_DIM
    slabs = [slice(MXU_DIM * q, MXU_DIM * (q + 1)) for q in range(n_sub)]
    gate_up = [(_dot(h, wg_ref[:, sl]), _dot(h, wu_ref[:, sl])) for sl in slabs]
    part = None
    for sl, (gate, up) in zip(slabs, gate_up):
        act = (gate * jax.nn.sigmoid(gate) * up).astype(BF16)
        p = _dot(act, wd_ref[sl, :])
        part = p if part is None else part + p
    acc_ref[...] += part

    @pl.when(j == pl.num_programs(1) - 1)
    def _():
        x_new = x_ref[...] + gt_ref[...] * acc_ref[...]
        if final:
            ms = jnp.mean(x_new * x_new, axis=-1, keepdims=True)
            x_new = x_new * lax.rsqrt(ms + NORM_EPS) * fg_ref[...]
        o_ref[...] = x_new


def _ffn(h2, x_new, w_gu_b, w_down_b, layer, mod3, mod_base, final_g, dims, first_tile, final):
    rows, d = x_new.shape
    tmf, th = dims["tmf"], dims["th"]
    hidden = w_down_b.shape[1]
    nh = hidden // th
    grp = dims["grp"]
    return pl.pallas_call(
        functools.partial(_ffn_kernel, final=final),
        grid=(rows // tmf, nh),
        in_specs=[
            pl.BlockSpec((tmf, d), lambda i, j: (i, 0)),
            pl.BlockSpec((tmf, d), lambda i, j: (i, 0)),
            pl.BlockSpec((None, d, th), lambda i, j: (layer, 0, j)),
            pl.BlockSpec((None, d, th), lambda i, j: (layer, 0, nh + j)),
            pl.BlockSpec((None, th, d), lambda i, j: (layer, j, 0)),
            pl.BlockSpec((None, 1, d), lambda i, j: (mod_base + grp(i + first_tile, tmf) * 6 + 5, 0, 0)),
            pl.BlockSpec((1, d), lambda i, j: (0, 0)),
        ],
        out_specs=pl.BlockSpec((tmf, d), lambda i, j: (i, 0)),
        out_shape=jax.ShapeDtypeStruct((rows, d), F32),
        scratch_shapes=[pltpu.VMEM((tmf, d), F32)],
        compiler_params=_cparams(("arbitrary", "arbitrary")),
        name="swiglu_ffn_final" if final else "swiglu_ffn",
    )(h2, x_new, w_gu_b, w_gu_b, w_down_b, mod3, final_g.reshape(1, d))


def _rope_tables(dims):
    t_len, nc, n_b = dims["t_len"], dims["nc"], dims["batch"]
    grid_rows = t_len // GRID_W
    n_freq = HEAD_DIM // 4
    inv = ROPE_BASE ** (-jnp.arange(n_freq, dtype=F32) / n_freq)
    ang_r = jnp.arange(grid_rows).astype(F32)[:, None] * inv
    ang_c = jnp.arange(GRID_W).astype(F32)[:, None] * inv

    def per_token(fn):
        by_row = jnp.repeat(fn(ang_r), GRID_W, axis=0)
        by_col = jnp.tile(fn(ang_c), (grid_rows, 1))
        return jnp.tile(jnp.concatenate([by_row, by_row, by_col, by_col], axis=-1), (1, LANES // HEAD_DIM))

    cos, sin = per_token(jnp.cos), per_token(jnp.sin)
    first_half = (jnp.arange(LANES) % 32) < 16
    sin_a = jnp.where(first_half, -sin, 0.0)
    sin_b = jnp.where(first_half, 0.0, sin)
    flat = lambda tab, fill: jnp.concatenate([jnp.full((nc, LANES), fill, F32), jnp.tile(tab, (n_b, 1))], axis=0)
    return flat(cos, 1.0), flat(sin_a, 0.0), flat(sin_b, 0.0)


def _block_diag2(w):
    z = jnp.zeros_like(w[0])
    return jnp.concatenate([jnp.concatenate([w[0], z], axis=1), jnp.concatenate([z, w[1]], axis=1)], axis=0)


def kernel(x, c, ctx, c_ctx, ada_w, ada_b, norm_mix_g, norm_ffn_g, w_in, rwkv_mu, rwkv_w0, rwkv_w_up, rwkv_a0,
           rwkv_a_up, rwkv_g_up, rwkv_k_k, rwkv_k_a, rwkv_r_k, rwkv_ln_g, rwkv_ln_b, pool_w, pool_scale,
           attn_sink, w_out, ffn_w_gu, ffn_w_down, final_norm_g):
    n_b, t_len, d = x.shape
    c_len = ctx.shape[1]
    depth = w_in.shape[0]
    nc = n_b * c_len
    rows = nc + n_b * t_len
    tm = 256 if c_len % 256 == 0 else c_len
    tmf = nc
    assert n_b + 1 <= SUBLANES and c_len % tm == 0 and t_len % tm == 0 and t_len % tmf == 0
    assert tm % ATT_BLOCK == 0 and c_len % CHUNK == 0 and t_len % GRID_W == 0 and d == w_out.shape[1]

    def grp(i, tile):
        r0 = i * tile
        return jnp.where(r0 < nc, 0, 1 + jnp.maximum(r0 - nc, 0) // t_len)

    hidden = ffn_w_down.shape[1]
    th = 512 if hidden % 512 == 0 else hidden
    dims = dict(batch=n_b, c_len=c_len, t_len=t_len, nc=nc, rows=rows, tm=tm, tmf=tmf, th=th, grp=grp)

    cvec = jnp.concatenate([c_ctx[None, :], c, jnp.zeros((SUBLANES - 1 - n_b, d), F32)], axis=0)
    mod = _modulation(cvec, ada_w, ada_b)
    mod3 = mod.reshape(depth * SUBLANES * 6, 1, d)

    x_src = (ctx.reshape(nc, d), x.reshape(n_b * t_len, d), 0)
    rope = _rope_tables(dims)
    hs = jnp.asarray(np.kron(np.eye(HEADS_PER_GROUP), np.ones((HEAD_DIM, HEAD_DIM))), BF16)

    w_in_b = w_in.astype(BF16)
    wq_b = w_in_b[:, :, OFF_AQ:OFF_AK].reshape(depth, d, ATT_KV_HEADS, ATT_GROUP, HEAD_DIM)
    wq_b = wq_b.transpose(0, 1, 3, 2, 4).reshape(depth, d, D_ATT)
    w_out_b = w_out.astype(BF16)
    wa_b = w_out_b[:, D_RWKV + D_POOL:].reshape(depth, ATT_KV_HEADS, ATT_GROUP, HEAD_DIM, d)
    wa_b = wa_b.transpose(0, 2, 1, 3, 4).reshape(depth, D_ATT, d)
    w_gu_b = ffn_w_gu.astype(BF16)
    w_down_b = ffn_w_down.astype(BF16)

    out = None
    for l in range(depth):
        last = l == depth - 1
        mod_base = l * SUBLANES * 6
        rw = dict(
            mu=rwkv_mu[l].reshape(1, RWKV_COLS),
            w0=rwkv_w0[l].reshape(1, 2 * D_RWKV),
            w_up=_block_diag2(rwkv_w_up[l]).astype(BF16),
            a0=rwkv_a0[l].reshape(1, 2 * D_RWKV),
            a_up=_block_diag2(rwkv_a_up[l]).astype(BF16),
            g_up=rwkv_g_up[l].astype(BF16),
            k_k=rwkv_k_k[l].reshape(1, D_RWKV),
            k_a=rwkv_k_a[l].reshape(1, D_RWKV),
            r_k=rwkv_r_k[l].reshape(1, D_RWKV),
        )

        z_r, z_p, q, kv = _in_projection(x_src, mod3, mod_base, norm_mix_g[l], w_in_b, wq_b, l, rope, dims)
        r, v, av, g, bonus, kd, b, lw = _rwkv_prepare(z_r, rw, hs, dims)
        yf, yb = _rwkv_scan(r, v, av, kd, b, lw, dims)
        y_p = _pool(z_p, pool_w[l], pool_scale[l], dims)
        ya_lat = _attention(q, kv, attn_sink[l], dims, windowed=True)
        ya_ctx = ya_lat if last else _attention(q, kv, attn_sink[l], dims, windowed=False)
        first_tile = nc // tm if last else 0
        x_new, h2 = _out_projection(yf, yb, bonus, g, y_p, (ya_ctx, ya_lat, 0), x_src, w_out_b, wa_b, l,
                                    rwkv_ln_g[l], rwkv_ln_b[l], hs, mod3, mod_base, norm_ffn_g[l], dims, first_tile)
        x_next = _ffn(h2, x_new, w_gu_b, w_down_b, l, mod3, mod_base, final_norm_g, dims,
                      first_tile * tm // tmf, final=last)
        if last:
            out = x_next.reshape(n_b, t_len, d)
        else:
            x_src = (x_next, x_next, nc // tm)
    return out
```

```python
import functools

import numpy as np
import jax
import jax.numpy as jnp
from jax import lax
from jax.experimental import pallas as pl
from jax.experimental.pallas import tpu as pltpu

F32 = jnp.float32
BF16 = jnp.bfloat16

HEAD_DIM = 64
RWKV_HEADS = 12
D_RWKV = RWKV_HEADS * HEAD_DIM
LORA_RANK = 64
GATE_RANK = 128
D_POOL = 512
POOL_WINDOWS = (2, 4, 8, 16)
POOL_GROUP_DIM = 128
POOL_HALO = 8
ATT_Q_HEADS = 12
ATT_KV_HEADS = 4
ATT_GROUP = ATT_Q_HEADS // ATT_KV_HEADS
D_ATT = ATT_Q_HEADS * HEAD_DIM
ATT_KV_DIM = ATT_KV_HEADS * HEAD_DIM
ATT_BLOCK = 128
GRID_W = 64
ROPE_BASE = 10000.0
NORM_EPS = 1e-6
RWKV_GN_EPS = 64e-5

OFF_K = D_RWKV
OFF_V = 2 * D_RWKV
OFF_WD = 3 * D_RWKV
OFF_AD = OFF_WD + 2 * LORA_RANK
OFF_GD = OFF_AD + 2 * LORA_RANK
RWKV_COLS = OFF_GD + GATE_RANK
OFF_POOL = RWKV_COLS
OFF_AQ = OFF_POOL + D_POOL
OFF_AK = OFF_AQ + D_ATT
OFF_AV = OFF_AK + ATT_KV_DIM
D_IN = OFF_AV + ATT_KV_DIM

LANES = 128
SUBLANES = 8
MXU_DIM = 256
HEADS_PER_GROUP = MXU_DIM // HEAD_DIM
N_GROUPS = D_RWKV // MXU_DIM
CHUNK = MXU_DIM // HEADS_PER_GROUP
INV_BASE = 8
N_INV_MASKS = (CHUNK // INV_BASE).bit_length() - 2
VMEM_LIMIT = 56 * 1024 * 1024


def _cparams(sem, vmem=VMEM_LIMIT):
    return pltpu.CompilerParams(dimension_semantics=sem, vmem_limit_bytes=vmem)


def _dot(a, b):
    return jnp.dot(a, b, preferred_element_type=F32)


def _dot_nt(a, b):
    return lax.dot_general(a, b, (((1,), (1,)), ((), ())), preferred_element_type=F32)


def _dot_tn(a, b):
    return lax.dot_general(a, b, (((0,), (0,)), ((), ())), preferred_element_type=F32)


def _split3(x):
    x1 = x.astype(BF16)
    r1 = x - x1.astype(F32)
    x2 = r1.astype(BF16)
    x3 = (r1 - x2.astype(F32)).astype(BF16)
    return x1, x2, x3


def _dot_exact_rhs(x, m_bf16):
    x1, x2, x3 = _split3(x)
    return _dot(x1, m_bf16) + _dot(x2, m_bf16) + _dot(x3, m_bf16)


def _dot_exact_lhs(m_bf16, x):
    x1, x2, x3 = _split3(x)
    return _dot(m_bf16, x1) + _dot(m_bf16, x2) + _dot(m_bf16, x3)


def _head_sum(x, hs):
    parts = [_dot_exact_rhs(x[:, MXU_DIM * g:MXU_DIM * (g + 1)], hs) for g in range(N_GROUPS)]
    return jnp.concatenate(parts, axis=1)


def _rms_mod(x, g, sc, sh):
    ms = jnp.mean(x * x, axis=-1, keepdims=True)
    return x * lax.rsqrt(ms + NORM_EPS) * g * (1.0 + sc) + sh


def _mod_kernel(c_ref, w_ref, b_ref, o_ref):
    c = c_ref[...]
    s = (c * jax.nn.sigmoid(c)).astype(BF16)
    o_ref[...] = _dot(s, w_ref[...].astype(BF16)) + b_ref[...]


def _modulation(cvec, ada_w, ada_b):
    n_layers, d, n6 = ada_w.shape
    tn = 1024
    return pl.pallas_call(
        _mod_kernel,
        grid=(n_layers, n6 // tn),
        in_specs=[
            pl.BlockSpec((SUBLANES, d), lambda l, j: (0, 0)),
            pl.BlockSpec((None, d, tn), lambda l, j: (l, 0, j)),
            pl.BlockSpec((None, 1, tn), lambda l, j: (l, 0, j)),
        ],
        out_specs=pl.BlockSpec((None, SUBLANES, tn), lambda l, j: (l, 0, j)),
        out_shape=jax.ShapeDtypeStruct((n_layers, SUBLANES, n6), F32),
        compiler_params=_cparams(("arbitrary", "arbitrary")),
        name="adaln_modulation",
    )(cvec, ada_w, ada_b.reshape(n_layers, 1, n6))


def _rope(z, cos, sin_a, sin_b):
    outs = []
    for j in range(z.shape[1] // LANES):
        zb = z[:, LANES * j:LANES * (j + 1)]
        outs.append(zb * cos + pltpu.roll(zb, LANES - 16, axis=1) * sin_a + pltpu.roll(zb, 16, axis=1) * sin_b)
    return jnp.concatenate(outs, axis=1)


def _two_source_specs(src_ctx, src_lat, lat_first_tile, tm, ctx_tiles, first_tile):
    width = src_ctx.shape[1]
    spec_c = pl.BlockSpec((tm, width), lambda i: (jnp.minimum(i + first_tile, ctx_tiles - 1), 0))
    spec_l = pl.BlockSpec((tm, width), lambda i: (jnp.maximum(i + first_tile - ctx_tiles, 0) + lat_first_tile, 0))
    return [spec_c, spec_l], [src_ctx, src_lat]


def _pick_source(c_ref, l_ref, ctx_tiles, first_tile):
    return jnp.where(pl.program_id(0) + first_tile < ctx_tiles, c_ref[...], l_ref[...])


def _inproj_kernel(xc_ref, xl_ref, g_ref, sc_ref, sh_ref, w_ref, wq_ref, cos_ref, sa_ref, sb_ref,
                   zr_ref, zp_ref, q_ref, kv_ref, *, ctx_tiles):
    x = _pick_source(xc_ref, xl_ref, ctx_tiles, 0)
    h = _rms_mod(x, g_ref[...], sc_ref[...], sh_ref[...]).astype(BF16)
    zr_ref[...] = _dot(h, w_ref[:, 0:RWKV_COLS])
    zp_ref[...] = _dot(h, w_ref[:, OFF_POOL:OFF_AQ])
    cos, sa, sb = cos_ref[...], sa_ref[...], sb_ref[...]
    zq = _dot(h, wq_ref[...])
    q_ref[...] = (_rope(zq, cos, sa, sb) * (HEAD_DIM ** -0.5)).astype(BF16)
    zk = _dot(h, w_ref[:, OFF_AK:OFF_AV])
    zv = _dot(h, w_ref[:, OFF_AV:D_IN])
    kv_ref[...] = jnp.concatenate([_rope(zk, cos, sa, sb), zv], axis=1).astype(BF16)


def _in_projection(x_src, mod3, mod_base, norm_g, w_in_b, wq_b, layer, rope, dims):
    tm = dims["tm"]
    rows, d = dims["rows"], x_src[0].shape[1]
    ctx_tiles = dims["nc"] // tm
    x_specs, x_args = _two_source_specs(*x_src, tm, ctx_tiles, 0)
    grp = dims["grp"]
    cos, sin_a, sin_b = rope

    def mod_spec(k):
        return pl.BlockSpec((None, 1, d), lambda i: (mod_base + grp(i, tm) * 6 + k, 0, 0))

    row_spec = lambda w: pl.BlockSpec((tm, w), lambda i: (i, 0))
    return pl.pallas_call(
        functools.partial(_inproj_kernel, ctx_tiles=ctx_tiles),
        grid=(rows // tm,),
        in_specs=x_specs + [
            pl.BlockSpec((1, d), lambda i: (0, 0)),
            mod_spec(1), mod_spec(0),
            pl.BlockSpec((None, d, D_IN), lambda i: (layer, 0, 0), pipeline_mode=pl.Buffered(1)),
            pl.BlockSpec((None, d, D_ATT), lambda i: (layer, 0, 0), pipeline_mode=pl.Buffered(1)),
            row_spec(LANES), row_spec(LANES), row_spec(LANES),
        ],
        out_specs=[row_spec(RWKV_COLS), row_spec(D_POOL), row_spec(D_ATT), row_spec(2 * ATT_KV_DIM)],
        out_shape=[
            jax.ShapeDtypeStruct((rows, RWKV_COLS), F32),
            jax.ShapeDtypeStruct((rows, D_POOL), F32),
            jax.ShapeDtypeStruct((rows, D_ATT), BF16),
            jax.ShapeDtypeStruct((rows, 2 * ATT_KV_DIM), BF16),
        ],
        compiler_params=_cparams(("arbitrary",)),
        name="norm_inproj_rope",
    )(*x_args, norm_g.reshape(1, d), mod3, mod3, w_in_b, wq_b, cos, sin_a, sin_b)


def _seq_edges(i, tm, dims):
    nc, c_len, t_len = dims["nc"], dims["c_len"], dims["t_len"]
    r0 = i * tm
    is_ctx = r0 < nc
    seq_len = jnp.where(is_ctx, c_len, t_len)
    pos0 = jnp.where(is_ctx, lax.rem(r0, c_len), lax.rem(jnp.maximum(r0 - nc, 0), t_len))
    return pos0, seq_len, pos0 == 0, pos0 + tm == seq_len


def _prep_kernel(z_ref, zp_ref, zn_ref, mu_ref, w0_ref, wup_ref, a0_ref, aup_ref, gup_ref,
                 kk_ref, ka_ref, rk_ref, hs_ref,
                 rva_ref, g_ref, bonus_ref, kdb_ref, lw_ref, *, tm, dims):
    i = pl.program_id(0)
    _, _, first, last = _seq_edges(i, tm, dims)
    z = z_ref[...]
    prev_row = jnp.where(first, 0.0, zp_ref[SUBLANES - 1:SUBLANES, :])
    next_row = jnp.where(last, 0.0, zn_ref[0:1, :])
    row = lax.broadcasted_iota(jnp.int32, (tm, 1), 0)
    z_m1 = jnp.where(row == 0, prev_row, pltpu.roll(z, 1, axis=0))
    z_p1 = jnp.where(row == tm - 1, next_row, pltpu.roll(z, tm - 1, axis=0))
    zs = z + mu_ref[...] * (0.5 * (z_m1 + z_p1) - z)

    r = zs[:, 0:OFF_K]
    k = zs[:, OFF_K:OFF_V]
    v = zs[:, OFF_V:OFF_WD]
    wd = zs[:, OFF_WD:OFF_AD]
    ad = zs[:, OFF_AD:OFF_GD]
    gd = zs[:, OFF_GD:RWKV_COLS]

    w_pre = w0_ref[...] + _dot(jnp.tanh(wd).astype(BF16), wup_ref[...])
    lw = -float(np.exp(-0.5)) * jax.nn.sigmoid(w_pre)
    a = jax.nn.sigmoid(a0_ref[...] + _dot(ad.astype(BF16), aup_ref[...]))
    g_ref[...] = _dot(jax.nn.sigmoid(gd).astype(BF16), gup_ref[...])

    hs = hs_ref[...]
    kx = k * kk_ref[...]
    kn = kx / jnp.maximum(jnp.sqrt(_head_sum(kx * kx, hs)), 1e-12)
    rk = r * rk_ref[...]
    ka = ka_ref[...]
    dot_rk = None
    for d in range(2):
        a_d = a[:, D_RWKV * d:D_RWKV * (d + 1)]
        kd = k * (1.0 + (a_d - 1.0) * ka)
        kdb_ref[d, :, 0:D_RWKV] = kd.astype(BF16)
        kdb_ref[d, :, D_RWKV:2 * D_RWKV] = (kn * a_d).astype(BF16)
        lw_ref[d] = lw[:, D_RWKV * d:D_RWKV * (d + 1)]
        dot_rk = rk * kd if dot_rk is None else dot_rk + rk * kd
    rva_ref[:, 0:D_RWKV] = r.astype(BF16)
    rva_ref[:, D_RWKV:2 * D_RWKV] = v.astype(BF16)
    rva_ref[:, 2 * D_RWKV:3 * D_RWKV] = (-kn).astype(BF16)
    bonus_ref[...] = _head_sum(dot_rk, hs) * v


def _rwkv_prepare(z_r, p, hs, dims):
    rows = z_r.shape[0]
    tm = dims["tm"]
    nblk8 = rows // SUBLANES
    per = tm // SUBLANES
    full = lambda shape: pl.BlockSpec(shape, lambda i: (0,) * len(shape))
    row_spec = lambda n: pl.BlockSpec((tm, n * D_RWKV), lambda i: (i, 0))
    dir_spec = lambda n: pl.BlockSpec((2, tm, n * D_RWKV), lambda i: (0, i, 0))
    one = lambda n, dt: jax.ShapeDtypeStruct((rows, n * D_RWKV), dt)
    two = lambda n, dt: jax.ShapeDtypeStruct((2, rows, n * D_RWKV), dt)
    return pl.pallas_call(
        functools.partial(_prep_kernel, tm=tm, dims=dims),
        grid=(rows // tm,),
        in_specs=[
            pl.BlockSpec((tm, RWKV_COLS), lambda i: (i, 0)),
            pl.BlockSpec((SUBLANES, RWKV_COLS), lambda i: (jnp.maximum(i * per - 1, 0), 0)),
            pl.BlockSpec((SUBLANES, RWKV_COLS), lambda i: (jnp.minimum((i + 1) * per, nblk8 - 1), 0)),
            full((1, RWKV_COLS)),
            full((1, 2 * D_RWKV)), full((2 * LORA_RANK, 2 * D_RWKV)),
            full((1, 2 * D_RWKV)), full((2 * LORA_RANK, 2 * D_RWKV)),
            full((GATE_RANK, D_RWKV)),
            full((1, D_RWKV)), full((1, D_RWKV)), full((1, D_RWKV)),
            full((MXU_DIM, MXU_DIM)),
        ],
        out_specs=[row_spec(3), row_spec(1), row_spec(1), dir_spec(2), dir_spec(1)],
        out_shape=[one(3, BF16), one(1, F32), one(1, F32), two(2, BF16), two(1, F32)],
        compiler_params=_cparams(("arbitrary",)),
        name="rwkv_prepare",
    )(z_r, z_r, z_r, p["mu"], p["w0"], p["w_up"], p["a0"], p["a_up"], p["g_up"],
      p["k_k"], p["k_a"], p["r_k"], hs)


def _block_rows_sum(x):
    parts = [x[s:s + INV_BASE] for s in range(0, x.shape[0], INV_BASE)]
    while len(parts) > 1:
        parts = [parts[i] + parts[i + 1] for i in range(0, len(parts), 2)]
    return parts[0]


def _group_bcast(a, m, lane0):
    halves = []
    for h in range(a.shape[1] // LANES):
        z = a[:, LANES * h:LANES * (h + 1)]
        if m:
            z = pltpu.roll(z, LANES - m, axis=1)
        z = z * lane0
        for shift in (1, 2, 4):
            z = z + pltpu.roll(z, shift, axis=1)
        halves.append(z)
    return jnp.concatenate(halves, axis=1)


def _scan_unit(r, v, av, kd, bd, e_c, e_cx, e_nc, e_rt, e_tot, masks, st_ref, d, g):
    w_strict, w_incl, w8_strict, w8, w_off32, bm, bm_b, inv_masks, eye_c, lane0 = masks
    sl = slice(MXU_DIM * g, MXU_DIM * (g + 1))

    def expand(xb):
        return jnp.tile(xb, (HEADS_PER_GROUP, 1)) * bm_b

    r_t = (r[:, sl] * e_c[:, sl]).astype(BF16)
    a_t = (av[:, sl] * e_cx[:, sl]).astype(BF16)
    k_t = (kd[:, sl] * e_nc[:, sl]).astype(BF16)
    b_t = (bd[:, sl] * e_nc[:, sl]).astype(BF16)
    k_h = (kd[:, sl] * e_rt[:, sl]).astype(BF16)
    b_h = (bd[:, sl] * e_rt[:, sl]).astype(BF16)
    v_b = v[:, sl].astype(BF16)
    a_r = jnp.concatenate([a_t, r_t], axis=0)
    eb, ek, ev = expand(b_t), expand(k_t), expand(v_b)
    yield

    g_b = _dot_nt(a_r, eb)
    yield
    g_k = _dot_nt(a_r, ek)
    yield
    a_ab_f = g_b[0:CHUNK]
    a_ab = a_ab_f.astype(BF16) * w_strict
    a_rb = g_b[CHUNK:].astype(BF16) * w_incl
    a_ak = g_k[0:CHUNK].astype(BF16) * w_strict
    a_rk = g_k[CHUNK:].astype(BF16) * w_incl
    g_ab = expand(a_ab)
    d_c = _block_rows_sum(a_ab_f * w8_strict)
    d_cols = [(m, _group_bcast(d_c, m, lane0)) for m in range(d, INV_BASE - 1 + d)]
    from_v = _dot(jnp.concatenate([a_ak, a_rk], axis=0), ev)
    yield

    def horner(x):
        out = eye_c
        for m, col in d_cols:
            out = out + col * jnp.broadcast_to(x[m:m + 1, :], x.shape)
        return out

    t8_c = eye_c + d_c
    for _ in range((INV_BASE - 2) // 2):
        t8_c = horner(t8_c)
    s0 = st_ref[d, g]
    from_state = _dot_nt(a_r, expand(s0.astype(BF16)))
    yield
    for _ in range(INV_BASE - 2 - (INV_BASE - 2) // 2):
        t8_c = horner(t8_c)
    rhs = from_state[0:CHUNK] + from_v[0:CHUNK]
    y_known = from_state[CHUNK:] + from_v[CHUNK:]

    t32 = (jnp.tile(t8_c, (CHUNK // INV_BASE, 1)) * w8).astype(BF16)
    for lvl in range(N_INV_MASKS):
        t_off = _dot(t32, g_ab * inv_masks[lvl]).astype(BF16)
        yield
        t32 = t32 + _dot(t_off, expand(t32)).astype(BF16)
        yield
    u32 = _dot(t32, expand(rhs.astype(BF16)))
    yield
    o_u = _dot(a_ab * w_off32, expand(u32.astype(BF16)))
    yield
    u_c = (u32 + _dot(t32, expand(o_u.astype(BF16)))).astype(BF16)
    yield
    y = y_known + _dot(a_rb, expand(u_c))
    yield
    upd = _dot_tn(jnp.concatenate([u_c, v_b], axis=0), jnp.concatenate([b_h, k_h], axis=0)) * bm
    upd_w = upd[0:CHUNK]
    for h in range(1, HEADS_PER_GROUP):
        upd_w = upd_w + upd[CHUNK * h:CHUNK * (h + 1)]
    st_ref[d, g] = s0 * e_tot[:, sl] + upd_w
    return y


def _run_interleaved(gens):
    results = [None] * len(gens)
    active = list(range(len(gens)))
    while active:
        for i in list(active):
            try:
                next(gens[i])
            except StopIteration as stop:
                results[i] = stop.value
                active.remove(i)
    return results


N_SCAN_CONSTS = 10
N_SCAN_STREAMS = 3


def _scan_kernel(*refs, n_b):
    n_in = n_b * 2 * N_SCAN_STREAMS
    streams, consts = refs[:n_in], refs[n_in:n_in + N_SCAN_CONSTS]
    y_refs, st_ref = refs[n_in + N_SCAN_CONSTS:-1], refs[-1]
    tri_ref, ws_ref, wi_ref, w8s_ref, w8_ref, woff_ref, bm_ref, im_ref, eyec_ref, lane0_ref = consts

    @pl.when(pl.program_id(0) == 0)
    def _():
        st_ref[...] = jnp.zeros(st_ref.shape, F32)

    bm = bm_ref[...]
    bm_b = bm.astype(BF16)
    inv_masks = [im_ref[lvl] for lvl in range(N_INV_MASKS)]
    w8, eye_c, lane0, w_off32 = w8_ref[...], eyec_ref[...], lane0_ref[...], woff_ref[...]
    units = []
    for bi in range(n_b):
        for d in range(2):
            base = (bi * 2 + d) * N_SCAN_STREAMS
            rva_ref, kdb_ref, lw_ref = streams[base:base + N_SCAN_STREAMS]
            lw = lw_ref[...]
            c = _dot_exact_lhs(tri_ref[d], lw)
            c_tot = jnp.sum(lw, axis=0, keepdims=True)
            e_c, e_cx, e_nc = jnp.exp(c), jnp.exp(c - lw), jnp.exp(-c)
            e_rt, e_tot = jnp.exp(c_tot - c), jnp.exp(c_tot)
            masks = (ws_ref[d], wi_ref[d], w8s_ref[d], w8, w_off32, bm, bm_b, inv_masks, eye_c, lane0)
            r, v, av = (rva_ref[:, D_RWKV * k:D_RWKV * (k + 1)] for k in range(3))
            kd, bd = (kdb_ref[:, D_RWKV * k:D_RWKV * (k + 1)] for k in range(2))
            units += [_scan_unit(r, v, av, kd, bd, e_c, e_cx, e_nc, e_rt, e_tot, masks, st_ref.at[bi], d, g)
                      for g in range(N_GROUPS)]
    ys = _run_interleaved(units)
    for bi in range(n_b):
        for d in range(2):
            first = (bi * 2 + d) * N_GROUPS
            y_refs[d][bi] = jnp.concatenate(ys[first:first + N_GROUPS], axis=1)


def _scan_constants():
    i = np.arange(MXU_DIM)
    blk = (i[:, None] // CHUNK) == (i[None, :] // CHUNK)
    j = np.arange(CHUNK)
    tri = np.stack([j[:, None] >= j[None, :], j[:, None] <= j[None, :]])
    t, s = j[:, None], i[None, :] % CHUNK
    wide_strict = np.stack([t > s, t < s])
    wide_incl = np.stack([t >= s, t <= s])
    wide8 = (t // INV_BASE) == (s // INV_BASE)
    wide8_strict = wide_strict & wide8[None]
    wide_off32 = (t // (CHUNK // 2)) != (s // (CHUNK // 2))
    inv_masks = []
    n = INV_BASE
    while 2 * n < CHUNK:
        inv_masks.append(((i[:, None] // (2 * n)) == (i[None, :] // (2 * n))) & ((i[:, None] // n) != (i[None, :] // n)))
        n *= 2
    assert len(inv_masks) == N_INV_MASKS
    eye_c = np.arange(INV_BASE)[:, None] == (i[None, :] % INV_BASE)
    lane0 = np.broadcast_to((np.arange(LANES) % INV_BASE) == 0, (INV_BASE, LANES))
    f32 = lambda a: jnp.asarray(a, F32)
    b16 = lambda a: jnp.asarray(a, BF16)
    return (b16(tri), b16(wide_strict), b16(wide_incl), f32(wide8_strict), f32(wide8), b16(wide_off32), f32(blk),
            b16(np.stack(inv_masks)), f32(eye_c), f32(lane0))


def _rwkv_scan(rva, kdb, lw, dims):
    rows = rva.shape[0]
    n_b, ncc, nct = dims["batch"], dims["c_len"] // CHUNK, dims["t_len"] // CHUNK
    consts = _scan_constants()

    assert len(consts) == N_SCAN_CONSTS

    def chunk(s, d):
        j_ctx = s if d == 0 else ncc - 1 - s
        j_lat = s - ncc if d == 0 else nct - 1 - (s - ncc)
        return j_ctx, j_lat

    def flat_chunk(bi, d):
        def index(s):
            j_ctx, j_lat = chunk(s, d)
            return jnp.where(s < ncc, bi * ncc + j_ctx, n_b * ncc + bi * nct + j_lat)
        return index

    in_specs, args = [], []
    for bi in range(n_b):
        for d in range(2):
            idx = flat_chunk(bi, d)
            per_dir = lambda n, idx=idx, d=d: pl.BlockSpec((None, CHUNK, n * D_RWKV), lambda s: (d, idx(s), 0))
            in_specs += [pl.BlockSpec((CHUNK, 3 * D_RWKV), lambda s, idx=idx: (idx(s), 0)), per_dir(2), per_dir(1)]
            args += [rva, kdb, lw]
    in_specs += [pl.BlockSpec(a.shape, lambda s, nd=a.ndim: (0,) * nd) for a in consts]

    def y_spec(d):
        def index(s):
            j_ctx, j_lat = chunk(s, d)
            return (0, jnp.where(s < ncc, j_ctx, ncc + j_lat), 0)
        return pl.BlockSpec((n_b, CHUNK, D_RWKV), index)

    y_shape = jax.ShapeDtypeStruct((n_b, rows // n_b, D_RWKV), F32)
    return pl.pallas_call(
        functools.partial(_scan_kernel, n_b=n_b),
        grid=(ncc + nct,),
        in_specs=in_specs,
        out_specs=[y_spec(0), y_spec(1)],
        out_shape=[y_shape, y_shape],
        scratch_shapes=[pltpu.VMEM((n_b, 2, N_GROUPS, HEAD_DIM, MXU_DIM), F32)],
        compiler_params=_cparams(("arbitrary",)),
        name="rwkv_chunk_scan",
    )(*args, *consts)


def _pool_kernel(z_ref, zp_ref, zn_ref, w_ref, sc_ref, o_ref, *, tm, dims):
    i = pl.program_id(0)
    pos0, seq_len, first, last = _seq_edges(i, tm, dims)
    prev = jnp.where(first, 0.0, zp_ref[...])
    nxt = jnp.where(last, 0.0, zn_ref[...])
    ext = jnp.concatenate([prev, z_ref[...], nxt], axis=0)
    n = tm + 2 * POOL_HALO
    t = pos0 + lax.broadcasted_iota(jnp.int32, (tm, 1), 0)
    outs = []
    for gi, w in enumerate(POOL_WINDOWS):
        e = ext[:, POOL_GROUP_DIM * gi:POOL_GROUP_DIM * (gi + 1)]
        acc, width = e, 1
        while width < w:
            acc = acc + pltpu.roll(acc, n - width, axis=0)
            width *= 2
        win = pltpu.roll(acc, w // 2, axis=0)[POOL_HALO:POOL_HALO + tm]
        lo = jnp.clip(t - w // 2, 0, seq_len)
        hi = jnp.clip(t - w // 2 + w, 0, seq_len)
        y = win / (hi - lo).astype(F32) - e[POOL_HALO:POOL_HALO + tm]
        outs.append(_dot(y.astype(BF16), w_ref[gi].astype(BF16)))
    o_ref[...] = (jnp.concatenate(outs, axis=1) * sc_ref[...]).astype(BF16)


def _pool(z_p, pool_w, pool_scale, dims):
    rows = z_p.shape[0]
    tm = dims["tm"]
    nblk8 = rows // SUBLANES
    per = tm // SUBLANES
    return pl.pallas_call(
        functools.partial(_pool_kernel, tm=tm, dims=dims),
        grid=(rows // tm,),
        in_specs=[
            pl.BlockSpec((tm, D_POOL), lambda i: (i, 0)),
            pl.BlockSpec((SUBLANES, D_POOL), lambda i: (jnp.maximum(i * per - 1, 0), 0)),
            pl.BlockSpec((SUBLANES, D_POOL), lambda i: (jnp.minimum((i + 1) * per, nblk8 - 1), 0)),
            pl.BlockSpec(pool_w.shape, lambda i: (0, 0, 0)),
            pl.BlockSpec((1, D_POOL), lambda i: (0, 0)),
        ],
        out_specs=pl.BlockSpec((tm, D_POOL), lambda i: (i, 0)),
        out_shape=jax.ShapeDtypeStruct((rows, D_POOL), BF16),
        compiler_params=_cparams(("arbitrary",)),
        name="multiscale_pool",
    )(z_p, z_p, z_p, pool_w, pool_scale.reshape(1, D_POOL))


def _attn_kernel(sink_ref, q_ref, *rest, windowed, nblk):
    if windowed:
        kp_ref, kc_ref, kn_ref, kx_ref, o_ref = rest
        kv = jnp.concatenate([kp_ref[...], kc_ref[...], kn_ref[...], kx_ref[...]], axis=0)
    else:
        kx_ref, o_ref = rest
        kv = kx_ref[...]
    n = pl.program_id(1)
    q = q_ref[...]
    k_all, v_all = kv[:, 0:ATT_KV_DIM], kv[:, ATT_KV_DIM:2 * ATT_KV_DIM]
    nk = kv.shape[0]
    rows = ATT_GROUP * ATT_BLOCK
    if windowed:
        qi = lax.broadcasted_iota(jnp.int32, (ATT_BLOCK, nk), 0)
        kj = lax.broadcasted_iota(jnp.int32, (ATT_BLOCK, nk), 1)
        rel = kj - ATT_BLOCK - qi
        k_lo = jnp.where(n == 0, ATT_BLOCK, 0)
        k_hi = jnp.where(n == nblk - 1, 2 * ATT_BLOCK, 3 * ATT_BLOCK)
        bad = jnp.where(rel > ATT_BLOCK, 1, 0) + jnp.where(rel < -ATT_BLOCK, 1, 0) \
            + jnp.where(kj < k_lo, 1, 0) + jnp.where(kj >= k_hi, 1, 0)
        bad = jnp.where(kj >= 3 * ATT_BLOCK, 0, bad)
        neg = jnp.tile(jnp.where(bad > 0, -1e30, 0.0).astype(F32), (ATT_GROUP, 1))
    lane = lax.shift_right_logical(lax.broadcasted_iota(jnp.int32, (1, ATT_KV_DIM), 1), HEAD_DIM.bit_length() - 1)
    grow = lax.shift_right_logical(lax.broadcasted_iota(jnp.int32, (rows, 1), 0), ATT_BLOCK.bit_length() - 1)
    def kv_head(hk):
        lm_f = (lane == hk).astype(F32)
        lm_b = lm_f.astype(BF16)
        lhs = jnp.concatenate([q[:, ATT_KV_DIM * g:ATT_KV_DIM * (g + 1)] * lm_b for g in range(ATT_GROUP)], axis=0)
        s = _dot_nt(lhs, k_all)
        yield
        if windowed:
            s = s + neg
        sk = jnp.where(grow == 0, sink_ref[ATT_GROUP * hk],
                       jnp.where(grow == 1, sink_ref[ATT_GROUP * hk + 1], sink_ref[ATT_GROUP * hk + 2]))
        m = jnp.maximum(jnp.max(s, axis=1, keepdims=True), sk)
        p = jnp.exp(s - m)
        den = jnp.sum(p, axis=1, keepdims=True) + jnp.exp(sk - m)
        yield
        pv = _dot(p.astype(BF16), v_all) / den
        return [pv[ATT_BLOCK * g:ATT_BLOCK * (g + 1)] * lm_f for g in range(ATT_GROUP)]

    heads = _run_interleaved([kv_head(hk) for hk in range(ATT_KV_HEADS)])
    outs = [sum((head[g] for head in heads[1:]), heads[0][g]) for g in range(ATT_GROUP)]
    o_ref[...] = jnp.concatenate(outs, axis=1).astype(BF16)


def _attention(q, kv, sink, dims, windowed):
    rows = q.shape[0]
    n_b, c_len, t_len, nc = dims["batch"], dims["c_len"], dims["t_len"], dims["nc"]
    seq = t_len if windowed else c_len
    nblk = seq // ATT_BLOCK
    base = (nc // ATT_BLOCK) if windowed else 0
    qrow = lambda bi, n: (base + bi * nblk + n, 0)
    ctx_spec = pl.BlockSpec((c_len, 2 * ATT_KV_DIM), lambda bi, n: (bi, 0))
    blk = lambda f: pl.BlockSpec((ATT_BLOCK, 2 * ATT_KV_DIM), f)
    in_specs = [pl.BlockSpec(memory_space=pltpu.SMEM), pl.BlockSpec((ATT_BLOCK, D_ATT), qrow)]
    args = [sink, q]
    if windowed:
        in_specs += [
            blk(lambda bi, n: (base + bi * nblk + jnp.maximum(n - 1, 0), 0)),
            blk(qrow),
            blk(lambda bi, n: (base + bi * nblk + jnp.minimum(n + 1, nblk - 1), 0)),
        ]
        args += [kv, kv, kv]
    in_specs.append(ctx_spec)
    args.append(kv)
    out_rows = n_b * seq
    return pl.pallas_call(
        functools.partial(_attn_kernel, windowed=windowed, nblk=nblk),
        grid=(n_b, nblk),
        in_specs=in_specs,
        out_specs=pl.BlockSpec((ATT_BLOCK, D_ATT), lambda bi, n: (bi * nblk + n, 0)),
        out_shape=jax.ShapeDtypeStruct((out_rows, D_ATT), BF16),
        compiler_params=_cparams(("arbitrary", "arbitrary")),
        name="window_attention" if windowed else "context_attention",
    )(*args)


def _outproj_kernel(yf_ref, yb_ref, bonus_ref, g_ref, yp_ref, yac_ref, yal_ref, xc_ref, xl_ref, w_ref, wa_ref,
                    lng_ref, lnb_ref, hs_ref, gt_ref, ng_ref, sc_ref, sh_ref, xo_ref, ho_ref, *, ctx_tiles, first_tile):
    hs = hs_ref[...]
    y = yf_ref[...] + yb_ref[...]
    inv_n = 1.0 / HEAD_DIM
    dev = y - _head_sum(y, hs) * inv_n
    var = _head_sum(dev * dev, hs) * inv_n
    yn = dev * lax.rsqrt(var + RWKV_GN_EPS) * lng_ref[...] + lnb_ref[...]
    y_r = ((yn + bonus_ref[...]) * g_ref[...]).astype(BF16)
    acc = (_dot(y_r, w_ref[0:D_RWKV])
           + _dot(yp_ref[...], w_ref[D_RWKV:D_RWKV + D_POOL])
           + _dot(_pick_source(yac_ref, yal_ref, ctx_tiles, first_tile), wa_ref[...]))
    x_new = _pick_source(xc_ref, xl_ref, ctx_tiles, first_tile) + gt_ref[...] * acc
    xo_ref[...] = x_new
    ho_ref[...] = _rms_mod(x_new, ng_ref[...], sc_ref[...], sh_ref[...]).astype(BF16)


def _out_projection(yf, yb, bonus, g, y_p, ya_src, x_src, w_out_b, wa_b, layer, ln_g, ln_b, hs, mod3, mod_base, norm_g,
                    dims, first_tile):
    tm = dims["tm"]
    rows, d = dims["rows"], x_src[0].shape[1]
    ctx_tiles = dims["nc"] // tm
    ya_specs, ya_args = _two_source_specs(*ya_src, tm, ctx_tiles, first_tile)
    x_specs, x_args = _two_source_specs(*x_src, tm, ctx_tiles, first_tile)
    grp = dims["grp"]
    n_tiles = rows // tm - first_tile
    off = lambda w: pl.BlockSpec((tm, w), lambda i: (i + first_tile, 0))

    def mod_spec(k):
        return pl.BlockSpec((None, 1, d), lambda i: (mod_base + grp(i + first_tile, tm) * 6 + k, 0, 0))

    def y_index(ft):
        per_ctx, per_lat = dims["c_len"] // tm, dims["t_len"] // tm
        lt = jnp.maximum(ft - ctx_tiles, 0)
        in_ctx = ft < ctx_tiles
        return (jnp.where(in_ctx, ft // per_ctx, lt // per_lat),
                jnp.where(in_ctx, ft % per_ctx, per_ctx + lt % per_lat), 0)

    y_spec = pl.BlockSpec((None, tm, D_RWKV), lambda i: y_index(i + first_tile))
    vec = lambda w: pl.BlockSpec((1, w), lambda i: (0, 0))
    out_rows = n_tiles * tm
    return pl.pallas_call(
        functools.partial(_outproj_kernel, ctx_tiles=ctx_tiles, first_tile=first_tile),
        grid=(n_tiles,),
        in_specs=[y_spec, y_spec, off(D_RWKV), off(D_RWKV), off(D_POOL)] + ya_specs + x_specs + [
            pl.BlockSpec((None, d, d), lambda i: (layer, 0, 0), pipeline_mode=pl.Buffered(1)),
            pl.BlockSpec((None, D_ATT, d), lambda i: (layer, 0, 0), pipeline_mode=pl.Buffered(1)),
            vec(D_RWKV), vec(D_RWKV),
            pl.BlockSpec((MXU_DIM, MXU_DIM), lambda i: (0, 0)),
            mod_spec(2), vec(d), mod_spec(4), mod_spec(3),
        ],
        out_specs=[pl.BlockSpec((tm, d), lambda i: (i, 0)), pl.BlockSpec((tm, d), lambda i: (i, 0))],
        out_shape=[jax.ShapeDtypeStruct((out_rows, d), F32), jax.ShapeDtypeStruct((out_rows, d), BF16)],
        compiler_params=_cparams(("arbitrary",)),
        name="readout_outproj_norm",
    )(yf, yb, bonus, g, y_p, *ya_args, *x_args, w_out_b, wa_b, ln_g.reshape(1, -1), ln_b.reshape(1, -1), hs,
      mod3, norm_g.reshape(1, d), mod3, mod3)


def _ffn_kernel(h_ref, x_ref, wg_ref, wu_ref, wd_ref, gt_ref, fg_ref, o_ref, acc_ref, *, final):
    j = pl.program_id(1)

    @pl.when(j == 0)
    def _():
        acc_ref[...] = jnp.zeros(acc_ref.shape, F32)

    h = h_ref[...]
    n_sub = wg_ref.shape[1] // MXU_DIM
    slabs = [slice(MXU_DIM * q, MXU_DIM * (q + 1)) for q in range(n_sub)]
    gate_up = [(_dot(h, wg_ref[:, sl]), _dot(h, wu_ref[:, sl])) for sl in slabs]
    part = None
    for sl, (gate, up) in zip(slabs, gate_up):
        act = (gate * jax.nn.sigmoid(gate) * up).astype(BF16)
        p = _dot(act, wd_ref[sl, :])
        part = p if part is None else part + p
    acc_ref[...] += part

    @pl.when(j == pl.num_programs(1) - 1)
    def _():
        x_new = x_ref[...] + gt_ref[...] * acc_ref[...]
        if final:
            ms = jnp.mean(x_new * x_new, axis=-1, keepdims=True)
            x_new = x_new * lax.rsqrt(ms + NORM_EPS) * fg_ref[...]
        o_ref[...] = x_new


def _ffn(h2, x_new, w_gu_b, w_down_b, layer, mod3, mod_base, final_g, dims, first_tile, final):
    rows, d = x_new.shape
    tmf, th = dims["tmf"], dims["th"]
    hidden = w_down_b.shape[1]
    nh = hidden // th
    grp = dims["grp"]
    return pl.pallas_call(
        functools.partial(_ffn_kernel, final=final),
        grid=(rows // tmf, nh),
        in_specs=[
            pl.BlockSpec((tmf, d), lambda i, j: (i, 0)),
            pl.BlockSpec((tmf, d), lambda i, j: (i, 0)),
            pl.BlockSpec((None, d, th), lambda i, j: (layer, 0, j)),
            pl.BlockSpec((None, d, th), lambda i, j: (layer, 0, nh + j)),
            pl.BlockSpec((None, th, d), lambda i, j: (layer, j, 0)),
            pl.BlockSpec((None, 1, d), lambda i, j: (mod_base + grp(i + first_tile, tmf) * 6 + 5, 0, 0)),
            pl.BlockSpec((1, d), lambda i, j: (0, 0)),
        ],
        out_specs=pl.BlockSpec((tmf, d), lambda i, j: (i, 0)),
        out_shape=jax.ShapeDtypeStruct((rows, d), F32),
        scratch_shapes=[pltpu.VMEM((tmf, d), F32)],
        compiler_params=_cparams(("arbitrary", "arbitrary")),
        name="swiglu_ffn_final" if final else "swiglu_ffn",
    )(h2, x_new, w_gu_b, w_gu_b, w_down_b, mod3, final_g.reshape(1, d))


def _rope_tables(dims):
    t_len, nc, n_b = dims["t_len"], dims["nc"], dims["batch"]
    grid_rows = t_len // GRID_W
    n_freq = HEAD_DIM // 4
    inv = ROPE_BASE ** (-jnp.arange(n_freq, dtype=F32) / n_freq)
    ang_r = jnp.arange(grid_rows).astype(F32)[:, None] * inv
    ang_c = jnp.arange(GRID_W).astype(F32)[:, None] * inv

    def per_token(fn):
        by_row = jnp.repeat(fn(ang_r), GRID_W, axis=0)
        by_col = jnp.tile(fn(ang_c), (grid_rows, 1))
        return jnp.tile(jnp.concatenate([by_row, by_row, by_col, by_col], axis=-1), (1, LANES // HEAD_DIM))

    cos, sin = per_token(jnp.cos), per_token(jnp.sin)
    first_half = (jnp.arange(LANES) % 32) < 16
    sin_a = jnp.where(first_half, -sin, 0.0)
    sin_b = jnp.where(first_half, 0.0, sin)
    flat = lambda tab, fill: jnp.concatenate([jnp.full((nc, LANES), fill, F32), jnp.tile(tab, (n_b, 1))], axis=0)
    return flat(cos, 1.0), flat(sin_a, 0.0), flat(sin_b, 0.0)


def _block_diag2(w):
    z = jnp.zeros_like(w[0])
    return jnp.concatenate([jnp.concatenate([w[0], z], axis=1), jnp.concatenate([z, w[1]], axis=1)], axis=0)


def kernel(x, c, ctx, c_ctx, ada_w, ada_b, norm_mix_g, norm_ffn_g, w_in, rwkv_mu, rwkv_w0, rwkv_w_up, rwkv_a0,
           rwkv_a_up, rwkv_g_up, rwkv_k_k, rwkv_k_a, rwkv_r_k, rwkv_ln_g, rwkv_ln_b, pool_w, pool_scale,
           attn_sink, w_out, ffn_w_gu, ffn_w_down, final_norm_g):
    n_b, t_len, d = x.shape
    c_len = ctx.shape[1]
    depth = w_in.shape[0]
    nc = n_b * c_len
    rows = nc + n_b * t_len
    tm = 256 if c_len % 256 == 0 else c_len
    tmf = nc
    assert n_b + 1 <= SUBLANES and c_len % tm == 0 and t_len % tm == 0 and t_len % tmf == 0
    assert tm % ATT_BLOCK == 0 and c_len % CHUNK == 0 and t_len % GRID_W == 0 and d == w_out.shape[1]

    def grp(i, tile):
        r0 = i * tile
        return jnp.where(r0 < nc, 0, 1 + jnp.maximum(r0 - nc, 0) // t_len)

    hidden = ffn_w_down.shape[1]
    th = 512 if hidden % 512 == 0 else hidden
    dims = dict(batch=n_b, c_len=c_len, t_len=t_len, nc=nc, rows=rows, tm=tm, tmf=tmf, th=th, grp=grp)

    cvec = jnp.concatenate([c_ctx[None, :], c, jnp.zeros((SUBLANES - 1 - n_b, d), F32)], axis=0)
    mod = _modulation(cvec, ada_w, ada_b)
    mod3 = mod.reshape(depth * SUBLANES * 6, 1, d)

    x_src = (ctx.reshape(nc, d), x.reshape(n_b * t_len, d), 0)
    rope = _rope_tables(dims)
    hs = jnp.asarray(np.kron(np.eye(HEADS_PER_GROUP), np.ones((HEAD_DIM, HEAD_DIM))), BF16)

    w_in_b = w_in.astype(BF16)
    wq_b = w_in_b[:, :, OFF_AQ:OFF_AK].reshape(depth, d, ATT_KV_HEADS, ATT_GROUP, HEAD_DIM)
    wq_b = wq_b.transpose(0, 1, 3, 2, 4).reshape(depth, d, D_ATT)
    w_out_b = w_out.astype(BF16)
    wa_b = w_out_b[:, D_RWKV + D_POOL:].reshape(depth, ATT_KV_HEADS, ATT_GROUP, HEAD_DIM, d)
    wa_b = wa_b.transpose(0, 2, 1, 3, 4).reshape(depth, D_ATT, d)
    w_gu_b = ffn_w_gu.astype(BF16)
    w_down_b = ffn_w_down.astype(BF16)

    out = None
    for l in range(depth):
        last = l == depth - 1
        mod_base = l * SUBLANES * 6
        rw = dict(
            mu=rwkv_mu[l].reshape(1, RWKV_COLS),
            w0=rwkv_w0[l].reshape(1, 2 * D_RWKV),
            w_up=_block_diag2(rwkv_w_up[l]).astype(BF16),
            a0=rwkv_a0[l].reshape(1, 2 * D_RWKV),
            a_up=_block_diag2(rwkv_a_up[l]).astype(BF16),
            g_up=rwkv_g_up[l].astype(BF16),
            k_k=rwkv_k_k[l].reshape(1, D_RWKV),
            k_a=rwkv_k_a[l].reshape(1, D_RWKV),
            r_k=rwkv_r_k[l].reshape(1, D_RWKV),
        )

        z_r, z_p, q, kv = _in_projection(x_src, mod3, mod_base, norm_mix_g[l], w_in_b, wq_b, l, rope, dims)
        rva, g, bonus, kdb, lw = _rwkv_prepare(z_r, rw, hs, dims)
        yf, yb = _rwkv_scan(rva, kdb, lw, dims)
        y_p = _pool(z_p, pool_w[l], pool_scale[l], dims)
        ya_lat = _attention(q, kv, attn_sink[l], dims, windowed=True)
        ya_ctx = ya_lat if last else _attention(q, kv, attn_sink[l], dims, windowed=False)
        first_tile = nc // tm if last else 0
        x_new, h2 = _out_projection(yf, yb, bonus, g, y_p, (ya_ctx, ya_lat, 0), x_src, w_out_b, wa_b, l,
                                    rwkv_ln_g[l], rwkv_ln_b[l], hs, mod3, mod_base, norm_ffn_g[l], dims, first_tile)
        x_next = _ffn(h2, x_new, w_gu_b, w_down_b, l, mod3, mod_base, final_norm_g, dims,
                      first_tile * tm // tmf, final=last)
        if last:
            out = x_next.reshape(n_b, t_len, d)
        else:
            x_src = (x_next, x_next, nc // tm)
    return out
```

```python
import functools

import numpy as np
import jax
import jax.numpy as jnp
from jax import lax
from jax.experimental import pallas as pl
from jax.experimental.pallas import tpu as pltpu

F32 = jnp.float32
BF16 = jnp.bfloat16

HEAD_DIM = 64
RWKV_HEADS = 12
D_RWKV = RWKV_HEADS * HEAD_DIM
LORA_RANK = 64
GATE_RANK = 128
D_POOL = 512
POOL_WINDOWS = (2, 4, 8, 16)
POOL_GROUP_DIM = 128
POOL_HALO = 8
ATT_Q_HEADS = 12
ATT_KV_HEADS = 4
ATT_GROUP = ATT_Q_HEADS // ATT_KV_HEADS
D_ATT = ATT_Q_HEADS * HEAD_DIM
ATT_KV_DIM = ATT_KV_HEADS * HEAD_DIM
ATT_BLOCK = 128
GRID_W = 64
ROPE_BASE = 10000.0
NORM_EPS = 1e-6
RWKV_GN_EPS = 64e-5

OFF_K = D_RWKV
OFF_V = 2 * D_RWKV
OFF_WD = 3 * D_RWKV
OFF_AD = OFF_WD + 2 * LORA_RANK
OFF_GD = OFF_AD + 2 * LORA_RANK
RWKV_COLS = OFF_GD + GATE_RANK
OFF_POOL = RWKV_COLS
OFF_AQ = OFF_POOL + D_POOL
OFF_AK = OFF_AQ + D_ATT
OFF_AV = OFF_AK + ATT_KV_DIM
D_IN = OFF_AV + ATT_KV_DIM

LANES = 128
SUBLANES = 8
MXU_DIM = 256
HEADS_PER_GROUP = MXU_DIM // HEAD_DIM
N_GROUPS = D_RWKV // MXU_DIM
CHUNK = MXU_DIM // HEADS_PER_GROUP
INV_BASE = 8
N_INV_MASKS = (CHUNK // INV_BASE).bit_length() - 2
VMEM_LIMIT = 56 * 1024 * 1024


def _cparams(sem, vmem=VMEM_LIMIT):
    return pltpu.CompilerParams(dimension_semantics=sem, vmem_limit_bytes=vmem)


def _dot(a, b):
    return jnp.dot(a, b, preferred_element_type=F32)


def _dot_nt(a, b):
    return lax.dot_general(a, b, (((1,), (1,)), ((), ())), preferred_element_type=F32)


def _dot_tn(a, b):
    return lax.dot_general(a, b, (((0,), (0,)), ((), ())), preferred_element_type=F32)


def _split3(x):
    x1 = x.astype(BF16)
    r1 = x - x1.astype(F32)
    x2 = r1.astype(BF16)
    x3 = (r1 - x2.astype(F32)).astype(BF16)
    return x1, x2, x3


def _dot_exact_rhs(x, m_bf16):
    x1, x2, x3 = _split3(x)
    return _dot(x1, m_bf16) + _dot(x2, m_bf16) + _dot(x3, m_bf16)


def _dot_exact_lhs(m_bf16, x):
    x1, x2, x3 = _split3(x)
    return _dot(m_bf16, x1) + _dot(m_bf16, x2) + _dot(m_bf16, x3)


def _head_sum(x, hs):
    parts = [_dot_exact_rhs(x[:, MXU_DIM * g:MXU_DIM * (g + 1)], hs) for g in range(N_GROUPS)]
    return jnp.concatenate(parts, axis=1)


def _rms_mod(x, g, sc, sh):
    ms = jnp.mean(x * x, axis=-1, keepdims=True)
    return x * lax.rsqrt(ms + NORM_EPS) * g * (1.0 + sc) + sh


def _mod_kernel(c_ref, w_ref, b_ref, o_ref):
    c = c_ref[...]
    s = (c * jax.nn.sigmoid(c)).astype(BF16)
    o_ref[...] = _dot(s, w_ref[...].astype(BF16)) + b_ref[...]


def _modulation(cvec, ada_w, ada_b):
    n_layers, d, n6 = ada_w.shape
    tn = 1024
    return pl.pallas_call(
        _mod_kernel,
        grid=(n_layers, n6 // tn),
        in_specs=[
            pl.BlockSpec((SUBLANES, d), lambda l, j: (0, 0)),
            pl.BlockSpec((None, d, tn), lambda l, j: (l, 0, j)),
            pl.BlockSpec((None, 1, tn), lambda l, j: (l, 0, j)),
        ],
        out_specs=pl.BlockSpec((None, SUBLANES, tn), lambda l, j: (l, 0, j)),
        out_shape=jax.ShapeDtypeStruct((n_layers, SUBLANES, n6), F32),
        compiler_params=_cparams(("arbitrary", "arbitrary")),
        name="adaln_modulation",
    )(cvec, ada_w, ada_b.reshape(n_layers, 1, n6))


def _rope(z, cos, sin_a, sin_b):
    outs = []
    for j in range(z.shape[1] // LANES):
        zb = z[:, LANES * j:LANES * (j + 1)]
        outs.append(zb * cos + pltpu.roll(zb, LANES - 16, axis=1) * sin_a + pltpu.roll(zb, 16, axis=1) * sin_b)
    return jnp.concatenate(outs, axis=1)


def _two_source_specs(src_ctx, src_lat, lat_first_tile, tm, ctx_tiles, first_tile):
    width = src_ctx.shape[1]
    spec_c = pl.BlockSpec((tm, width), lambda i: (jnp.minimum(i + first_tile, ctx_tiles - 1), 0))
    spec_l = pl.BlockSpec((tm, width), lambda i: (jnp.maximum(i + first_tile - ctx_tiles, 0) + lat_first_tile, 0))
    return [spec_c, spec_l], [src_ctx, src_lat]


def _pick_source(c_ref, l_ref, ctx_tiles, first_tile):
    return jnp.where(pl.program_id(0) + first_tile < ctx_tiles, c_ref[...], l_ref[...])


def _inproj_kernel(xc_ref, xl_ref, g_ref, sc_ref, sh_ref, w_ref, wq_ref, cos_ref, sa_ref, sb_ref,
                   zr_ref, zp_ref, q_ref, kv_ref, *, ctx_tiles):
    x = _pick_source(xc_ref, xl_ref, ctx_tiles, 0)
    h = _rms_mod(x, g_ref[...], sc_ref[...], sh_ref[...]).astype(BF16)
    zr_ref[...] = _dot(h, w_ref[:, 0:RWKV_COLS])
    zp_ref[...] = _dot(h, w_ref[:, OFF_POOL:OFF_AQ])
    cos, sa, sb = cos_ref[...], sa_ref[...], sb_ref[...]
    zq = _dot(h, wq_ref[...])
    q_ref[...] = (_rope(zq, cos, sa, sb) * (HEAD_DIM ** -0.5)).astype(BF16)
    zk = _dot(h, w_ref[:, OFF_AK:OFF_AV])
    zv = _dot(h, w_ref[:, OFF_AV:D_IN])
    kv_ref[...] = jnp.concatenate([_rope(zk, cos, sa, sb), zv], axis=1).astype(BF16)


def _in_projection(x_src, mod3, mod_base, norm_g, w_in_b, wq_b, layer, rope, dims):
    tm = dims["tm"]
    rows, d = dims["rows"], x_src[0].shape[1]
    ctx_tiles = dims["nc"] // tm
    x_specs, x_args = _two_source_specs(*x_src, tm, ctx_tiles, 0)
    grp = dims["grp"]
    cos, sin_a, sin_b = rope

    def mod_spec(k):
        return pl.BlockSpec((None, 1, d), lambda i: (mod_base + grp(i, tm) * 6 + k, 0, 0))

    row_spec = lambda w: pl.BlockSpec((tm, w), lambda i: (i, 0))
    return pl.pallas_call(
        functools.partial(_inproj_kernel, ctx_tiles=ctx_tiles),
        grid=(rows // tm,),
        in_specs=x_specs + [
            pl.BlockSpec((1, d), lambda i: (0, 0)),
            mod_spec(1), mod_spec(0),
            pl.BlockSpec((None, d, D_IN), lambda i: (layer, 0, 0), pipeline_mode=pl.Buffered(1)),
            pl.BlockSpec((None, d, D_ATT), lambda i: (layer, 0, 0), pipeline_mode=pl.Buffered(1)),
            row_spec(LANES), row_spec(LANES), row_spec(LANES),
        ],
        out_specs=[row_spec(RWKV_COLS), row_spec(D_POOL), row_spec(D_ATT), row_spec(2 * ATT_KV_DIM)],
        out_shape=[
            jax.ShapeDtypeStruct((rows, RWKV_COLS), F32),
            jax.ShapeDtypeStruct((rows, D_POOL), F32),
            jax.ShapeDtypeStruct((rows, D_ATT), BF16),
            jax.ShapeDtypeStruct((rows, 2 * ATT_KV_DIM), BF16),
        ],
        compiler_params=_cparams(("arbitrary",)),
        name="norm_inproj_rope",
    )(*x_args, norm_g.reshape(1, d), mod3, mod3, w_in_b, wq_b, cos, sin_a, sin_b)


def _seq_edges(i, tm, dims):
    nc, c_len, t_len = dims["nc"], dims["c_len"], dims["t_len"]
    r0 = i * tm
    is_ctx = r0 < nc
    seq_len = jnp.where(is_ctx, c_len, t_len)
    pos0 = jnp.where(is_ctx, lax.rem(r0, c_len), lax.rem(jnp.maximum(r0 - nc, 0), t_len))
    return pos0, seq_len, pos0 == 0, pos0 + tm == seq_len


def _prep_kernel(z_ref, zp_ref, zn_ref, mu_ref, w0_ref, wup_ref, a0_ref, aup_ref, gup_ref,
                 kk_ref, ka_ref, rk_ref, hs_ref,
                 rva_ref, g_ref, bonus_ref, kdb_ref, lw_ref, *, tm, dims):
    i = pl.program_id(0)
    _, _, first, last = _seq_edges(i, tm, dims)
    z = z_ref[...]
    prev_row = jnp.where(first, 0.0, zp_ref[SUBLANES - 1:SUBLANES, :])
    next_row = jnp.where(last, 0.0, zn_ref[0:1, :])
    row = lax.broadcasted_iota(jnp.int32, (tm, 1), 0)
    z_m1 = jnp.where(row == 0, prev_row, pltpu.roll(z, 1, axis=0))
    z_p1 = jnp.where(row == tm - 1, next_row, pltpu.roll(z, tm - 1, axis=0))
    zs = z + mu_ref[...] * (0.5 * (z_m1 + z_p1) - z)

    r = zs[:, 0:OFF_K]
    k = zs[:, OFF_K:OFF_V]
    v = zs[:, OFF_V:OFF_WD]
    wd = zs[:, OFF_WD:OFF_AD]
    ad = zs[:, OFF_AD:OFF_GD]
    gd = zs[:, OFF_GD:RWKV_COLS]

    w_pre = w0_ref[...] + _dot(jnp.tanh(wd).astype(BF16), wup_ref[...])
    lw = -float(np.exp(-0.5)) * jax.nn.sigmoid(w_pre)
    a = jax.nn.sigmoid(a0_ref[...] + _dot(ad.astype(BF16), aup_ref[...]))
    g_ref[...] = _dot(jax.nn.sigmoid(gd).astype(BF16), gup_ref[...])

    hs = hs_ref[...]
    kx = k * kk_ref[...]
    kn = kx / jnp.maximum(jnp.sqrt(_head_sum(kx * kx, hs)), 1e-12)
    rk = r * rk_ref[...]
    ka = ka_ref[...]
    dot_rk = None
    for d in range(2):
        a_d = a[:, D_RWKV * d:D_RWKV * (d + 1)]
        kd = k * (1.0 + (a_d - 1.0) * ka)
        kdb_ref[d, :, 0:D_RWKV] = kd.astype(BF16)
        kdb_ref[d, :, D_RWKV:2 * D_RWKV] = (kn * a_d).astype(BF16)
        lw_ref[d] = lw[:, D_RWKV * d:D_RWKV * (d + 1)]
        dot_rk = rk * kd if dot_rk is None else dot_rk + rk * kd
    rva_ref[:, 0:D_RWKV] = r.astype(BF16)
    rva_ref[:, D_RWKV:2 * D_RWKV] = v.astype(BF16)
    rva_ref[:, 2 * D_RWKV:3 * D_RWKV] = (-kn).astype(BF16)
    bonus_ref[...] = _head_sum(dot_rk, hs) * v


def _rwkv_prepare(z_r, p, hs, dims):
    rows = z_r.shape[0]
    tm = dims["tm"]
    nblk8 = rows // SUBLANES
    per = tm // SUBLANES
    full = lambda shape: pl.BlockSpec(shape, lambda i: (0,) * len(shape))
    row_spec = lambda n: pl.BlockSpec((tm, n * D_RWKV), lambda i: (i, 0))
    dir_spec = lambda n: pl.BlockSpec((2, tm, n * D_RWKV), lambda i: (0, i, 0))
    one = lambda n, dt: jax.ShapeDtypeStruct((rows, n * D_RWKV), dt)
    two = lambda n, dt: jax.ShapeDtypeStruct((2, rows, n * D_RWKV), dt)
    return pl.pallas_call(
        functools.partial(_prep_kernel, tm=tm, dims=dims),
        grid=(rows // tm,),
        in_specs=[
            pl.BlockSpec((tm, RWKV_COLS), lambda i: (i, 0)),
            pl.BlockSpec((SUBLANES, RWKV_COLS), lambda i: (jnp.maximum(i * per - 1, 0), 0)),
            pl.BlockSpec((SUBLANES, RWKV_COLS), lambda i: (jnp.minimum((i + 1) * per, nblk8 - 1), 0)),
            full((1, RWKV_COLS)),
            full((1, 2 * D_RWKV)), full((2 * LORA_RANK, 2 * D_RWKV)),
            full((1, 2 * D_RWKV)), full((2 * LORA_RANK, 2 * D_RWKV)),
            full((GATE_RANK, D_RWKV)),
            full((1, D_RWKV)), full((1, D_RWKV)), full((1, D_RWKV)),
            full((MXU_DIM, MXU_DIM)),
        ],
        out_specs=[row_spec(3), row_spec(1), row_spec(1), dir_spec(2), dir_spec(1)],
        out_shape=[one(3, BF16), one(1, F32), one(1, F32), two(2, BF16), two(1, F32)],
        compiler_params=_cparams(("arbitrary",)),
        name="rwkv_prepare",
    )(z_r, z_r, z_r, p["mu"], p["w0"], p["w_up"], p["a0"], p["a_up"], p["g_up"],
      p["k_k"], p["k_a"], p["r_k"], hs)


def _block_rows_sum(x):
    parts = [x[s:s + INV_BASE] for s in range(0, x.shape[0], INV_BASE)]
    while len(parts) > 1:
        parts = [parts[i] + parts[i + 1] for i in range(0, len(parts), 2)]
    return parts[0]


def _group_bcast(a, m, lane0):
    halves = []
    for h in range(a.shape[1] // LANES):
        z = a[:, LANES * h:LANES * (h + 1)]
        if m:
            z = pltpu.roll(z, LANES - m, axis=1)
        z = z * lane0
        for shift in (1, 2, 4):
            z = z + pltpu.roll(z, shift, axis=1)
        halves.append(z)
    return jnp.concatenate(halves, axis=1)


def _scan_unit(r, v, av, kd, bd, e_c, e_cx, e_nc, e_rt, e_tot, masks, st_ref, d, g):
    w_strict, w_incl, w8_strict, w8, w_off32, bm, bm_b, inv_masks, eye_c, lane0 = masks
    sl = slice(MXU_DIM * g, MXU_DIM * (g + 1))

    def expand(xb):
        return jnp.tile(xb, (HEADS_PER_GROUP, 1)) * bm_b

    r_t = (r[:, sl] * e_c[:, sl]).astype(BF16)
    a_t = (av[:, sl] * e_cx[:, sl]).astype(BF16)
    k_t = (kd[:, sl] * e_nc[:, sl]).astype(BF16)
    b_t = (bd[:, sl] * e_nc[:, sl]).astype(BF16)
    k_h = (kd[:, sl] * e_rt[:, sl]).astype(BF16)
    b_h = (bd[:, sl] * e_rt[:, sl]).astype(BF16)
    v_b = v[:, sl].astype(BF16)
    a_r = jnp.concatenate([a_t, r_t], axis=0)
    eb, ek, ev = expand(b_t), expand(k_t), expand(v_b)
    yield

    g_b = _dot_nt(a_r, eb)
    yield
    g_k = _dot_nt(a_r, ek)
    yield
    a_ab_f = g_b[0:CHUNK]
    a_ab = a_ab_f.astype(BF16) * w_strict
    a_rb = g_b[CHUNK:].astype(BF16) * w_incl
    a_ak = g_k[0:CHUNK].astype(BF16) * w_strict
    a_rk = g_k[CHUNK:].astype(BF16) * w_incl
    g_ab = expand(a_ab)
    d_c = _block_rows_sum(a_ab_f * w8_strict)
    d_cols = [(m, _group_bcast(d_c, m, lane0)) for m in range(d, INV_BASE - 1 + d)]
    from_v = _dot(jnp.concatenate([a_ak, a_rk], axis=0), ev)
    yield

    def horner(x):
        out = eye_c
        for m, col in d_cols:
            out = out + col * jnp.broadcast_to(x[m:m + 1, :], x.shape)
        return out

    t8_c = eye_c + d_c
    for _ in range((INV_BASE - 2) // 2):
        t8_c = horner(t8_c)
    s0 = st_ref[d, g]
    from_state = _dot_nt(a_r, expand(s0.astype(BF16)))
    yield
    for _ in range(INV_BASE - 2 - (INV_BASE - 2) // 2):
        t8_c = horner(t8_c)
    rhs = from_state[0:CHUNK] + from_v[0:CHUNK]
    y_known = from_state[CHUNK:] + from_v[CHUNK:]

    t32 = (jnp.tile(t8_c, (CHUNK // INV_BASE, 1)) * w8).astype(BF16)
    for lvl in range(N_INV_MASKS):
        t_off = _dot(t32, g_ab * inv_masks[lvl]).astype(BF16)
        yield
        t32 = t32 + _dot(t_off, expand(t32)).astype(BF16)
        yield
    u32 = _dot(t32, expand(rhs.astype(BF16)))
    yield
    o_u = _dot(a_ab * w_off32, expand(u32.astype(BF16)))
    yield
    u_c = (u32 + _dot(t32, expand(o_u.astype(BF16)))).astype(BF16)
    yield
    y = y_known + _dot(a_rb, expand(u_c))
    yield
    upd = _dot_tn(jnp.concatenate([u_c, v_b], axis=0), jnp.concatenate([b_h, k_h], axis=0)) * bm
    upd_w = upd[0:CHUNK]
    for h in range(1, HEADS_PER_GROUP):
        upd_w = upd_w + upd[CHUNK * h:CHUNK * (h + 1)]
    st_ref[d, g] = s0 * e_tot[:, sl] + upd_w
    return y


def _run_interleaved(gens):
    results = [None] * len(gens)
    active = list(range(len(gens)))
    while active:
        for i in list(active):
            try:
                next(gens[i])
            except StopIteration as stop:
                results[i] = stop.value
                active.remove(i)
    return results


N_SCAN_CONSTS = 10
N_SCAN_STREAMS = 3


def _scan_kernel(*refs, n_b):
    n_in = n_b * 2 * N_SCAN_STREAMS
    streams, consts = refs[:n_in], refs[n_in:n_in + N_SCAN_CONSTS]
    wgu_ref, wdn_ref = refs[n_in + N_SCAN_CONSTS:n_in + N_SCAN_CONSTS + 2]
    y_refs, (wgu_b_ref, wdn_b_ref), st_ref = refs[-5:-3], refs[-3:-1], refs[-1]
    tri_ref, ws_ref, wi_ref, w8s_ref, w8_ref, woff_ref, bm_ref, im_ref, eyec_ref, lane0_ref = consts

    @pl.when(pl.program_id(0) == 0)
    def _():
        st_ref[...] = jnp.zeros(st_ref.shape, F32)

    wgu_b_ref[...] = wgu_ref[...].astype(BF16)
    wdn_b_ref[...] = wdn_ref[...].astype(BF16)

    bm = bm_ref[...]
    bm_b = bm.astype(BF16)
    inv_masks = [im_ref[lvl] for lvl in range(N_INV_MASKS)]
    w8, eye_c, lane0, w_off32 = w8_ref[...], eyec_ref[...], lane0_ref[...], woff_ref[...]
    units = []
    for bi in range(n_b):
        for d in range(2):
            base = (bi * 2 + d) * N_SCAN_STREAMS
            rva_ref, kdb_ref, lw_ref = streams[base:base + N_SCAN_STREAMS]
            lw = lw_ref[...]
            c = _dot_exact_lhs(tri_ref[d], lw)
            c_tot = jnp.sum(lw, axis=0, keepdims=True)
            e_c, e_cx, e_nc = jnp.exp(c), jnp.exp(c - lw), jnp.exp(-c)
            e_rt, e_tot = jnp.exp(c_tot - c), jnp.exp(c_tot)
            masks = (ws_ref[d], wi_ref[d], w8s_ref[d], w8, w_off32, bm, bm_b, inv_masks, eye_c, lane0)
            r, v, av = (rva_ref[:, D_RWKV * k:D_RWKV * (k + 1)] for k in range(3))
            kd, bd = (kdb_ref[:, D_RWKV * k:D_RWKV * (k + 1)] for k in range(2))
            units += [_scan_unit(r, v, av, kd, bd, e_c, e_cx, e_nc, e_rt, e_tot, masks, st_ref.at[bi], d, g)
                      for g in range(N_GROUPS)]
    ys = _run_interleaved(units)
    for bi in range(n_b):
        for d in range(2):
            first = (bi * 2 + d) * N_GROUPS
            y_refs[d][bi] = jnp.concatenate(ys[first:first + N_GROUPS], axis=1)


def _scan_constants():
    i = np.arange(MXU_DIM)
    blk = (i[:, None] // CHUNK) == (i[None, :] // CHUNK)
    j = np.arange(CHUNK)
    tri = np.stack([j[:, None] >= j[None, :], j[:, None] <= j[None, :]])
    t, s = j[:, None], i[None, :] % CHUNK
    wide_strict = np.stack([t > s, t < s])
    wide_incl = np.stack([t >= s, t <= s])
    wide8 = (t // INV_BASE) == (s // INV_BASE)
    wide8_strict = wide_strict & wide8[None]
    wide_off32 = (t // (CHUNK // 2)) != (s // (CHUNK // 2))
    inv_masks = []
    n = INV_BASE
    while 2 * n < CHUNK:
        inv_masks.append(((i[:, None] // (2 * n)) == (i[None, :] // (2 * n))) & ((i[:, None] // n) != (i[None, :] // n)))
        n *= 2
    assert len(inv_masks) == N_INV_MASKS
    eye_c = np.arange(INV_BASE)[:, None] == (i[None, :] % INV_BASE)
    lane0 = np.broadcast_to((np.arange(LANES) % INV_BASE) == 0, (INV_BASE, LANES))
    f32 = lambda a: jnp.asarray(a, F32)
    b16 = lambda a: jnp.asarray(a, BF16)
    return (b16(tri), b16(wide_strict), b16(wide_incl), f32(wide8_strict), f32(wide8), b16(wide_off32), f32(blk),
            b16(np.stack(inv_masks)), f32(eye_c), f32(lane0))


def _rows_per_step(total, n_steps):
    step = 2 * SUBLANES
    return min(r for r in range(step, total + 1, step) if total % r == 0 and r * n_steps >= total)


def _rwkv_scan(rva, kdb, lw, w_gu, w_down, layer, dims):
    rows = rva.shape[0]
    n_b, ncc, nct = dims["batch"], dims["c_len"] // CHUNK, dims["t_len"] // CHUNK
    consts = _scan_constants()

    assert len(consts) == N_SCAN_CONSTS

    def chunk(s, d):
        j_ctx = s if d == 0 else ncc - 1 - s
        j_lat = s - ncc if d == 0 else nct - 1 - (s - ncc)
        return j_ctx, j_lat

    def flat_chunk(bi, d):
        def index(s):
            j_ctx, j_lat = chunk(s, d)
            return jnp.where(s < ncc, bi * ncc + j_ctx, n_b * ncc + bi * nct + j_lat)
        return index

    in_specs, args = [], []
    for bi in range(n_b):
        for d in range(2):
            idx = flat_chunk(bi, d)
            per_dir = lambda n, idx=idx, d=d: pl.BlockSpec((None, CHUNK, n * D_RWKV), lambda s: (d, idx(s), 0))
            in_specs += [pl.BlockSpec((CHUNK, 3 * D_RWKV), lambda s, idx=idx: (idx(s), 0)), per_dir(2), per_dir(1)]
            args += [rva, kdb, lw]
    in_specs += [pl.BlockSpec(a.shape, lambda s, nd=a.ndim: (0,) * nd) for a in consts]

    w_out_specs, w_out_shapes = [], []
    for w in (w_gu, w_down):
        n_rows, n_cols = w.shape[1:]
        blk_rows = _rows_per_step(n_rows, ncc + nct)
        last = n_rows // blk_rows - 1
        in_specs.append(pl.BlockSpec((None, blk_rows, n_cols), lambda s, last=last: (layer, jnp.minimum(s, last), 0)))
        w_out_specs.append(pl.BlockSpec((blk_rows, n_cols), lambda s, last=last: (jnp.minimum(s, last), 0)))
        w_out_shapes.append(jax.ShapeDtypeStruct((n_rows, n_cols), BF16))

    def y_spec(d):
        def index(s):
            j_ctx, j_lat = chunk(s, d)
            return (0, jnp.where(s < ncc, j_ctx, ncc + j_lat), 0)
        return pl.BlockSpec((n_b, CHUNK, D_RWKV), index)

    y_shape = jax.ShapeDtypeStruct((n_b, rows // n_b, D_RWKV), F32)
    return pl.pallas_call(
        functools.partial(_scan_kernel, n_b=n_b),
        grid=(ncc + nct,),
        in_specs=in_specs,
        out_specs=[y_spec(0), y_spec(1)] + w_out_specs,
        out_shape=[y_shape, y_shape] + w_out_shapes,
        scratch_shapes=[pltpu.VMEM((n_b, 2, N_GROUPS, HEAD_DIM, MXU_DIM), F32)],
        compiler_params=_cparams(("arbitrary",)),
        name="rwkv_chunk_scan",
    )(*args, *consts, w_gu, w_down)


def _pool_kernel(z_ref, zp_ref, zn_ref, w_ref, sc_ref, o_ref, *, tm, dims):
    i = pl.program_id(0)
    pos0, seq_len, first, last = _seq_edges(i, tm, dims)
    prev = jnp.where(first, 0.0, zp_ref[...])
    nxt = jnp.where(last, 0.0, zn_ref[...])
    ext = jnp.concatenate([prev, z_ref[...], nxt], axis=0)
    n = tm + 2 * POOL_HALO
    t = pos0 + lax.broadcasted_iota(jnp.int32, (tm, 1), 0)
    outs = []
    for gi, w in enumerate(POOL_WINDOWS):
        e = ext[:, POOL_GROUP_DIM * gi:POOL_GROUP_DIM * (gi + 1)]
        acc, width = e, 1
        while width < w:
            acc = acc + pltpu.roll(acc, n - width, axis=0)
            width *= 2
        win = pltpu.roll(acc, w // 2, axis=0)[POOL_HALO:POOL_HALO + tm]
        lo = jnp.clip(t - w // 2, 0, seq_len)
        hi = jnp.clip(t - w // 2 + w, 0, seq_len)
        y = win / (hi - lo).astype(F32) - e[POOL_HALO:POOL_HALO + tm]
        outs.append(_dot(y.astype(BF16), w_ref[gi].astype(BF16)))
    o_ref[...] = (jnp.concatenate(outs, axis=1) * sc_ref[...]).astype(BF16)


def _pool(z_p, pool_w, pool_scale, dims):
    rows = z_p.shape[0]
    tm = dims["tm"]
    nblk8 = rows // SUBLANES
    per = tm // SUBLANES
    return pl.pallas_call(
        functools.partial(_pool_kernel, tm=tm, dims=dims),
        grid=(rows // tm,),
        in_specs=[
            pl.BlockSpec((tm, D_POOL), lambda i: (i, 0)),
            pl.BlockSpec((SUBLANES, D_POOL), lambda i: (jnp.maximum(i * per - 1, 0), 0)),
            pl.BlockSpec((SUBLANES, D_POOL), lambda i: (jnp.minimum((i + 1) * per, nblk8 - 1), 0)),
            pl.BlockSpec(pool_w.shape, lambda i: (0, 0, 0)),
            pl.BlockSpec((1, D_POOL), lambda i: (0, 0)),
        ],
        out_specs=pl.BlockSpec((tm, D_POOL), lambda i: (i, 0)),
        out_shape=jax.ShapeDtypeStruct((rows, D_POOL), BF16),
        compiler_params=_cparams(("arbitrary",)),
        name="multiscale_pool",
    )(z_p, z_p, z_p, pool_w, pool_scale.reshape(1, D_POOL))


def _attn_kernel(sink_ref, q_ref, *rest, windowed, nblk):
    if windowed:
        kp_ref, kc_ref, kn_ref, kx_ref, o_ref = rest
        kv = jnp.concatenate([kp_ref[...], kc_ref[...], kn_ref[...], kx_ref[...]], axis=0)
    else:
        kx_ref, o_ref = rest
        kv = kx_ref[...]
    n = pl.program_id(1)
    q = q_ref[...]
    k_all, v_all = kv[:, 0:ATT_KV_DIM], kv[:, ATT_KV_DIM:2 * ATT_KV_DIM]
    nk = kv.shape[0]
    rows = ATT_GROUP * ATT_BLOCK
    if windowed:
        qi = lax.broadcasted_iota(jnp.int32, (ATT_BLOCK, nk), 0)
        kj = lax.broadcasted_iota(jnp.int32, (ATT_BLOCK, nk), 1)
        rel = kj - ATT_BLOCK - qi
        k_lo = jnp.where(n == 0, ATT_BLOCK, 0)
        k_hi = jnp.where(n == nblk - 1, 2 * ATT_BLOCK, 3 * ATT_BLOCK)
        bad = jnp.where(rel > ATT_BLOCK, 1, 0) + jnp.where(rel < -ATT_BLOCK, 1, 0) \
            + jnp.where(kj < k_lo, 1, 0) + jnp.where(kj >= k_hi, 1, 0)
        bad = jnp.where(kj >= 3 * ATT_BLOCK, 0, bad)
        neg = jnp.tile(jnp.where(bad > 0, -1e30, 0.0).astype(F32), (ATT_GROUP, 1))
    lane = lax.shift_right_logical(lax.broadcasted_iota(jnp.int32, (1, ATT_KV_DIM), 1), HEAD_DIM.bit_length() - 1)
    grow = lax.shift_right_logical(lax.broadcasted_iota(jnp.int32, (rows, 1), 0), ATT_BLOCK.bit_length() - 1)
    def kv_head(hk):
        lm_f = (lane == hk).astype(F32)
        lm_b = lm_f.astype(BF16)
        lhs = jnp.concatenate([q[:, ATT_KV_DIM * g:ATT_KV_DIM * (g + 1)] * lm_b for g in range(ATT_GROUP)], axis=0)
        s = _dot_nt(lhs, k_all)
        yield
        if windowed:
            s = s + neg
        sk = jnp.where(grow == 0, sink_ref[ATT_GROUP * hk],
                       jnp.where(grow == 1, sink_ref[ATT_GROUP * hk + 1], sink_ref[ATT_GROUP * hk + 2]))
        m = jnp.maximum(jnp.max(s, axis=1, keepdims=True), sk)
        p = jnp.exp(s - m)
        den = jnp.sum(p, axis=1, keepdims=True) + jnp.exp(sk - m)
        yield
        pv = _dot(p.astype(BF16), v_all) / den
        return [pv[ATT_BLOCK * g:ATT_BLOCK * (g + 1)] * lm_f for g in range(ATT_GROUP)]

    heads = _run_interleaved([kv_head(hk) for hk in range(ATT_KV_HEADS)])
    outs = [sum((head[g] for head in heads[1:]), heads[0][g]) for g in range(ATT_GROUP)]
    o_ref[...] = jnp.concatenate(outs, axis=1).astype(BF16)


def _attention(q, kv, sink, dims, windowed):
    rows = q.shape[0]
    n_b, c_len, t_len, nc = dims["batch"], dims["c_len"], dims["t_len"], dims["nc"]
    seq = t_len if windowed else c_len
    nblk = seq // ATT_BLOCK
    base = (nc // ATT_BLOCK) if windowed else 0
    qrow = lambda bi, n: (base + bi * nblk + n, 0)
    ctx_spec = pl.BlockSpec((c_len, 2 * ATT_KV_DIM), lambda bi, n: (bi, 0))
    blk = lambda f: pl.BlockSpec((ATT_BLOCK, 2 * ATT_KV_DIM), f)
    in_specs = [pl.BlockSpec(memory_space=pltpu.SMEM), pl.BlockSpec((ATT_BLOCK, D_ATT), qrow)]
    args = [sink, q]
    if windowed:
        in_specs += [
            blk(lambda bi, n: (base + bi * nblk + jnp.maximum(n - 1, 0), 0)),
            blk(qrow),
            blk(lambda bi, n: (base + bi * nblk + jnp.minimum(n + 1, nblk - 1), 0)),
        ]
        args += [kv, kv, kv]
    in_specs.append(ctx_spec)
    args.append(kv)
    out_rows = n_b * seq
    return pl.pallas_call(
        functools.partial(_attn_kernel, windowed=windowed, nblk=nblk),
        grid=(n_b, nblk),
        in_specs=in_specs,
        out_specs=pl.BlockSpec((ATT_BLOCK, D_ATT), lambda bi, n: (bi * nblk + n, 0)),
        out_shape=jax.ShapeDtypeStruct((out_rows, D_ATT), BF16),
        compiler_params=_cparams(("arbitrary", "arbitrary")),
        name="window_attention" if windowed else "context_attention",
    )(*args)


def _outproj_kernel(yf_ref, yb_ref, bonus_ref, g_ref, yp_ref, yac_ref, yal_ref, xc_ref, xl_ref, w_ref, wa_ref,
                    lng_ref, lnb_ref, hs_ref, gt_ref, ng_ref, sc_ref, sh_ref, xo_ref, ho_ref, *, ctx_tiles, first_tile):
    hs = hs_ref[...]
    y = yf_ref[...] + yb_ref[...]
    inv_n = 1.0 / HEAD_DIM
    dev = y - _head_sum(y, hs) * inv_n
    var = _head_sum(dev * dev, hs) * inv_n
    yn = dev * lax.rsqrt(var + RWKV_GN_EPS) * lng_ref[...] + lnb_ref[...]
    y_r = ((yn + bonus_ref[...]) * g_ref[...]).astype(BF16)
    acc = (_dot(y_r, w_ref[0:D_RWKV])
           + _dot(yp_ref[...], w_ref[D_RWKV:D_RWKV + D_POOL])
           + _dot(_pick_source(yac_ref, yal_ref, ctx_tiles, first_tile), wa_ref[...]))
    x_new = _pick_source(xc_ref, xl_ref, ctx_tiles, first_tile) + gt_ref[...] * acc
    xo_ref[...] = x_new
    ho_ref[...] = _rms_mod(x_new, ng_ref[...], sc_ref[...], sh_ref[...]).astype(BF16)


def _out_projection(yf, yb, bonus, g, y_p, ya_src, x_src, w_out_b, wa_b, layer, ln_g, ln_b, hs, mod3, mod_base, norm_g,
                    dims, first_tile):
    tm = dims["tm"]
    rows, d = dims["rows"], x_src[0].shape[1]
    ctx_tiles = dims["nc"] // tm
    ya_specs, ya_args = _two_source_specs(*ya_src, tm, ctx_tiles, first_tile)
    x_specs, x_args = _two_source_specs(*x_src, tm, ctx_tiles, first_tile)
    grp = dims["grp"]
    n_tiles = rows // tm - first_tile
    off = lambda w: pl.BlockSpec((tm, w), lambda i: (i + first_tile, 0))

    def mod_spec(k):
        return pl.BlockSpec((None, 1, d), lambda i: (mod_base + grp(i + first_tile, tm) * 6 + k, 0, 0))

    def y_index(ft):
        per_ctx, per_lat = dims["c_len"] // tm, dims["t_len"] // tm
        lt = jnp.maximum(ft - ctx_tiles, 0)
        in_ctx = ft < ctx_tiles
        return (jnp.where(in_ctx, ft // per_ctx, lt // per_lat),
                jnp.where(in_ctx, ft % per_ctx, per_ctx + lt % per_lat), 0)

    y_spec = pl.BlockSpec((None, tm, D_RWKV), lambda i: y_index(i + first_tile))
    vec = lambda w: pl.BlockSpec((1, w), lambda i: (0, 0))
    out_rows = n_tiles * tm
    return pl.pallas_call(
        functools.partial(_outproj_kernel, ctx_tiles=ctx_tiles, first_tile=first_tile),
        grid=(n_tiles,),
        in_specs=[y_spec, y_spec, off(D_RWKV), off(D_RWKV), off(D_POOL)] + ya_specs + x_specs + [
            pl.BlockSpec((None, d, d), lambda i: (layer, 0, 0), pipeline_mode=pl.Buffered(1)),
            pl.BlockSpec((None, D_ATT, d), lambda i: (layer, 0, 0), pipeline_mode=pl.Buffered(1)),
            vec(D_RWKV), vec(D_RWKV),
            pl.BlockSpec((MXU_DIM, MXU_DIM), lambda i: (0, 0)),
            mod_spec(2), vec(d), mod_spec(4), mod_spec(3),
        ],
        out_specs=[pl.BlockSpec((tm, d), lambda i: (i, 0)), pl.BlockSpec((tm, d), lambda i: (i, 0))],
        out_shape=[jax.ShapeDtypeStruct((out_rows, d), F32), jax.ShapeDtypeStruct((out_rows, d), BF16)],
        compiler_params=_cparams(("arbitrary",)),
        name="readout_outproj_norm",
    )(yf, yb, bonus, g, y_p, *ya_args, *x_args, w_out_b, wa_b, ln_g.reshape(1, -1), ln_b.reshape(1, -1), hs,
      mod3, norm_g.reshape(1, d), mod3, mod3)


def _ffn_kernel(h_ref, x_ref, wg_ref, wu_ref, wd_ref, gt_ref, fg_ref, o_ref, acc_ref, *, final):
    j = pl.program_id(1)

    @pl.when(j == 0)
    def _():
        acc_ref[...] = jnp.zeros(acc_ref.shape, F32)

    h = h_ref[...]
    n_sub = wg_ref.shape[1] // MXU_DIM
    slabs = [slice(MXU_DIM * q, MXU_DIM * (q + 1)) for q in range(n_sub)]
    gate_up = [(_dot(h, wg_ref[:, sl]), _dot(h, wu_ref[:, sl])) for sl in slabs]
    part = None
    for sl, (gate, up) in zip(slabs, gate_up):
        act = (gate * jax.nn.sigmoid(gate) * up).astype(BF16)
        p = _dot(act, wd_ref[sl, :])
        part = p if part is None else part + p
    acc_ref[...] += part

    @pl.when(j == pl.num_programs(1) - 1)
    def _():
        x_new = x_ref[...] + gt_ref[...] * acc_ref[...]
        if final:
            ms = jnp.mean(x_new * x_new, axis=-1, keepdims=True)
            x_new = x_new * lax.rsqrt(ms + NORM_EPS) * fg_ref[...]
        o_ref[...] = x_new


def _ffn(h2, x_new, w_gu_b, w_down_b, layer, mod3, mod_base, final_g, dims, first_tile, final):
    rows, d = x_new.shape
    tmf, th = dims["tmf"], dims["th"]
    hidden = w_down_b.shape[1]
    nh = hidden // th
    grp = dims["grp"]
    return pl.pallas_call(
        functools.partial(_ffn_kernel, final=final),
        grid=(rows // tmf, nh),
        in_specs=[
            pl.BlockSpec((tmf, d), lambda i, j: (i, 0)),
            pl.BlockSpec((tmf, d), lambda i, j: (i, 0)),
            pl.BlockSpec((None, d, th), lambda i, j: (layer, 0, j)),
            pl.BlockSpec((None, d, th), lambda i, j: (layer, 0, nh + j)),
            pl.BlockSpec((None, th, d), lambda i, j: (layer, j, 0)),
            pl.BlockSpec((None, 1, d), lambda i, j: (mod_base + grp(i + first_tile, tmf) * 6 + 5, 0, 0)),
            pl.BlockSpec((1, d), lambda i, j: (0, 0)),
        ],
        out_specs=pl.BlockSpec((tmf, d), lambda i, j: (i, 0)),
        out_shape=jax.ShapeDtypeStruct((rows, d), F32),
        scratch_shapes=[pltpu.VMEM((tmf, d), F32)],
        compiler_params=_cparams(("arbitrary", "arbitrary")),
        name="swiglu_ffn_final" if final else "swiglu_ffn",
    )(h2, x_new, w_gu_b, w_gu_b, w_down_b, mod3, final_g.reshape(1, d))


def _rope_tables(dims):
    t_len, nc, n_b = dims["t_len"], dims["nc"], dims["batch"]
    grid_rows = t_len // GRID_W
    n_freq = HEAD_DIM // 4
    inv = ROPE_BASE ** (-jnp.arange(n_freq, dtype=F32) / n_freq)
    ang_r = jnp.arange(grid_rows).astype(F32)[:, None] * inv
    ang_c = jnp.arange(GRID_W).astype(F32)[:, None] * inv

    def per_token(fn):
        by_row = jnp.repeat(fn(ang_r), GRID_W, axis=0)
        by_col = jnp.tile(fn(ang_c), (grid_rows, 1))
        return jnp.tile(jnp.concatenate([by_row, by_row, by_col, by_col], axis=-1), (1, LANES // HEAD_DIM))

    cos, sin = per_token(jnp.cos), per_token(jnp.sin)
    first_half = (jnp.arange(LANES) % 32) < 16
    sin_a = jnp.where(first_half, -sin, 0.0)
    sin_b = jnp.where(first_half, 0.0, sin)
    flat = lambda tab, fill: jnp.concatenate([jnp.full((nc, LANES), fill, F32), jnp.tile(tab, (n_b, 1))], axis=0)
    return flat(cos, 1.0), flat(sin_a, 0.0), flat(sin_b, 0.0)


def _block_diag2(w):
    z = jnp.zeros_like(w[0])
    return jnp.concatenate([jnp.concatenate([w[0], z], axis=1), jnp.concatenate([z, w[1]], axis=1)], axis=0)


def kernel(x, c, ctx, c_ctx, ada_w, ada_b, norm_mix_g, norm_ffn_g, w_in, rwkv_mu, rwkv_w0, rwkv_w_up, rwkv_a0,
           rwkv_a_up, rwkv_g_up, rwkv_k_k, rwkv_k_a, rwkv_r_k, rwkv_ln_g, rwkv_ln_b, pool_w, pool_scale,
           attn_sink, w_out, ffn_w_gu, ffn_w_down, final_norm_g):
    n_b, t_len, d = x.shape
    c_len = ctx.shape[1]
    depth = w_in.shape[0]
    nc = n_b * c_len
    rows = nc + n_b * t_len
    tm = 256 if c_len % 256 == 0 else c_len
    tmf = nc
    assert n_b + 1 <= SUBLANES and c_len % tm == 0 and t_len % tm == 0 and t_len % tmf == 0
    assert tm % ATT_BLOCK == 0 and c_len % CHUNK == 0 and t_len % GRID_W == 0 and d == w_out.shape[1]

    def grp(i, tile):
        r0 = i * tile
        return jnp.where(r0 < nc, 0, 1 + jnp.maximum(r0 - nc, 0) // t_len)

    hidden = ffn_w_down.shape[1]
    th = 512 if hidden % 512 == 0 else hidden
    dims = dict(batch=n_b, c_len=c_len, t_len=t_len, nc=nc, rows=rows, tm=tm, tmf=tmf, th=th, grp=grp)

    cvec = jnp.concatenate([c_ctx[None, :], c, jnp.zeros((SUBLANES - 1 - n_b, d), F32)], axis=0)
    mod = _modulation(cvec, ada_w, ada_b)
    mod3 = mod.reshape(depth * SUBLANES * 6, 1, d)

    x_src = (ctx.reshape(nc, d), x.reshape(n_b * t_len, d), 0)
    rope = _rope_tables(dims)
    hs = jnp.asarray(np.kron(np.eye(HEADS_PER_GROUP), np.ones((HEAD_DIM, HEAD_DIM))), BF16)

    w_in_b = w_in.astype(BF16)
    wq_b = w_in_b[:, :, OFF_AQ:OFF_AK].reshape(depth, d, ATT_KV_HEADS, ATT_GROUP, HEAD_DIM)
    wq_b = wq_b.transpose(0, 1, 3, 2, 4).reshape(depth, d, D_ATT)
    w_out_b = w_out.astype(BF16)
    wa_b = w_out_b[:, D_RWKV + D_POOL:].reshape(depth, ATT_KV_HEADS, ATT_GROUP, HEAD_DIM, d)
    wa_b = wa_b.transpose(0, 2, 1, 3, 4).reshape(depth, D_ATT, d)

    out = None
    for l in range(depth):
        last = l == depth - 1
        mod_base = l * SUBLANES * 6
        rw = dict(
            mu=rwkv_mu[l].reshape(1, RWKV_COLS),
            w0=rwkv_w0[l].reshape(1, 2 * D_RWKV),
            w_up=_block_diag2(rwkv_w_up[l]).astype(BF16),
            a0=rwkv_a0[l].reshape(1, 2 * D_RWKV),
            a_up=_block_diag2(rwkv_a_up[l]).astype(BF16),
            g_up=rwkv_g_up[l].astype(BF16),
            k_k=rwkv_k_k[l].reshape(1, D_RWKV),
            k_a=rwkv_k_a[l].reshape(1, D_RWKV),
            r_k=rwkv_r_k[l].reshape(1, D_RWKV),
        )

        z_r, z_p, q, kv = _in_projection(x_src, mod3, mod_base, norm_mix_g[l], w_in_b, wq_b, l, rope, dims)
        rva, g, bonus, kdb, lw = _rwkv_prepare(z_r, rw, hs, dims)
        yf, yb, w_gu_b, w_down_b = _rwkv_scan(rva, kdb, lw, ffn_w_gu, ffn_w_down, l, dims)
        y_p = _pool(z_p, pool_w[l], pool_scale[l], dims)
        ya_lat = _attention(q, kv, attn_sink[l], dims, windowed=True)
        ya_ctx = ya_lat if last else _attention(q, kv, attn_sink[l], dims, windowed=False)
        first_tile = nc // tm if last else 0
        x_new, h2 = _out_projection(yf, yb, bonus, g, y_p, (ya_ctx, ya_lat, 0), x_src, w_out_b, wa_b, l,
                                    rwkv_ln_g[l], rwkv_ln_b[l], hs, mod3, mod_base, norm_ffn_g[l], dims, first_tile)
        x_next = _ffn(h2, x_new, w_gu_b[None], w_down_b[None], 0, mod3, mod_base, final_norm_g, dims,
                      first_tile * tm // tmf, final=last)
        if last:
            out = x_next.reshape(n_b, t_len, d)
        else:
            x_src = (x_next, x_next, nc // tm)
    return out
```

```python
import functools

import numpy as np
import jax
import jax.numpy as jnp
from jax import lax
from jax.experimental import pallas as pl
from jax.experimental.pallas import tpu as pltpu

F32 = jnp.float32
BF16 = jnp.bfloat16

HEAD_DIM = 64
RWKV_HEADS = 12
D_RWKV = RWKV_HEADS * HEAD_DIM
LORA_RANK = 64
GATE_RANK = 128
D_POOL = 512
POOL_WINDOWS = (2, 4, 8, 16)
POOL_GROUP_DIM = 128
POOL_HALO = 8
ATT_Q_HEADS = 12
ATT_KV_HEADS = 4
ATT_GROUP = ATT_Q_HEADS // ATT_KV_HEADS
D_ATT = ATT_Q_HEADS * HEAD_DIM
ATT_KV_DIM = ATT_KV_HEADS * HEAD_DIM
ATT_BLOCK = 128
GRID_W = 64
ROPE_BASE = 10000.0
NORM_EPS = 1e-6
RWKV_GN_EPS = 64e-5

OFF_K = D_RWKV
OFF_V = 2 * D_RWKV
OFF_WD = 3 * D_RWKV
OFF_AD = OFF_WD + 2 * LORA_RANK
OFF_GD = OFF_AD + 2 * LORA_RANK
RWKV_COLS = OFF_GD + GATE_RANK
OFF_POOL = RWKV_COLS
OFF_AQ = OFF_POOL + D_POOL
OFF_AK = OFF_AQ + D_ATT
OFF_AV = OFF_AK + ATT_KV_DIM
D_IN = OFF_AV + ATT_KV_DIM

LANES = 128
SUBLANES = 8
MXU_DIM = 256
HEADS_PER_GROUP = MXU_DIM // HEAD_DIM
N_GROUPS = D_RWKV // MXU_DIM
CHUNK = MXU_DIM // HEADS_PER_GROUP
INV_BASE = 8
N_INV_MASKS = (CHUNK // INV_BASE).bit_length() - 2
VMEM_LIMIT = 56 * 1024 * 1024


def _cparams(sem, vmem=VMEM_LIMIT):
    return pltpu.CompilerParams(dimension_semantics=sem, vmem_limit_bytes=vmem)


def _dot(a, b):
    return jnp.dot(a, b, preferred_element_type=F32)


def _dot_nt(a, b):
    return lax.dot_general(a, b, (((1,), (1,)), ((), ())), preferred_element_type=F32)


def _dot_tn(a, b):
    return lax.dot_general(a, b, (((0,), (0,)), ((), ())), preferred_element_type=F32)


def _split3(x):
    x1 = x.astype(BF16)
    r1 = x - x1.astype(F32)
    x2 = r1.astype(BF16)
    x3 = (r1 - x2.astype(F32)).astype(BF16)
    return x1, x2, x3


def _dot_exact_rhs(x, m_bf16):
    x1, x2, x3 = _split3(x)
    return _dot(x1, m_bf16) + _dot(x2, m_bf16) + _dot(x3, m_bf16)


def _dot_exact_lhs(m_bf16, x):
    x1, x2, x3 = _split3(x)
    return _dot(m_bf16, x1) + _dot(m_bf16, x2) + _dot(m_bf16, x3)


def _head_sum(x, hs):
    parts = [_dot_exact_rhs(x[:, MXU_DIM * g:MXU_DIM * (g + 1)], hs) for g in range(N_GROUPS)]
    return jnp.concatenate(parts, axis=1)


def _rms_mod(x, g, sc, sh):
    ms = jnp.mean(x * x, axis=-1, keepdims=True)
    return x * lax.rsqrt(ms + NORM_EPS) * g * (1.0 + sc) + sh


def _mod_kernel(c_ref, w_ref, b_ref, o_ref):
    c = c_ref[...]
    s = (c * jax.nn.sigmoid(c)).astype(BF16)
    o_ref[...] = _dot(s, w_ref[...].astype(BF16)) + b_ref[...]


def _modulation(cvec, ada_w, ada_b):
    n_layers, d, n6 = ada_w.shape
    tn = 1024
    return pl.pallas_call(
        _mod_kernel,
        grid=(n_layers, n6 // tn),
        in_specs=[
            pl.BlockSpec((SUBLANES, d), lambda l, j: (0, 0)),
            pl.BlockSpec((None, d, tn), lambda l, j: (l, 0, j)),
            pl.BlockSpec((None, 1, tn), lambda l, j: (l, 0, j)),
        ],
        out_specs=pl.BlockSpec((None, SUBLANES, tn), lambda l, j: (l, 0, j)),
        out_shape=jax.ShapeDtypeStruct((n_layers, SUBLANES, n6), F32),
        compiler_params=_cparams(("arbitrary", "arbitrary")),
        name="adaln_modulation",
    )(cvec, ada_w, ada_b.reshape(n_layers, 1, n6))


def _rope(z, cos, sin_a, sin_b):
    outs = []
    for j in range(z.shape[1] // LANES):
        zb = z[:, LANES * j:LANES * (j + 1)]
        outs.append(zb * cos + pltpu.roll(zb, LANES - 16, axis=1) * sin_a + pltpu.roll(zb, 16, axis=1) * sin_b)
    return jnp.concatenate(outs, axis=1)


def _two_source_specs(src_ctx, src_lat, lat_first_tile, tm, ctx_tiles, first_tile):
    width = src_ctx.shape[1]
    spec_c = pl.BlockSpec((tm, width), lambda i: (jnp.minimum(i + first_tile, ctx_tiles - 1), 0))
    spec_l = pl.BlockSpec((tm, width), lambda i: (jnp.maximum(i + first_tile - ctx_tiles, 0) + lat_first_tile, 0))
    return [spec_c, spec_l], [src_ctx, src_lat]


def _pick_source(c_ref, l_ref, ctx_tiles, first_tile):
    return jnp.where(pl.program_id(0) + first_tile < ctx_tiles, c_ref[...], l_ref[...])


def _inproj_kernel(xc_ref, xl_ref, g_ref, sc_ref, sh_ref, w_ref, wq_ref, cos_ref, sa_ref, sb_ref,
                   zr_ref, zp_ref, q_ref, kv_ref, *, ctx_tiles):
    x = _pick_source(xc_ref, xl_ref, ctx_tiles, 0)
    h = _rms_mod(x, g_ref[...], sc_ref[...], sh_ref[...]).astype(BF16)
    zr_ref[...] = _dot(h, w_ref[:, 0:RWKV_COLS])
    zp_ref[...] = _dot(h, w_ref[:, OFF_POOL:OFF_AQ])
    cos, sa, sb = cos_ref[...], sa_ref[...], sb_ref[...]
    zq = _dot(h, wq_ref[...])
    q_ref[...] = (_rope(zq, cos, sa, sb) * (HEAD_DIM ** -0.5)).astype(BF16)
    zk = _dot(h, w_ref[:, OFF_AK:OFF_AV])
    zv = _dot(h, w_ref[:, OFF_AV:D_IN])
    kv_ref[...] = jnp.concatenate([_rope(zk, cos, sa, sb), zv], axis=1).astype(BF16)


def _in_projection(x_src, mod3, mod_base, norm_g, w_in_b, wq_b, layer, rope, dims):
    tm = dims["tm"]
    rows, d = dims["rows"], x_src[0].shape[1]
    ctx_tiles = dims["nc"] // tm
    x_specs, x_args = _two_source_specs(*x_src, tm, ctx_tiles, 0)
    grp = dims["grp"]
    cos, sin_a, sin_b = rope

    def mod_spec(k):
        return pl.BlockSpec((None, 1, d), lambda i: (mod_base + grp(i, tm) * 6 + k, 0, 0))

    row_spec = lambda w: pl.BlockSpec((tm, w), lambda i: (i, 0))
    return pl.pallas_call(
        functools.partial(_inproj_kernel, ctx_tiles=ctx_tiles),
        grid=(rows // tm,),
        in_specs=x_specs + [
            pl.BlockSpec((1, d), lambda i: (0, 0)),
            mod_spec(1), mod_spec(0),
            pl.BlockSpec((None, d, D_IN), lambda i: (layer, 0, 0), pipeline_mode=pl.Buffered(1)),
            pl.BlockSpec((None, d, D_ATT), lambda i: (layer, 0, 0), pipeline_mode=pl.Buffered(1)),
            row_spec(LANES), row_spec(LANES), row_spec(LANES),
        ],
        out_specs=[row_spec(RWKV_COLS), row_spec(D_POOL), row_spec(D_ATT), row_spec(2 * ATT_KV_DIM)],
        out_shape=[
            jax.ShapeDtypeStruct((rows, RWKV_COLS), F32),
            jax.ShapeDtypeStruct((rows, D_POOL), F32),
            jax.ShapeDtypeStruct((rows, D_ATT), BF16),
            jax.ShapeDtypeStruct((rows, 2 * ATT_KV_DIM), BF16),
        ],
        compiler_params=_cparams(("arbitrary",)),
        name="norm_inproj_rope",
    )(*x_args, norm_g.reshape(1, d), mod3, mod3, w_in_b, wq_b, cos, sin_a, sin_b)


def _seq_edges(i, tm, dims):
    nc, c_len, t_len = dims["nc"], dims["c_len"], dims["t_len"]
    r0 = i * tm
    is_ctx = r0 < nc
    seq_len = jnp.where(is_ctx, c_len, t_len)
    pos0 = jnp.where(is_ctx, lax.rem(r0, c_len), lax.rem(jnp.maximum(r0 - nc, 0), t_len))
    return pos0, seq_len, pos0 == 0, pos0 + tm == seq_len


def _prep_kernel(z_ref, zp_ref, zn_ref, mu_ref, w0_ref, wup_ref, a0_ref, aup_ref, gup_ref,
                 kk_ref, ka_ref, rk_ref, hs_ref,
                 rva_ref, g_ref, bonus_ref, kdb_ref, lw_ref, *, tm, dims):
    i = pl.program_id(0)
    _, _, first, last = _seq_edges(i, tm, dims)
    z = z_ref[...]
    prev_row = jnp.where(first, 0.0, zp_ref[SUBLANES - 1:SUBLANES, :])
    next_row = jnp.where(last, 0.0, zn_ref[0:1, :])
    row = lax.broadcasted_iota(jnp.int32, (tm, 1), 0)
    z_m1 = jnp.where(row == 0, prev_row, pltpu.roll(z, 1, axis=0))
    z_p1 = jnp.where(row == tm - 1, next_row, pltpu.roll(z, tm - 1, axis=0))
    zs = z + mu_ref[...] * (0.5 * (z_m1 + z_p1) - z)

    r = zs[:, 0:OFF_K]
    k = zs[:, OFF_K:OFF_V]
    v = zs[:, OFF_V:OFF_WD]
    wd = zs[:, OFF_WD:OFF_AD]
    ad = zs[:, OFF_AD:OFF_GD]
    gd = zs[:, OFF_GD:RWKV_COLS]

    w_pre = w0_ref[...] + _dot(jnp.tanh(wd).astype(BF16), wup_ref[...])
    lw = -float(np.exp(-0.5)) * jax.nn.sigmoid(w_pre)
    a = jax.nn.sigmoid(a0_ref[...] + _dot(ad.astype(BF16), aup_ref[...]))
    g_ref[...] = _dot(jax.nn.sigmoid(gd).astype(BF16), gup_ref[...])

    hs = hs_ref[...]
    kx = k * kk_ref[...]
    kn = kx / jnp.maximum(jnp.sqrt(_head_sum(kx * kx, hs)), 1e-12)
    rk = r * rk_ref[...]
    ka = ka_ref[...]
    dot_rk = None
    for d in range(2):
        a_d = a[:, D_RWKV * d:D_RWKV * (d + 1)]
        kd = k * (1.0 + (a_d - 1.0) * ka)
        kdb_ref[d, :, 0:D_RWKV] = kd.astype(BF16)
        kdb_ref[d, :, D_RWKV:2 * D_RWKV] = (kn * a_d).astype(BF16)
        lw_ref[d] = lw[:, D_RWKV * d:D_RWKV * (d + 1)]
        dot_rk = rk * kd if dot_rk is None else dot_rk + rk * kd
    rva_ref[:, 0:D_RWKV] = r.astype(BF16)
    rva_ref[:, D_RWKV:2 * D_RWKV] = v.astype(BF16)
    rva_ref[:, 2 * D_RWKV:3 * D_RWKV] = (-kn).astype(BF16)
    bonus_ref[...] = _head_sum(dot_rk, hs) * v


N_PREP_IN, N_POOL_IN, N_PREP_OUT = 13, 5, 5


def _prep_pool_kernel(*refs, tm, dims):
    prep_in, pool_in = refs[:N_PREP_IN], refs[N_PREP_IN:N_PREP_IN + N_POOL_IN]
    prep_out, pool_out = refs[N_PREP_IN + N_POOL_IN:-1], refs[-1]
    assert len(prep_out) == N_PREP_OUT
    _prep_kernel(*prep_in, *prep_out, tm=tm, dims=dims)
    _pool_kernel(*pool_in, pool_out, tm=tm, dims=dims)


def _rwkv_prepare_and_pool(z_r, p, hs, z_p, pool_w, pool_scale, dims):
    rows = z_r.shape[0]
    tm = dims["tm"]
    nblk8 = rows // SUBLANES
    per = tm // SUBLANES
    full = lambda shape: pl.BlockSpec(shape, lambda i: (0,) * len(shape))
    row_spec = lambda n: pl.BlockSpec((tm, n * D_RWKV), lambda i: (i, 0))
    dir_spec = lambda n: pl.BlockSpec((2, tm, n * D_RWKV), lambda i: (0, i, 0))
    one = lambda n, dt: jax.ShapeDtypeStruct((rows, n * D_RWKV), dt)
    two = lambda n, dt: jax.ShapeDtypeStruct((2, rows, n * D_RWKV), dt)
    def halo_specs(width):
        return [pl.BlockSpec((tm, width), lambda i: (i, 0)),
                pl.BlockSpec((SUBLANES, width), lambda i: (jnp.maximum(i * per - 1, 0), 0)),
                pl.BlockSpec((SUBLANES, width), lambda i: (jnp.minimum((i + 1) * per, nblk8 - 1), 0))]

    pool_specs = halo_specs(D_POOL) + [full(pool_w.shape), full((1, D_POOL))]
    return pl.pallas_call(
        functools.partial(_prep_pool_kernel, tm=tm, dims=dims),
        grid=(rows // tm,),
        in_specs=halo_specs(RWKV_COLS) + [
            full((1, RWKV_COLS)),
            full((1, 2 * D_RWKV)), full((2 * LORA_RANK, 2 * D_RWKV)),
            full((1, 2 * D_RWKV)), full((2 * LORA_RANK, 2 * D_RWKV)),
            full((GATE_RANK, D_RWKV)),
            full((1, D_RWKV)), full((1, D_RWKV)), full((1, D_RWKV)),
            full((MXU_DIM, MXU_DIM)),
        ] + pool_specs,
        out_specs=[row_spec(3), row_spec(1), row_spec(1), dir_spec(2), dir_spec(1),
                   pl.BlockSpec((tm, D_POOL), lambda i: (i, 0))],
        out_shape=[one(3, BF16), one(1, F32), one(1, F32), two(2, BF16), two(1, F32),
                   jax.ShapeDtypeStruct((rows, D_POOL), BF16)],
        compiler_params=_cparams(("arbitrary",)),
        name="rwkv_prepare_pool",
    )(z_r, z_r, z_r, p["mu"], p["w0"], p["w_up"], p["a0"], p["a_up"], p["g_up"],
      p["k_k"], p["k_a"], p["r_k"], hs, z_p, z_p, z_p, pool_w, pool_scale.reshape(1, D_POOL))


def _block_rows_sum(x):
    parts = [x[s:s + INV_BASE] for s in range(0, x.shape[0], INV_BASE)]
    while len(parts) > 1:
        parts = [parts[i] + parts[i + 1] for i in range(0, len(parts), 2)]
    return parts[0]


def _group_bcast(a, m, lane0):
    halves = []
    for h in range(a.shape[1] // LANES):
        z = a[:, LANES * h:LANES * (h + 1)]
        if m:
            z = pltpu.roll(z, LANES - m, axis=1)
        z = z * lane0
        for shift in (1, 2, 4):
            z = z + pltpu.roll(z, shift, axis=1)
        halves.append(z)
    return jnp.concatenate(halves, axis=1)


def _scan_unit(r, v, av, kd, bd, e_c, e_cx, e_nc, e_rt, e_tot, masks, st_ref, d, g):
    w_strict, w_incl, w8_strict, w8, w_off32, bm, bm_b, inv_masks, eye_c, lane0 = masks
    sl = slice(MXU_DIM * g, MXU_DIM * (g + 1))

    def expand(xb):
        return jnp.tile(xb, (HEADS_PER_GROUP, 1)) * bm_b

    r_t = (r[:, sl] * e_c[:, sl]).astype(BF16)
    a_t = (av[:, sl] * e_cx[:, sl]).astype(BF16)
    k_t = (kd[:, sl] * e_nc[:, sl]).astype(BF16)
    b_t = (bd[:, sl] * e_nc[:, sl]).astype(BF16)
    k_h = (kd[:, sl] * e_rt[:, sl]).astype(BF16)
    b_h = (bd[:, sl] * e_rt[:, sl]).astype(BF16)
    v_b = v[:, sl].astype(BF16)
    a_r = jnp.concatenate([a_t, r_t], axis=0)
    eb, ek, ev = expand(b_t), expand(k_t), expand(v_b)
    yield

    g_b = _dot_nt(a_r, eb)
    yield
    g_k = _dot_nt(a_r, ek)
    yield
    a_ab_f = g_b[0:CHUNK]
    a_ab = a_ab_f.astype(BF16) * w_strict
    a_rb = g_b[CHUNK:].astype(BF16) * w_incl
    a_ak = g_k[0:CHUNK].astype(BF16) * w_strict
    a_rk = g_k[CHUNK:].astype(BF16) * w_incl
    g_ab = expand(a_ab)
    d_c = _block_rows_sum(a_ab_f * w8_strict)
    d_cols = [(m, _group_bcast(d_c, m, lane0)) for m in range(d, INV_BASE - 1 + d)]
    from_v = _dot(jnp.concatenate([a_ak, a_rk], axis=0), ev)
    yield

    def horner(x):
        out = eye_c
        for m, col in d_cols:
            out = out + col * jnp.broadcast_to(x[m:m + 1, :], x.shape)
        return out

    t8_c = eye_c + d_c
    for _ in range((INV_BASE - 2) // 2):
        t8_c = horner(t8_c)
    s0 = st_ref[d, g]
    from_state = _dot_nt(a_r, expand(s0.astype(BF16)))
    yield
    for _ in range(INV_BASE - 2 - (INV_BASE - 2) // 2):
        t8_c = horner(t8_c)
    rhs = from_state[0:CHUNK] + from_v[0:CHUNK]
    y_known = from_state[CHUNK:] + from_v[CHUNK:]

    t32 = (jnp.tile(t8_c, (CHUNK // INV_BASE, 1)) * w8).astype(BF16)
    for lvl in range(N_INV_MASKS):
        t_off = _dot(t32, g_ab * inv_masks[lvl]).astype(BF16)
        yield
        t32 = t32 + _dot(t_off, expand(t32)).astype(BF16)
        yield
    u32 = _dot(t32, expand(rhs.astype(BF16)))
    yield
    o_u = _dot(a_ab * w_off32, expand(u32.astype(BF16)))
    yield
    u_c = (u32 + _dot(t32, expand(o_u.astype(BF16)))).astype(BF16)
    yield
    y = y_known + _dot(a_rb, expand(u_c))
    yield
    upd = _dot_tn(jnp.concatenate([u_c, v_b], axis=0), jnp.concatenate([b_h, k_h], axis=0)) * bm
    upd_w = upd[0:CHUNK]
    for h in range(1, HEADS_PER_GROUP):
        upd_w = upd_w + upd[CHUNK * h:CHUNK * (h + 1)]
    st_ref[d, g] = s0 * e_tot[:, sl] + upd_w
    return y


def _run_interleaved(gens):
    results = [None] * len(gens)
    active = list(range(len(gens)))
    while active:
        for i in list(active):
            try:
                next(gens[i])
            except StopIteration as stop:
                results[i] = stop.value
                active.remove(i)
    return results


N_SCAN_CONSTS = 10
N_SCAN_STREAMS = 3


def _scan_kernel(*refs, n_b):
    n_in = n_b * 2 * N_SCAN_STREAMS
    streams, consts = refs[:n_in], refs[n_in:n_in + N_SCAN_CONSTS]
    wgu_ref, wdn_ref = refs[n_in + N_SCAN_CONSTS:n_in + N_SCAN_CONSTS + 2]
    y_refs, (wgu_b_ref, wdn_b_ref), st_ref = refs[-5:-3], refs[-3:-1], refs[-1]
    tri_ref, ws_ref, wi_ref, w8s_ref, w8_ref, woff_ref, bm_ref, im_ref, eyec_ref, lane0_ref = consts

    @pl.when(pl.program_id(0) == 0)
    def _():
        st_ref[...] = jnp.zeros(st_ref.shape, F32)

    wgu_b_ref[...] = wgu_ref[...].astype(BF16)
    wdn_b_ref[...] = wdn_ref[...].astype(BF16)

    bm = bm_ref[...]
    bm_b = bm.astype(BF16)
    inv_masks = [im_ref[lvl] for lvl in range(N_INV_MASKS)]
    w8, eye_c, lane0, w_off32 = w8_ref[...], eyec_ref[...], lane0_ref[...], woff_ref[...]
    units = []
    for bi in range(n_b):
        for d in range(2):
            base = (bi * 2 + d) * N_SCAN_STREAMS
            rva_ref, kdb_ref, lw_ref = streams[base:base + N_SCAN_STREAMS]
            lw = lw_ref[...]
            c = _dot_exact_lhs(tri_ref[d], lw)
            c_tot = jnp.sum(lw, axis=0, keepdims=True)
            e_c, e_cx, e_nc = jnp.exp(c), jnp.exp(c - lw), jnp.exp(-c)
            e_rt, e_tot = jnp.exp(c_tot - c), jnp.exp(c_tot)
            masks = (ws_ref[d], wi_ref[d], w8s_ref[d], w8, w_off32, bm, bm_b, inv_masks, eye_c, lane0)
            r, v, av = (rva_ref[:, D_RWKV * k:D_RWKV * (k + 1)] for k in range(3))
            kd, bd = (kdb_ref[:, D_RWKV * k:D_RWKV * (k + 1)] for k in range(2))
            units += [_scan_unit(r, v, av, kd, bd, e_c, e_cx, e_nc, e_rt, e_tot, masks, st_ref.at[bi], d, g)
                      for g in range(N_GROUPS)]
    ys = _run_interleaved(units)
    for bi in range(n_b):
        for d in range(2):
            first = (bi * 2 + d) * N_GROUPS
            y_refs[d][bi] = jnp.concatenate(ys[first:first + N_GROUPS], axis=1)


def _scan_constants():
    i = np.arange(MXU_DIM)
    blk = (i[:, None] // CHUNK) == (i[None, :] // CHUNK)
    j = np.arange(CHUNK)
    tri = np.stack([j[:, None] >= j[None, :], j[:, None] <= j[None, :]])
    t, s = j[:, None], i[None, :] % CHUNK
    wide_strict = np.stack([t > s, t < s])
    wide_incl = np.stack([t >= s, t <= s])
    wide8 = (t // INV_BASE) == (s // INV_BASE)
    wide8_strict = wide_strict & wide8[None]
    wide_off32 = (t // (CHUNK // 2)) != (s // (CHUNK // 2))
    inv_masks = []
    n = INV_BASE
    while 2 * n < CHUNK:
        inv_masks.append(((i[:, None] // (2 * n)) == (i[None, :] // (2 * n))) & ((i[:, None] // n) != (i[None, :] // n)))
        n *= 2
    assert len(inv_masks) == N_INV_MASKS
    eye_c = np.arange(INV_BASE)[:, None] == (i[None, :] % INV_BASE)
    lane0 = np.broadcast_to((np.arange(LANES) % INV_BASE) == 0, (INV_BASE, LANES))
    f32 = lambda a: jnp.asarray(a, F32)
    b16 = lambda a: jnp.asarray(a, BF16)
    return (b16(tri), b16(wide_strict), b16(wide_incl), f32(wide8_strict), f32(wide8), b16(wide_off32), f32(blk),
            b16(np.stack(inv_masks)), f32(eye_c), f32(lane0))


def _rows_per_step(total, n_steps):
    step = 2 * SUBLANES
    return min(r for r in range(step, total + 1, step) if total % r == 0 and r * n_steps >= total)


def _rwkv_scan(rva, kdb, lw, w_gu, w_down, layer, dims):
    rows = rva.shape[0]
    n_b, ncc, nct = dims["batch"], dims["c_len"] // CHUNK, dims["t_len"] // CHUNK
    consts = _scan_constants()

    assert len(consts) == N_SCAN_CONSTS

    def chunk(s, d):
        j_ctx = s if d == 0 else ncc - 1 - s
        j_lat = s - ncc if d == 0 else nct - 1 - (s - ncc)
        return j_ctx, j_lat

    def flat_chunk(bi, d):
        def index(s):
            j_ctx, j_lat = chunk(s, d)
            return jnp.where(s < ncc, bi * ncc + j_ctx, n_b * ncc + bi * nct + j_lat)
        return index

    in_specs, args = [], []
    for bi in range(n_b):
        for d in range(2):
            idx = flat_chunk(bi, d)
            per_dir = lambda n, idx=idx, d=d: pl.BlockSpec((None, CHUNK, n * D_RWKV), lambda s: (d, idx(s), 0))
            in_specs += [pl.BlockSpec((CHUNK, 3 * D_RWKV), lambda s, idx=idx: (idx(s), 0)), per_dir(2), per_dir(1)]
            args += [rva, kdb, lw]
    in_specs += [pl.BlockSpec(a.shape, lambda s, nd=a.ndim: (0,) * nd) for a in consts]

    w_out_specs, w_out_shapes = [], []
    for w in (w_gu, w_down):
        n_rows, n_cols = w.shape[1:]
        blk_rows = _rows_per_step(n_rows, ncc + nct)
        last = n_rows // blk_rows - 1
        in_specs.append(pl.BlockSpec((None, blk_rows, n_cols), lambda s, last=last: (layer, jnp.minimum(s, last), 0)))
        w_out_specs.append(pl.BlockSpec((blk_rows, n_cols), lambda s, last=last: (jnp.minimum(s, last), 0)))
        w_out_shapes.append(jax.ShapeDtypeStruct((n_rows, n_cols), BF16))

    def y_spec(d):
        def index(s):
            j_ctx, j_lat = chunk(s, d)
            return (0, jnp.where(s < ncc, j_ctx, ncc + j_lat), 0)
        return pl.BlockSpec((n_b, CHUNK, D_RWKV), index)

    y_shape = jax.ShapeDtypeStruct((n_b, rows // n_b, D_RWKV), F32)
    return pl.pallas_call(
        functools.partial(_scan_kernel, n_b=n_b),
        grid=(ncc + nct,),
        in_specs=in_specs,
        out_specs=[y_spec(0), y_spec(1)] + w_out_specs,
        out_shape=[y_shape, y_shape] + w_out_shapes,
        scratch_shapes=[pltpu.VMEM((n_b, 2, N_GROUPS, HEAD_DIM, MXU_DIM), F32)],
        compiler_params=_cparams(("arbitrary",)),
        name="rwkv_chunk_scan",
    )(*args, *consts, w_gu, w_down)


def _pool_kernel(z_ref, zp_ref, zn_ref, w_ref, sc_ref, o_ref, *, tm, dims):
    i = pl.program_id(0)
    pos0, seq_len, first, last = _seq_edges(i, tm, dims)
    prev = jnp.where(first, 0.0, zp_ref[...])
    nxt = jnp.where(last, 0.0, zn_ref[...])
    ext = jnp.concatenate([prev, z_ref[...], nxt], axis=0)
    n = tm + 2 * POOL_HALO
    t = pos0 + lax.broadcasted_iota(jnp.int32, (tm, 1), 0)
    outs = []
    for gi, w in enumerate(POOL_WINDOWS):
        e = ext[:, POOL_GROUP_DIM * gi:POOL_GROUP_DIM * (gi + 1)]
        acc, width = e, 1
        while width < w:
            acc = acc + pltpu.roll(acc, n - width, axis=0)
            width *= 2
        win = pltpu.roll(acc, w // 2, axis=0)[POOL_HALO:POOL_HALO + tm]
        lo = jnp.clip(t - w // 2, 0, seq_len)
        hi = jnp.clip(t - w // 2 + w, 0, seq_len)
        y = win / (hi - lo).astype(F32) - e[POOL_HALO:POOL_HALO + tm]
        outs.append(_dot(y.astype(BF16), w_ref[gi].astype(BF16)))
    o_ref[...] = (jnp.concatenate(outs, axis=1) * sc_ref[...]).astype(BF16)


def _attn_kernel(sink_ref, q_ref, *rest, windowed, nblk):
    if windowed:
        kp_ref, kc_ref, kn_ref, kx_ref, o_ref = rest
        kv = jnp.concatenate([kp_ref[...], kc_ref[...], kn_ref[...], kx_ref[...]], axis=0)
    else:
        kx_ref, o_ref = rest
        kv = kx_ref[...]
    n = pl.program_id(1)
    q = q_ref[...]
    k_all, v_all = kv[:, 0:ATT_KV_DIM], kv[:, ATT_KV_DIM:2 * ATT_KV_DIM]
    nk = kv.shape[0]
    rows = ATT_GROUP * ATT_BLOCK
    if windowed:
        qi = lax.broadcasted_iota(jnp.int32, (ATT_BLOCK, nk), 0)
        kj = lax.broadcasted_iota(jnp.int32, (ATT_BLOCK, nk), 1)
        rel = kj - ATT_BLOCK - qi
        k_lo = jnp.where(n == 0, ATT_BLOCK, 0)
        k_hi = jnp.where(n == nblk - 1, 2 * ATT_BLOCK, 3 * ATT_BLOCK)
        bad = jnp.where(rel > ATT_BLOCK, 1, 0) + jnp.where(rel < -ATT_BLOCK, 1, 0) \
            + jnp.where(kj < k_lo, 1, 0) + jnp.where(kj >= k_hi, 1, 0)
        bad = jnp.where(kj >= 3 * ATT_BLOCK, 0, bad)
        neg = jnp.tile(jnp.where(bad > 0, -1e30, 0.0).astype(F32), (ATT_GROUP, 1))
    lane = lax.shift_right_logical(lax.broadcasted_iota(jnp.int32, (1, ATT_KV_DIM), 1), HEAD_DIM.bit_length() - 1)
    grow = lax.shift_right_logical(lax.broadcasted_iota(jnp.int32, (rows, 1), 0), ATT_BLOCK.bit_length() - 1)
    def kv_head(hk):
        lm_f = (lane == hk).astype(F32)
        lm_b = lm_f.astype(BF16)
        lhs = jnp.concatenate([q[:, ATT_KV_DIM * g:ATT_KV_DIM * (g + 1)] * lm_b for g in range(ATT_GROUP)], axis=0)
        s = _dot_nt(lhs, k_all)
        yield
        if windowed:
            s = s + neg
        sk = jnp.where(grow == 0, sink_ref[ATT_GROUP * hk],
                       jnp.where(grow == 1, sink_ref[ATT_GROUP * hk + 1], sink_ref[ATT_GROUP * hk + 2]))
        m = jnp.maximum(jnp.max(s, axis=1, keepdims=True), sk)
        p = jnp.exp(s - m)
        den = jnp.sum(p, axis=1, keepdims=True) + jnp.exp(sk - m)
        yield
        pv = _dot(p.astype(BF16), v_all) / den
        return [pv[ATT_BLOCK * g:ATT_BLOCK * (g + 1)] * lm_f for g in range(ATT_GROUP)]

    heads = _run_interleaved([kv_head(hk) for hk in range(ATT_KV_HEADS)])
    outs = [sum((head[g] for head in heads[1:]), heads[0][g]) for g in range(ATT_GROUP)]
    o_ref[...] = jnp.concatenate(outs, axis=1).astype(BF16)


def _attention(q, kv, sink, dims, windowed):
    rows = q.shape[0]
    n_b, c_len, t_len, nc = dims["batch"], dims["c_len"], dims["t_len"], dims["nc"]
    seq = t_len if windowed else c_len
    nblk = seq // ATT_BLOCK
    base = (nc // ATT_BLOCK) if windowed else 0
    qrow = lambda bi, n: (base + bi * nblk + n, 0)
    ctx_spec = pl.BlockSpec((c_len, 2 * ATT_KV_DIM), lambda bi, n: (bi, 0))
    blk = lambda f: pl.BlockSpec((ATT_BLOCK, 2 * ATT_KV_DIM), f)
    in_specs = [pl.BlockSpec(memory_space=pltpu.SMEM), pl.BlockSpec((ATT_BLOCK, D_ATT), qrow)]
    args = [sink, q]
    if windowed:
        in_specs += [
            blk(lambda bi, n: (base + bi * nblk + jnp.maximum(n - 1, 0), 0)),
            blk(qrow),
            blk(lambda bi, n: (base + bi * nblk + jnp.minimum(n + 1, nblk - 1), 0)),
        ]
        args += [kv, kv, kv]
    in_specs.append(ctx_spec)
    args.append(kv)
    out_rows = n_b * seq
    return pl.pallas_call(
        functools.partial(_attn_kernel, windowed=windowed, nblk=nblk),
        grid=(n_b, nblk),
        in_specs=in_specs,
        out_specs=pl.BlockSpec((ATT_BLOCK, D_ATT), lambda bi, n: (bi * nblk + n, 0)),
        out_shape=jax.ShapeDtypeStruct((out_rows, D_ATT), BF16),
        compiler_params=_cparams(("arbitrary", "arbitrary")),
        name="window_attention" if windowed else "context_attention",
    )(*args)


def _outproj_kernel(yf_ref, yb_ref, bonus_ref, g_ref, yp_ref, yac_ref, yal_ref, xc_ref, xl_ref, w_ref, wa_ref,
                    lng_ref, lnb_ref, hs_ref, gt_ref, ng_ref, sc_ref, sh_ref, xo_ref, ho_ref, *, ctx_tiles, first_tile):
    hs = hs_ref[...]
    y = yf_ref[...] + yb_ref[...]
    inv_n = 1.0 / HEAD_DIM
    dev = y - _head_sum(y, hs) * inv_n
    var = _head_sum(dev * dev, hs) * inv_n
    yn = dev * lax.rsqrt(var + RWKV_GN_EPS) * lng_ref[...] + lnb_ref[...]
    y_r = ((yn + bonus_ref[...]) * g_ref[...]).astype(BF16)
    acc = (_dot(y_r, w_ref[0:D_RWKV])
           + _dot(yp_ref[...], w_ref[D_RWKV:D_RWKV + D_POOL])
           + _dot(_pick_source(yac_ref, yal_ref, ctx_tiles, first_tile), wa_ref[...]))
    x_new = _pick_source(xc_ref, xl_ref, ctx_tiles, first_tile) + gt_ref[...] * acc
    xo_ref[...] = x_new
    ho_ref[...] = _rms_mod(x_new, ng_ref[...], sc_ref[...], sh_ref[...]).astype(BF16)


def _out_projection(yf, yb, bonus, g, y_p, ya_src, x_src, w_out_b, wa_b, layer, ln_g, ln_b, hs, mod3, mod_base, norm_g,
                    dims, first_tile):
    tm = dims["tm"]
    rows, d = dims["rows"], x_src[0].shape[1]
    ctx_tiles = dims["nc"] // tm
    ya_specs, ya_args = _two_source_specs(*ya_src, tm, ctx_tiles, first_tile)
    x_specs, x_args = _two_source_specs(*x_src, tm, ctx_tiles, first_tile)
    grp = dims["grp"]
    n_tiles = rows // tm - first_tile
    off = lambda w: pl.BlockSpec((tm, w), lambda i: (i + first_tile, 0))

    def mod_spec(k):
        return pl.BlockSpec((None, 1, d), lambda i: (mod_base + grp(i + first_tile, tm) * 6 + k, 0, 0))

    def y_index(ft):
        per_ctx, per_lat = dims["c_len"] // tm, dims["t_len"] // tm
        lt = jnp.maximum(ft - ctx_tiles, 0)
        in_ctx = ft < ctx_tiles
        return (jnp.where(in_ctx, ft // per_ctx, lt // per_lat),
                jnp.where(in_ctx, ft % per_ctx, per_ctx + lt % per_lat), 0)

    y_spec = pl.BlockSpec((None, tm, D_RWKV), lambda i: y_index(i + first_tile))
    vec = lambda w: pl.BlockSpec((1, w), lambda i: (0, 0))
    out_rows = n_tiles * tm
    return pl.pallas_call(
        functools.partial(_outproj_kernel, ctx_tiles=ctx_tiles, first_tile=first_tile),
        grid=(n_tiles,),
        in_specs=[y_spec, y_spec, off(D_RWKV), off(D_RWKV), off(D_POOL)] + ya_specs + x_specs + [
            pl.BlockSpec((None, d, d), lambda i: (layer, 0, 0), pipeline_mode=pl.Buffered(1)),
            pl.BlockSpec((None, D_ATT, d), lambda i: (layer, 0, 0), pipeline_mode=pl.Buffered(1)),
            vec(D_RWKV), vec(D_RWKV),
            pl.BlockSpec((MXU_DIM, MXU_DIM), lambda i: (0, 0)),
            mod_spec(2), vec(d), mod_spec(4), mod_spec(3),
        ],
        out_specs=[pl.BlockSpec((tm, d), lambda i: (i, 0)), pl.BlockSpec((tm, d), lambda i: (i, 0))],
        out_shape=[jax.ShapeDtypeStruct((out_rows, d), F32), jax.ShapeDtypeStruct((out_rows, d), BF16)],
        compiler_params=_cparams(("arbitrary",)),
        name="readout_outproj_norm",
    )(yf, yb, bonus, g, y_p, *ya_args, *x_args, w_out_b, wa_b, ln_g.reshape(1, -1), ln_b.reshape(1, -1), hs,
      mod3, norm_g.reshape(1, d), mod3, mod3)


def _ffn_kernel(h_ref, x_ref, wg_ref, wu_ref, wd_ref, gt_ref, fg_ref, o_ref, acc_ref, *, final):
    j = pl.program_id(1)

    @pl.when(j == 0)
    def _():
        acc_ref[...] = jnp.zeros(acc_ref.shape, F32)

    h = h_ref[...]
    n_sub = wg_ref.shape[1] // MXU_DIM
    slabs = [slice(MXU_DIM * q, MXU_DIM * (q + 1)) for q in range(n_sub)]
    gate_up = [(_dot(h, wg_ref[:, sl]), _dot(h, wu_ref[:, sl])) for sl in slabs]
    part = None
    for sl, (gate, up) in zip(slabs, gate_up):
        act = (gate * jax.nn.sigmoid(gate) * up).astype(BF16)
        p = _dot(act, wd_ref[sl, :])
        part = p if part is None else part + p
    acc_ref[...] += part

    @pl.when(j == pl.num_programs(1) - 1)
    def _():
        x_new = x_ref[...] + gt_ref[...] * acc_ref[...]
        if final:
            ms = jnp.mean(x_new * x_new, axis=-1, keepdims=True)
            x_new = x_new * lax.rsqrt(ms + NORM_EPS) * fg_ref[...]
        o_ref[...] = x_new


def _ffn(h2, x_new, w_gu_b, w_down_b, layer, mod3, mod_base, final_g, dims, first_tile, final):
    rows, d = x_new.shape
    tmf, th = dims["tmf"], dims["th"]
    hidden = w_down_b.shape[1]
    nh = hidden // th
    grp = dims["grp"]
    return pl.pallas_call(
        functools.partial(_ffn_kernel, final=final),
        grid=(rows // tmf, nh),
        in_specs=[
            pl.BlockSpec((tmf, d), lambda i, j: (i, 0)),
            pl.BlockSpec((tmf, d), lambda i, j: (i, 0)),
            pl.BlockSpec((None, d, th), lambda i, j: (layer, 0, j)),
            pl.BlockSpec((None, d, th), lambda i, j: (layer, 0, nh + j)),
            pl.BlockSpec((None, th, d), lambda i, j: (layer, j, 0)),
            pl.BlockSpec((None, 1, d), lambda i, j: (mod_base + grp(i + first_tile, tmf) * 6 + 5, 0, 0)),
            pl.BlockSpec((1, d), lambda i, j: (0, 0)),
        ],
        out_specs=pl.BlockSpec((tmf, d), lambda i, j: (i, 0)),
        out_shape=jax.ShapeDtypeStruct((rows, d), F32),
        scratch_shapes=[pltpu.VMEM((tmf, d), F32)],
        compiler_params=_cparams(("arbitrary", "arbitrary")),
        name="swiglu_ffn_final" if final else "swiglu_ffn",
    )(h2, x_new, w_gu_b, w_gu_b, w_down_b, mod3, final_g.reshape(1, d))


def _rope_tables(dims):
    t_len, nc, n_b = dims["t_len"], dims["nc"], dims["batch"]
    grid_rows = t_len // GRID_W
    n_freq = HEAD_DIM // 4
    inv = ROPE_BASE ** (-jnp.arange(n_freq, dtype=F32) / n_freq)
    ang_r = jnp.arange(grid_rows).astype(F32)[:, None] * inv
    ang_c = jnp.arange(GRID_W).astype(F32)[:, None] * inv

    def per_token(fn):
        by_row = jnp.repeat(fn(ang_r), GRID_W, axis=0)
        by_col = jnp.tile(fn(ang_c), (grid_rows, 1))
        return jnp.tile(jnp.concatenate([by_row, by_row, by_col, by_col], axis=-1), (1, LANES // HEAD_DIM))

    cos, sin = per_token(jnp.cos), per_token(jnp.sin)
    first_half = (jnp.arange(LANES) % 32) < 16
    sin_a = jnp.where(first_half, -sin, 0.0)
    sin_b = jnp.where(first_half, 0.0, sin)
    flat = lambda tab, fill: jnp.concatenate([jnp.full((nc, LANES), fill, F32), jnp.tile(tab, (n_b, 1))], axis=0)
    return flat(cos, 1.0), flat(sin_a, 0.0), flat(sin_b, 0.0)


def _block_diag2(w):
    z = jnp.zeros_like(w[0])
    return jnp.concatenate([jnp.concatenate([w[0], z], axis=1), jnp.concatenate([z, w[1]], axis=1)], axis=0)


def kernel(x, c, ctx, c_ctx, ada_w, ada_b, norm_mix_g, norm_ffn_g, w_in, rwkv_mu, rwkv_w0, rwkv_w_up, rwkv_a0,
           rwkv_a_up, rwkv_g_up, rwkv_k_k, rwkv_k_a, rwkv_r_k, rwkv_ln_g, rwkv_ln_b, pool_w, pool_scale,
           attn_sink, w_out, ffn_w_gu, ffn_w_down, final_norm_g):
    n_b, t_len, d = x.shape
    c_len = ctx.shape[1]
    depth = w_in.shape[0]
    nc = n_b * c_len
    rows = nc + n_b * t_len
    tm = 256 if c_len % 256 == 0 else c_len
    tmf = nc
    assert n_b + 1 <= SUBLANES and c_len % tm == 0 and t_len % tm == 0 and t_len % tmf == 0
    assert tm % ATT_BLOCK == 0 and c_len % CHUNK == 0 and t_len % GRID_W == 0 and d == w_out.shape[1]

    def grp(i, tile):
        r0 = i * tile
        return jnp.where(r0 < nc, 0, 1 + jnp.maximum(r0 - nc, 0) // t_len)

    hidden = ffn_w_down.shape[1]
    th = 512 if hidden % 512 == 0 else hidden
    dims = dict(batch=n_b, c_len=c_len, t_len=t_len, nc=nc, rows=rows, tm=tm, tmf=tmf, th=th, grp=grp)

    cvec = jnp.concatenate([c_ctx[None, :], c, jnp.zeros((SUBLANES - 1 - n_b, d), F32)], axis=0)
    mod = _modulation(cvec, ada_w, ada_b)
    mod3 = mod.reshape(depth * SUBLANES * 6, 1, d)

    x_src = (ctx.reshape(nc, d), x.reshape(n_b * t_len, d), 0)
    rope = _rope_tables(dims)
    hs = jnp.asarray(np.kron(np.eye(HEADS_PER_GROUP), np.ones((HEAD_DIM, HEAD_DIM))), BF16)

    w_in_b = w_in.astype(BF16)
    wq_b = w_in_b[:, :, OFF_AQ:OFF_AK].reshape(depth, d, ATT_KV_HEADS, ATT_GROUP, HEAD_DIM)
    wq_b = wq_b.transpose(0, 1, 3, 2, 4).reshape(depth, d, D_ATT)
    w_out_b = w_out.astype(BF16)
    wa_b = w_out_b[:, D_RWKV + D_POOL:].reshape(depth, ATT_KV_HEADS, ATT_GROUP, HEAD_DIM, d)
    wa_b = wa_b.transpose(0, 2, 1, 3, 4).reshape(depth, D_ATT, d)

    out = None
    for l in range(depth):
        last = l == depth - 1
        mod_base = l * SUBLANES * 6
        rw = dict(
            mu=rwkv_mu[l].reshape(1, RWKV_COLS),
            w0=rwkv_w0[l].reshape(1, 2 * D_RWKV),
            w_up=_block_diag2(rwkv_w_up[l]).astype(BF16),
            a0=rwkv_a0[l].reshape(1, 2 * D_RWKV),
            a_up=_block_diag2(rwkv_a_up[l]).astype(BF16),
            g_up=rwkv_g_up[l].astype(BF16),
            k_k=rwkv_k_k[l].reshape(1, D_RWKV),
            k_a=rwkv_k_a[l].reshape(1, D_RWKV),
            r_k=rwkv_r_k[l].reshape(1, D_RWKV),
        )

        z_r, z_p, q, kv = _in_projection(x_src, mod3, mod_base, norm_mix_g[l], w_in_b, wq_b, l, rope, dims)
        rva, g, bonus, kdb, lw, y_p = _rwkv_prepare_and_pool(z_r, rw, hs, z_p, pool_w[l], pool_scale[l], dims)
        yf, yb, w_gu_b, w_down_b = _rwkv_scan(rva, kdb, lw, ffn_w_gu, ffn_w_down, l, dims)
        ya_lat = _attention(q, kv, attn_sink[l], dims, windowed=True)
        ya_ctx = ya_lat if last else _attention(q, kv, attn_sink[l], dims, windowed=False)
        first_tile = nc // tm if last else 0
        x_new, h2 = _out_projection(yf, yb, bonus, g, y_p, (ya_ctx, ya_lat, 0), x_src, w_out_b, wa_b, l,
                                    rwkv_ln_g[l], rwkv_ln_b[l], hs, mod3, mod_base, norm_ffn_g[l], dims, first_tile)
        x_next = _ffn(h2, x_new, w_gu_b[None], w_down_b[None], 0, mod3, mod_base, final_norm_g, dims,
                      first_tile * tm // tmf, final=last)
        if last:
            out = x_next.reshape(n_b, t_len, d)
        else:
            x_src = (x_next, x_next, nc // tm)
    return out
```
